```python
import math
import jax, jax.numpy as jnp
from jax import lax
import numpy as np

D_MODEL = 1024
BATCH = 8
SEQ = 2048
DEPTH = 4

HEAD_DIM = 64
MIX_WIDTH = D_MODEL
SSM_WIDTH = 3 * D_MODEL // 8
SSM_GROUP_CH = 16
SSM_GROUPS = SSM_WIDTH // SSM_GROUP_CH
SSM_STATE = 64
DIL_WIDTH = 3 * D_MODEL // 8
DIL_HEADS = DIL_WIDTH // HEAD_DIM
DIL_PATTERNS = ((128, 1), (512, 4), (2048, 16))
DIFF_WIDTH = MIX_WIDTH - SSM_WIDTH - DIL_WIDTH
DIFF_HEADS = DIFF_WIDTH // HEAD_DIM
DIFF_QK_DIM = HEAD_DIM // 2
IN_WIDTH = SSM_WIDTH + 3 * DIL_WIDTH + 3 * DIFF_WIDTH
Q_BLOCK = 128
PEER_HEADS = 8
PEER_KEYS = 128
PEER_EXPERTS = PEER_KEYS * PEER_KEYS
PEER_TOPK = 16
PEER_QUERY_DIM = 256
PEER_HALF = PEER_QUERY_DIM // 2
PEER_TOKEN_BLOCK = 128
NORM_EPS = 1e-6

kernel_name = "hybrid_s5_dilated_diffattn_peer"


def rms_norm(x, gain):
    xf = x.astype(jnp.float32)
    y = xf * lax.rsqrt(jnp.mean(xf * xf, axis=-1, keepdims=True) + NORM_EPS)
    return (y * gain.astype(jnp.float32)).astype(x.dtype)


def _linear_recurrence_combine(left, right):
    a_l, b_l = left
    a_r, b_r = right
    return a_r * a_l, a_r * b_l + b_r


def s5_mixer(u, lam_re, lam_im, log_step, b_re, b_im, c_re, c_im, d_skip):
    f32 = jnp.float32
    bsz, seq, _ = u.shape
    uf = u.astype(f32).reshape(bsz, seq, SSM_GROUPS, SSM_GROUP_CH)
    lam = lax.complex(lam_re.astype(f32), lam_im.astype(f32))
    step = jnp.exp(log_step.astype(f32))[:, None]
    lam_bar = jnp.exp(lam * step)
    b_mat = lax.complex(b_re.astype(f32), b_im.astype(f32))
    b_bar = ((lam_bar - 1.0) / lam)[:, :, None] * b_mat
    bu = jnp.einsum("gpc,blgc->blgp", b_bar, uf.astype(jnp.complex64))
    decay = jnp.broadcast_to(lam_bar, bu.shape)
    _, states = lax.associative_scan(_linear_recurrence_combine, (decay, bu), axis=1)
    c_mat = lax.complex(c_re.astype(f32), c_im.astype(f32))
    y = jnp.real(jnp.einsum("gcp,blgp->blgc", c_mat, states))
    y = y + d_skip.astype(f32).reshape(SSM_GROUPS, SSM_GROUP_CH) * uf
    return y.reshape(bsz, seq, SSM_WIDTH)


def dilated_attention(q, k, v):
    f32 = jnp.float32
    bsz, seq, _ = q.shape
    q = q.reshape(bsz, seq, DIL_HEADS, HEAD_DIM).transpose(0, 2, 1, 3)
    k = k.reshape(bsz, seq, DIL_HEADS, HEAD_DIM).transpose(0, 2, 1, 3)
    v = v.reshape(bsz, seq, DIL_HEADS, HEAD_DIM).transpose(0, 2, 1, 3)
    n_blk = seq // Q_BLOCK
    q_blocks = q.reshape(bsz, DIL_HEADS, n_blk, Q_BLOCK, HEAD_DIM).transpose(2, 0, 1, 3, 4)
    scale = HEAD_DIM ** -0.5

    def one_block(args):
        qb, blk = args
        pos = blk * Q_BLOCK + jnp.arange(Q_BLOCK)
        outs, lses = [], []
        for window, dil in DIL_PATTERNS:
            offs = dil * jnp.arange(window // dil + 1)
            idx = pos[:, None] - offs[None, :]
            valid = idx >= 0
            idx = jnp.maximum(idx, 0)
            kg = k[:, :, idx]
            vg = v[:, :, idx]
            s = jnp.einsum("bhqd,bhqjd->bhqj", qb, kg).astype(f32) * scale
            s = jnp.where(valid, s, -jnp.inf)
            m = jnp.max(s, axis=-1, keepdims=True)
            p = jnp.exp(s - m)
            den = jnp.sum(p, axis=-1, keepdims=True)
            outs.append(jnp.einsum("bhqj,bhqjd->bhqd", p / den, vg.astype(f32)))
            lses.append(m[..., 0] + jnp.log(den[..., 0]))
        weights = jax.nn.softmax(jnp.stack(lses), axis=0)
        return jnp.sum(weights[..., None] * jnp.stack(outs), axis=0)

    out = lax.map(one_block, (q_blocks, jnp.arange(n_blk)))
    return out.transpose(1, 0, 3, 2, 4).reshape(bsz, seq, DIL_WIDTH)


def diff_attention(q, k, v, lam_q1, lam_k1, lam_q2, lam_k2, subln_g, lambda_init):
    f32 = jnp.float32
    bsz, seq, _ = q.shape
    q = q.reshape(bsz, seq, DIFF_HEADS, 2, DIFF_QK_DIM).transpose(0, 2, 3, 1, 4)
    k = k.reshape(bsz, seq, DIFF_HEADS, 2, DIFF_QK_DIM).transpose(0, 2, 3, 1, 4)
    v = v.reshape(bsz, seq, DIFF_HEADS, HEAD_DIM).transpose(0, 2, 1, 3).astype(f32)
    lam = (jnp.exp(jnp.sum(lam_q1.astype(f32) * lam_k1.astype(f32)))
           - jnp.exp(jnp.sum(lam_q2.astype(f32) * lam_k2.astype(f32))) + lambda_init)
    n_blk = seq // Q_BLOCK
    q_blocks = q.reshape(bsz, DIFF_HEADS, 2, n_blk, Q_BLOCK, DIFF_QK_DIM).transpose(3, 0, 1, 2, 4, 5)
    key_pos = jnp.arange(seq)
    scale = DIFF_QK_DIM ** -0.5

    def one_block(args):
        qb, blk = args
        pos = blk * Q_BLOCK + jnp.arange(Q_BLOCK)
        s = jnp.einsum("bhmqd,bhmkd->bhmqk", qb, k).astype(f32) * scale
        s = jnp.where(key_pos[None, :] <= pos[:, None], s, -jnp.inf)
        p = jax.nn.softmax(s, axis=-1)
        attn = p[:, :, 0] - lam * p[:, :, 1]
        return jnp.einsum("bhqk,bhkd->bhqd", attn, v)

    out = lax.map(one_block, (q_blocks, jnp.arange(n_blk)))
    out = out.transpose(1, 0, 3, 2, 4)
    out = rms_norm(out, subln_g) * (1.0 - lambda_init)
    return out.reshape(bsz, seq, DIFF_WIDTH)


def hybrid_mixer(h, w_in, lam_re, lam_im, log_step, b_re, b_im, c_re, c_im, d_skip,
                 w_glu, ssm_norm_g, dil_norm_g, lam_q1, lam_k1, lam_q2, lam_k2, subln_g,
                 w_out, lambda_init):
    proj = h @ w_in
    cuts = [SSM_WIDTH,
            SSM_WIDTH + DIL_WIDTH,
            SSM_WIDTH + 2 * DIL_WIDTH,
            SSM_WIDTH + 3 * DIL_WIDTH,
            SSM_WIDTH + 3 * DIL_WIDTH + DIFF_WIDTH,
            SSM_WIDTH + 3 * DIL_WIDTH + 2 * DIFF_WIDTH]
    u_ssm, dq, dk, dv, fq, fk, fv = jnp.split(proj, cuts, axis=-1)
    y = s5_mixer(u_ssm, lam_re, lam_im, log_step, b_re, b_im, c_re, c_im, d_skip)
    g = jax.nn.gelu(y)
    y = g * jax.nn.sigmoid(g @ w_glu.astype(jnp.float32))
    y_ssm = rms_norm(y, ssm_norm_g).astype(h.dtype)
    y_dil = rms_norm(dilated_attention(dq, dk, dv), dil_norm_g).astype(h.dtype)
    y_diff = diff_attention(fq, fk, fv, lam_q1, lam_k1, lam_q2, lam_k2, subln_g,
                            lambda_init).astype(h.dtype)
    return jnp.concatenate([y_ssm, y_dil, y_diff], axis=-1) @ w_out


def peer_ffn(h, w_query, sub_keys, u_exp, v_exp):
    f32 = jnp.float32
    bsz, seq, dm = h.shape
    flat = h.reshape(-1, dm)
    n_chunks = flat.shape[0] // PEER_TOKEN_BLOCK
    chunks = flat.reshape(n_chunks, PEER_TOKEN_BLOCK, dm)

    def one_chunk(xc):
        q = (xc @ w_query).reshape(PEER_TOKEN_BLOCK, PEER_HEADS, 2, PEER_HALF)
        s = jnp.einsum("thcd,hcnd->thcn", q, sub_keys).astype(f32)
        top_s, top_i = lax.top_k(s, PEER_TOPK)
        cand_s = top_s[:, :, 0, :, None] + top_s[:, :, 1, None, :]
        cand_i = top_i[:, :, 0, :, None] * PEER_KEYS + top_i[:, :, 1, None, :]
        cand_s = cand_s.reshape(PEER_TOKEN_BLOCK, PEER_HEADS, PEER_TOPK * PEER_TOPK)
        cand_i = cand_i.reshape(PEER_TOKEN_BLOCK, PEER_HEADS, PEER_TOPK * PEER_TOPK)
        best_s, best_j = lax.top_k(cand_s, PEER_TOPK)
        expert = jnp.take_along_axis(cand_i, best_j, axis=-1)
        gate = jax.nn.softmax(best_s, axis=-1)
        u = u_exp[expert]
        act = jax.nn.gelu(jnp.einsum("thkd,td->thk", u, xc).astype(f32))
        vv = v_exp[expert]
        return jnp.einsum("thk,thkd->td", (gate * act).astype(vv.dtype), vv).astype(xc.dtype)

    out = lax.map(one_chunk, chunks)
    return out.reshape(bsz, seq, dm)


def setup_inputs(seed: int = 0) -> dict:
    key = jax.random.key(seed)
    ks = jax.random.split(key, 26)
    f32 = jnp.float32

    def nrm(k, shape, scale):
        return scale * jax.random.normal(k, shape, f32)

    def gain(k, shape):
        return 1.0 + 0.02 * jax.random.normal(k, shape, f32)

    L = DEPTH
    x = jax.random.normal(ks[0], (BATCH, SEQ, D_MODEL), f32)
    norm1_g = gain(ks[1], (L, D_MODEL))
    w_in = nrm(ks[2], (L, D_MODEL, IN_WIDTH), D_MODEL ** -0.5)
    ssm_lam_re = -0.5 + nrm(ks[3], (L, SSM_GROUPS, SSM_STATE), 0.01)
    ssm_lam_im = jnp.pi * jnp.arange(SSM_STATE, dtype=f32) + nrm(ks[4], (L, SSM_GROUPS, SSM_STATE), 0.01)
    ssm_log_step = jax.random.uniform(ks[5], (L, SSM_GROUPS), f32,
                                      minval=math.log(1e-3), maxval=math.log(1e-1))
    ssm_b_re = nrm(ks[6], (L, SSM_GROUPS, SSM_STATE, SSM_GROUP_CH), (2 * SSM_GROUP_CH) ** -0.5)
    ssm_b_im = nrm(ks[7], (L, SSM_GROUPS, SSM_STATE, SSM_GROUP_CH), (2 * SSM_GROUP_CH) ** -0.5)
    ssm_c_re = nrm(ks[8], (L, SSM_GROUPS, SSM_GROUP_CH, SSM_STATE), SSM_STATE ** -0.5)
    ssm_c_im = nrm(ks[9], (L, SSM_GROUPS, SSM_GROUP_CH, SSM_STATE), SSM_STATE ** -0.5)
    ssm_d = nrm(ks[10], (L, SSM_WIDTH), 1.0)
    ssm_w_glu = nrm(ks[11], (L, SSM_WIDTH, SSM_WIDTH), SSM_WIDTH ** -0.5)
    ssm_norm_g = gain(ks[12], (L, SSM_WIDTH))
    dil_norm_g = gain(ks[13], (L, DIL_WIDTH))
    diff_lam_q1 = nrm(ks[14], (L, DIFF_QK_DIM), 0.1)
    diff_lam_k1 = nrm(ks[15], (L, DIFF_QK_DIM), 0.1)
    diff_lam_q2 = nrm(ks[16], (L, DIFF_QK_DIM), 0.1)
    diff_lam_k2 = nrm(ks[17], (L, DIFF_QK_DIM), 0.1)
    diff_subln_g = gain(ks[18], (L, HEAD_DIM))
    w_out = nrm(ks[19], (L, MIX_WIDTH, D_MODEL), MIX_WIDTH ** -0.5)
    norm2_g = gain(ks[20], (L, D_MODEL))
    peer_w_query = nrm(ks[21], (L, D_MODEL, PEER_HEADS * PEER_QUERY_DIM), D_MODEL ** -0.5)
    peer_sub_keys = nrm(ks[22], (L, PEER_HEADS, 2, PEER_KEYS, PEER_HALF), PEER_HALF ** -0.5)
    peer_u = nrm(ks[23], (L, PEER_EXPERTS, D_MODEL), D_MODEL ** -0.5)
    peer_v = nrm(ks[24], (L, PEER_EXPERTS, D_MODEL), PEER_HEADS ** -0.5)
    final_norm_g = gain(ks[25], (D_MODEL,))
    return {"x": x, "norm1_g": norm1_g, "w_in": w_in,
            "ssm_lam_re": ssm_lam_re, "ssm_lam_im": ssm_lam_im, "ssm_log_step": ssm_log_step,
            "ssm_b_re": ssm_b_re, "ssm_b_im": ssm_b_im, "ssm_c_re": ssm_c_re, "ssm_c_im": ssm_c_im,
            "ssm_d": ssm_d, "ssm_w_glu": ssm_w_glu, "ssm_norm_g": ssm_norm_g, "dil_norm_g": dil_norm_g,
            "diff_lam_q1": diff_lam_q1, "diff_lam_k1": diff_lam_k1, "diff_lam_q2": diff_lam_q2,
            "diff_lam_k2": diff_lam_k2, "diff_subln_g": diff_subln_g, "w_out": w_out,
            "norm2_g": norm2_g, "peer_w_query": peer_w_query, "peer_sub_keys": peer_sub_keys,
            "peer_u": peer_u, "peer_v": peer_v, "final_norm_g": final_norm_g}


def reference(x, norm1_g, w_in, ssm_lam_re, ssm_lam_im, ssm_log_step, ssm_b_re, ssm_b_im,
              ssm_c_re, ssm_c_im, ssm_d, ssm_w_glu, ssm_norm_g, dil_norm_g, diff_lam_q1,
              diff_lam_k1, diff_lam_q2, diff_lam_k2, diff_subln_g, w_out, norm2_g,
              peer_w_query, peer_sub_keys, peer_u, peer_v, final_norm_g):
    for layer in range(DEPTH):
        lambda_init = 0.8 - 0.6 * math.exp(-0.3 * layer)
        h = rms_norm(x, norm1_g[layer])
        x = x + hybrid_mixer(h, w_in[layer], ssm_lam_re[layer], ssm_lam_im[layer],
                             ssm_log_step[layer], ssm_b_re[layer], ssm_b_im[layer],
                             ssm_c_re[layer], ssm_c_im[layer], ssm_d[layer], ssm_w_glu[layer],
                             ssm_norm_g[layer], dil_norm_g[layer], diff_lam_q1[layer],
                             diff_lam_k1[layer], diff_lam_q2[layer], diff_lam_k2[layer],
                             diff_subln_g[layer], w_out[layer], lambda_init)
        h = rms_norm(x, norm2_g[layer])
        x = x + peer_ffn(h, peer_w_query[layer], peer_sub_keys[layer], peer_u[layer], peer_v[layer])
    return rms_norm(x, final_norm_g)
```

```python
import functools
import math

import jax
import jax.numpy as jnp
from jax import lax
from jax.experimental import pallas as pl
from jax.experimental.pallas import tpu as pltpu

F32 = jnp.float32
BF16 = jnp.bfloat16

D_MODEL = 1024
HEAD_DIM = 64
SSM_WIDTH = 384
SSM_GROUP_CH = 16
SSM_GROUPS = 24
SSM_STATE = 64
SSM_HALF = SSM_GROUPS * SSM_STATE
DIL_WIDTH = 384
DIL_HEADS = 6
DIL_DILATIONS = (1, 4, 16)
DIFF_WIDTH = 256
DIFF_HEADS = 4
DIFF_QK_DIM = 32
IN_WIDTH = 2304
PEER_HEADS = 8
PEER_KEYS = 128
PEER_TOPK = 16
PEER_EXPERTS = PEER_KEYS * PEER_KEYS
NORM_EPS = 1e-6

LANES = 128
SUBLANES = 8
VMEM_LIMIT_BYTES = 56 * 1024 * 1024

PROJ_TOKENS = 512
SSM_CHUNK = 128
ATT_BLOCK = 128
DIFF_BLOCK = 256
PEER_TOKENS = 1024
PEER_GATE_CHUNK = 256
PEER_EXPERT_BLOCK = 256

NEG_INF = float("-inf")


def _rms(x, gain):
    return x * lax.rsqrt(jnp.mean(x * x, axis=-1, keepdims=True) + NORM_EPS) * gain


def _dot_nt(a, b):
    return lax.dot_general(a, b, (((1,), (1,)), ((), ())), preferred_element_type=F32)


def _params(semantics):
    return pltpu.CompilerParams(dimension_semantics=semantics, vmem_limit_bytes=VMEM_LIMIT_BYTES)


def _in_proj_kernel(x_ref, g_ref, w_ref, scale_ref, u_ref, dil_ref, diff_ref):
    h = _rms(x_ref[...], g_ref[...])
    p = jnp.dot(h.astype(BF16), w_ref[...], preferred_element_type=F32) * scale_ref[...]
    u_ref[...] = p[:, :SSM_WIDTH].astype(BF16)
    dil_ref[...] = p[:, SSM_WIDTH:SSM_WIDTH + 3 * DIL_WIDTH].astype(BF16)
    diff_ref[...] = p[:, SSM_WIDTH + 3 * DIL_WIDTH:].astype(BF16)


def _in_proj(x2, gain, w_in, col_scale, batch, seq):
    n = batch * seq
    blocks_per_seq = seq // PROJ_TOKENS
    return pl.pallas_call(
        _in_proj_kernel,
        grid=(n // PROJ_TOKENS,),
        in_specs=[
            pl.BlockSpec((PROJ_TOKENS, D_MODEL), lambda i: (i, 0)),
            pl.BlockSpec((1, D_MODEL), lambda i: (0, 0)),
            pl.BlockSpec((D_MODEL, IN_WIDTH), lambda i: (0, 0)),
            pl.BlockSpec((1, IN_WIDTH), lambda i: (0, 0)),
        ],
        out_specs=[
            pl.BlockSpec((PROJ_TOKENS, SSM_WIDTH),
                         lambda i: (i % blocks_per_seq, i // blocks_per_seq)),
            pl.BlockSpec((PROJ_TOKENS, 3 * DIL_WIDTH), lambda i: (i, 0)),
            pl.BlockSpec((PROJ_TOKENS, 3 * DIFF_WIDTH), lambda i: (i, 0)),
        ],
        out_shape=[
            jax.ShapeDtypeStruct((seq, batch * SSM_WIDTH), BF16),
            jax.ShapeDtypeStruct((n, 3 * DIL_WIDTH), BF16),
            jax.ShapeDtypeStruct((n, 3 * DIFF_WIDTH), BF16),
        ],
        compiler_params=_params(("arbitrary",)),
        name="in_proj",
    )(x2, gain, w_in, col_scale)


def _ssm_kernel(u_ref, bmat_ref, cmat_ref, lre_ref, lim_ref, d_ref, wglu_ref, g_ref,
                y_ref, s_ref, state_ref):
    rows = u_ref.shape[0]
    sub = 256

    @pl.when(pl.program_id(0) == 0)
    def _():
        state_ref[...] = jnp.zeros_like(state_ref)

    for c in range(rows // sub):
        s_ref[c * sub:(c + 1) * sub, :] = jnp.dot(
            u_ref[c * sub:(c + 1) * sub, :], bmat_ref[...], preferred_element_type=F32)

    lre = lre_ref[...]
    lim = lim_ref[...]

    def step(t, carry):
        sre, sim = carry
        r0 = pl.multiple_of(t * SUBLANES, SUBLANES)
        bre = s_ref[pl.ds(r0, SUBLANES), :SSM_HALF]
        bim = s_ref[pl.ds(r0, SUBLANES), SSM_HALF:]
        nre = lre * sre - lim * sim + bre
        nim = lre * sim + lim * sre + bim
        s_ref[pl.ds(r0, SUBLANES), :SSM_HALF] = nre
        s_ref[pl.ds(r0, SUBLANES), SSM_HALF:] = nim
        return nre, nim

    sre, sim = lax.fori_loop(0, rows // SUBLANES, step,
                             (state_ref[:, :SSM_HALF], state_ref[:, SSM_HALF:]), unroll=2)
    state_ref[:, :SSM_HALF] = sre
    state_ref[:, SSM_HALF:] = sim

    for c in range(rows // sub):
        sl = slice(c * sub, (c + 1) * sub)
        y = jnp.dot(s_ref[sl, :].astype(BF16), cmat_ref[...], preferred_element_type=F32)
        y = y + d_ref[...] * u_ref[sl, :].astype(F32)
        g = jax.nn.gelu(y)
        z = jnp.dot(g.astype(BF16), wglu_ref[...], preferred_element_type=F32)
        y = g * jax.nn.sigmoid(z)
        y_ref[sl, :] = _rms(y, g_ref[...]).astype(BF16)


def _ssm(u_tm, bmat, cmat, lre, lim, d_skip, w_glu, gain, batch, seq):
    assert batch == SUBLANES, "the S5 scan keeps one batch per sublane"
    rows = SSM_CHUNK * batch
    const = lambda i: (0, 0)
    return pl.pallas_call(
        _ssm_kernel,
        grid=(seq // SSM_CHUNK,),
        in_specs=[
            pl.BlockSpec((rows, SSM_WIDTH), lambda i: (i, 0)),
            pl.BlockSpec((SSM_WIDTH, 2 * SSM_HALF), const),
            pl.BlockSpec((2 * SSM_HALF, SSM_WIDTH), const),
            pl.BlockSpec((SUBLANES, SSM_HALF), const),
            pl.BlockSpec((SUBLANES, SSM_HALF), const),
            pl.BlockSpec((1, SSM_WIDTH), const),
            pl.BlockSpec((SSM_WIDTH, SSM_WIDTH), const),
            pl.BlockSpec((1, SSM_WIDTH), const),
        ],
        out_specs=pl.BlockSpec((rows, SSM_WIDTH), lambda i: (i, 0)),
        out_shape=jax.ShapeDtypeStruct((seq * batch, SSM_WIDTH), BF16),
        scratch_shapes=[
            pltpu.VMEM((rows, 2 * SSM_HALF), F32),
            pltpu.VMEM((SUBLANES, 2 * SSM_HALF), F32),
        ],
        compiler_params=_params(("arbitrary",)),
        name="ssm",
    )(u_tm, bmat, cmat, lre, lim, d_skip, w_glu, gain)


def _band_kernel(has_prev, *refs):
    if has_prev:
        q_ref, kc_ref, vc_ref, kp_ref, vp_ref, o_ref, lse_ref = refs
    else:
        q_ref, kc_ref, vc_ref, o_ref, lse_ref = refs
    blk = pl.program_id(2)
    row = lax.broadcasted_iota(jnp.int32, (ATT_BLOCK, ATT_BLOCK), 0)
    col = lax.broadcasted_iota(jnp.int32, (ATT_BLOCK, ATT_BLOCK), 1)
    cur_ok = col <= row
    if has_prev:
        prev_ok = jnp.logical_and(col >= row, blk > 0)
    lane = lax.broadcasted_iota(jnp.int32, (ATT_BLOCK, LANES), 1)
    lse_tile = jnp.zeros((ATT_BLOCK, LANES), F32)
    for h in range(DIL_HEADS):
        hs = slice(h * HEAD_DIM, (h + 1) * HEAD_DIM)
        q = q_ref[0, :, hs]
        sc = jnp.where(cur_ok, _dot_nt(q, kc_ref[0, :, hs]), NEG_INF)
        m = jnp.max(sc, axis=-1, keepdims=True)
        if has_prev:
            sp = jnp.where(prev_ok, _dot_nt(q, kp_ref[0, :, hs]), NEG_INF)
            m = jnp.maximum(m, jnp.max(sp, axis=-1, keepdims=True))
        pc = jnp.exp(sc - m)
        den = jnp.sum(pc, axis=-1, keepdims=True)
        acc = jnp.dot(pc.astype(BF16), vc_ref[0, :, hs], preferred_element_type=F32)
        if has_prev:
            pp = jnp.exp(sp - m)
            den = den + jnp.sum(pp, axis=-1, keepdims=True)
            acc = acc + jnp.dot(pp.astype(BF16), vp_ref[0, :, hs], preferred_element_type=F32)
        o_ref[0, :, hs] = (acc / den).astype(BF16)
        lse_tile = jnp.where(lane == h, m + jnp.log(den), lse_tile)
    lse_ref[0] = lse_tile


def _band_attention(qkv, dilation, batch, seq):
    rows = seq // dilation
    nblk = rows // ATT_BLOCK
    has_prev = nblk > 1
    view = qkv.reshape(batch, rows, dilation * 3 * DIL_WIDTH)
    blk = (1, ATT_BLOCK, DIL_WIDTH)
    in_specs = [
        pl.BlockSpec(blk, lambda b, r, i: (b, i, 3 * r)),
        pl.BlockSpec(blk, lambda b, r, i: (b, i, 3 * r + 1)),
        pl.BlockSpec(blk, lambda b, r, i: (b, i, 3 * r + 2)),
    ]
    args = [view, view, view]
    if has_prev:
        in_specs += [
            pl.BlockSpec(blk, lambda b, r, i: (b, jnp.maximum(i - 1, 0), 3 * r + 1)),
            pl.BlockSpec(blk, lambda b, r, i: (b, jnp.maximum(i - 1, 0), 3 * r + 2)),
        ]
        args += [view, view]
    out, lse = pl.pallas_call(
        functools.partial(_band_kernel, has_prev),
        grid=(batch, dilation, nblk),
        in_specs=in_specs,
        out_specs=[
            pl.BlockSpec(blk, lambda b, r, i: (b, i, r)),
            pl.BlockSpec((1, ATT_BLOCK, LANES), lambda b, r, i: (b, i, r)),
        ],
        out_shape=[
            jax.ShapeDtypeStruct((batch, rows, dilation * DIL_WIDTH), BF16),
            jax.ShapeDtypeStruct((batch, rows, dilation * LANES), F32),
        ],
        compiler_params=_params(("arbitrary", "arbitrary", "arbitrary")),
        name=f"dilated_d{dilation}",
    )(*args)
    n = batch * seq
    return out.reshape(n, DIL_WIDTH), lse.reshape(n, LANES)


def _diff_kernel(out_scale, lam_ref, q_ref, k_ref, v_ref, g_ref, o_ref):
    seq = q_ref.shape[0]
    nblk = seq // DIFF_BLOCK
    lam = lam_ref[0]
    row = lax.broadcasted_iota(jnp.int32, (DIFF_BLOCK, DIFF_BLOCK), 0)
    col = lax.broadcasted_iota(jnp.int32, (DIFF_BLOCK, DIFF_BLOCK), 1)
    causal = col <= row

    def attend(q, k, v, carry, mask):
        m, l, acc = carry
        s = _dot_nt(q, k)
        if mask:
            s = jnp.where(causal, s, NEG_INF)
        m_new = jnp.maximum(m, jnp.max(s, axis=-1, keepdims=True))
        alpha = jnp.exp(m - m_new)
        p = jnp.exp(s - m_new)
        l = alpha * l + jnp.sum(p, axis=-1, keepdims=True)
        acc = alpha * acc + jnp.dot(p.astype(BF16), v, preferred_element_type=F32)
        return m_new, l, acc

    for h in range(DIFF_HEADS):
        c1 = slice(h * HEAD_DIM, h * HEAD_DIM + DIFF_QK_DIM)
        c2 = slice(h * HEAD_DIM + DIFF_QK_DIM, (h + 1) * HEAD_DIM)
        cv = slice(h * HEAD_DIM, (h + 1) * HEAD_DIM)

        def q_block(qi, _):
            q0 = pl.multiple_of(qi * DIFF_BLOCK, DIFF_BLOCK)
            q1 = q_ref[pl.ds(q0, DIFF_BLOCK), c1]
            q2 = q_ref[pl.ds(q0, DIFF_BLOCK), c2]
            init = (jnp.full((DIFF_BLOCK, 1), NEG_INF, F32),
                    jnp.zeros((DIFF_BLOCK, 1), F32),
                    jnp.zeros((DIFF_BLOCK, HEAD_DIM), F32))

            def k_block(kj, carry):
                k0 = pl.multiple_of(kj * DIFF_BLOCK, DIFF_BLOCK)
                v = v_ref[pl.ds(k0, DIFF_BLOCK), cv]
                s1 = attend(q1, k_ref[pl.ds(k0, DIFF_BLOCK), c1], v, carry[0], False)
                s2 = attend(q2, k_ref[pl.ds(k0, DIFF_BLOCK), c2], v, carry[1], False)
                return s1, s2

            carry = lax.fori_loop(0, qi, k_block, (init, init))
            v = v_ref[pl.ds(q0, DIFF_BLOCK), cv]
            _, l1, a1 = attend(q1, k_ref[pl.ds(q0, DIFF_BLOCK), c1], v, carry[0], True)
            _, l2, a2 = attend(q2, k_ref[pl.ds(q0, DIFF_BLOCK), c2], v, carry[1], True)
            o = a1 / l1 - lam * (a2 / l2)
            o = _rms(o, g_ref[...]) * out_scale
            o_ref[pl.ds(q0, DIFF_BLOCK), cv] = o.astype(BF16)
            return 0

        lax.fori_loop(0, nblk, q_block, 0)


def _diff_attention(qkv, lam, subln_g, out_scale, batch, seq):
    n = batch * seq
    blk = (seq, DIFF_WIDTH)
    return pl.pallas_call(
        functools.partial(_diff_kernel, out_scale),
        grid=(batch,),
        in_specs=[
            pl.BlockSpec(memory_space=pltpu.SMEM),
            pl.BlockSpec(blk, lambda b: (b, 0)),
            pl.BlockSpec(blk, lambda b: (b, 1)),
            pl.BlockSpec(blk, lambda b: (b, 2)),
            pl.BlockSpec((1, HEAD_DIM), lambda b: (0, 0)),
        ],
        out_specs=pl.BlockSpec(blk, lambda b: (b, 0)),
        out_shape=jax.ShapeDtypeStruct((n, DIFF_WIDTH), BF16),
        compiler_params=_params(("arbitrary",)),
        name="diff_attention",
    )(lam, qkv, qkv, qkv, subln_g)


def _out_proj_kernel(x_ref, ssm_ref, o1_ref, o2_ref, o3_ref, l1_ref, l2_ref, l3_ref, diff_ref,
                     g_ref, w_ref, y_ref):
    l1, l2, l3 = l1_ref[...], l2_ref[...], l3_ref[...]
    m = jnp.maximum(jnp.maximum(l1, l2), l3)
    e1, e2, e3 = jnp.exp(l1 - m), jnp.exp(l2 - m), jnp.exp(l3 - m)
    inv = 1.0 / (e1 + e2 + e3)
    head_of_col = lax.broadcasted_iota(jnp.int32, (LANES, DIL_WIDTH), 1) // HEAD_DIM
    spread = (lax.broadcasted_iota(jnp.int32, (LANES, DIL_WIDTH), 0) == head_of_col).astype(BF16)
    mix = jnp.zeros((x_ref.shape[0], DIL_WIDTH), F32)
    for e, o_ref in ((e1, o1_ref), (e2, o2_ref), (e3, o3_ref)):
        w = jnp.dot((e * inv).astype(BF16), spread, preferred_element_type=F32)
        mix = mix + w * o_ref[...].astype(F32)
    y_dil = _rms(mix, g_ref[...]).astype(BF16)
    acc = jnp.dot(ssm_ref[...], w_ref[:SSM_WIDTH, :], preferred_element_type=F32)
    acc = acc + jnp.dot(y_dil, w_ref[SSM_WIDTH:SSM_WIDTH + DIL_WIDTH, :],
                        preferred_element_type=F32)
    acc = acc + jnp.dot(diff_ref[...], w_ref[SSM_WIDTH + DIL_WIDTH:, :],
                        preferred_element_type=F32)
    y_ref[...] = x_ref[...] + acc


def _out_proj(x2, y_ssm_tm, dil_outs, dil_lses, y_diff, dil_gain, w_out, batch, seq):
    n = batch * seq
    blocks_per_seq = seq // PROJ_TOKENS
    tok = lambda w: pl.BlockSpec((PROJ_TOKENS, w), lambda i: (i, 0))
    return pl.pallas_call(
        _out_proj_kernel,
        grid=(n // PROJ_TOKENS,),
        in_specs=[
            tok(D_MODEL),
            pl.BlockSpec((PROJ_TOKENS, SSM_WIDTH),
                         lambda i: (i % blocks_per_seq, i // blocks_per_seq)),
            tok(DIL_WIDTH), tok(DIL_WIDTH), tok(DIL_WIDTH),
            tok(LANES), tok(LANES), tok(LANES),
            tok(DIFF_WIDTH),
            pl.BlockSpec((1, DIL_WIDTH), lambda i: (0, 0)),
            pl.BlockSpec((D_MODEL, D_MODEL), lambda i: (0, 0)),
        ],
        out_specs=tok(D_MODEL),
        out_shape=jax.ShapeDtypeStruct((n, D_MODEL), F32),
        compiler_params=_params(("arbitrary",)),
        name="out_proj",
    )(x2, y_ssm_tm, *dil_outs, *dil_lses, y_diff, dil_gain, w_out)


def _sorting_network(n):
    pairs = []
    p = 1
    while p < n:
        k = p
        while k >= 1:
            for j in range(k % p, n - k, 2 * k):
                for i in range(min(k, n - j - k)):
                    if (i + j) // (2 * p) == (i + j + k) // (2 * p):
                        pairs.append((i + j, i + j + k))
            k //= 2
        p *= 2
    return pairs


_SORT16 = _sorting_network(PEER_TOPK)
_BITONIC16 = [(i, i + s) for s in (8, 4, 2, 1) for i in range(PEER_TOPK) if (i // s) % 2 == 0]


def _top16_desc(vals):
    v = list(vals)
    for a, b in _SORT16:
        hi, lo = jnp.maximum(v[a], v[b]), jnp.minimum(v[a], v[b])
        v[a], v[b] = hi, lo
    for shift in (4, 2, 1):
        other = [pltpu.roll(x, shift, 0) for x in v]
        v = [jnp.maximum(v[k], other[PEER_TOPK - 1 - k]) for k in range(PEER_TOPK)]
        for a, b in _BITONIC16:
            hi, lo = jnp.maximum(v[a], v[b]), jnp.minimum(v[a], v[b])
            v[a], v[b] = hi, lo
    return v


def _peer_gates(sa, sb):
    t = sa.shape[1]
    a_top = _top16_desc([sa[SUBLANES * v:SUBLANES * (v + 1), :] for v in range(PEER_KEYS // SUBLANES)])
    b_top = _top16_desc([sb[SUBLANES * v:SUBLANES * (v + 1), :] for v in range(PEER_KEYS // SUBLANES)])
    sub = lax.broadcasted_iota(jnp.int32, (SUBLANES, t), 0)

    def pack(rows):
        out = rows[0]
        for s in range(1, SUBLANES):
            out = jnp.where(sub == s, rows[s], out)
        return out

    b_lo, b_hi, a_hi = pack(b_top[:8]), pack(b_top[8:]), pack(a_top[8:])
    cands = [a_top[0] + b_lo, a_top[0] + b_hi]
    cands += [a_top[k] + b_lo for k in range(1, 8)]
    cands += [a_hi + b_top[0]]
    valid = [None, None] + [sub < (PEER_TOPK // (k + 1)) for k in range(1, 8)] + [None]
    cands = [c if ok is None else jnp.where(ok, c, NEG_INF) for c, ok in zip(cands, valid)]
    pad = jnp.full((SUBLANES, t), NEG_INF, F32)
    tau = _top16_desc(cands + [pad] * (PEER_TOPK - len(cands)))[PEER_TOPK - 1]
    top = a_top[0] + b_top[0]
    z = jnp.zeros((SUBLANES, t), F32)
    for c in cands:
        z = z + jnp.where(c >= tau, jnp.exp(c - top), 0.0)
    z = jnp.sum(z, axis=0, keepdims=True)

    a_thr, b_thr = a_top[PEER_TOPK - 1][:1], b_top[PEER_TOPK - 1][:1]
    wa = jnp.where(sa >= a_thr, jnp.exp(sa - a_top[0][:1]), 0.0) / z
    wb = jnp.where(sb >= b_thr, jnp.exp(sb - b_top[0][:1]), 0.0)
    count = jnp.zeros_like(sa)
    rank = jnp.zeros_like(sb)
    for k in range(PEER_TOPK):
        bk = b_top[k][:1]
        count = count + jnp.where(sa + bk >= tau[:1], 1.0, 0.0)
        rank = rank + jnp.where(bk >= sb, 1.0, 0.0)
    return wa, count, wb, rank


def _peer_kernel(final, x_ref, g_ref, wq_ref, keys_ref, u_ref, vt_ref, fg_ref, y_ref,
                 h_ref, wa_ref, cnt_ref, wb_ref, rank_ref, p_ref, acc_ref):
    e = pl.program_id(1)
    tb = x_ref.shape[0]
    n_chunks = tb // PEER_GATE_CHUNK

    @pl.when(e == 0)
    def _():
        h_ref[...] = _rms(x_ref[...], g_ref[...]).astype(BF16)
        acc_ref[...] = jnp.zeros_like(acc_ref)

        def chunk(ci, _):
            r0 = pl.multiple_of(ci * PEER_GATE_CHUNK, PEER_GATE_CHUNK)
            hc = h_ref[pl.ds(r0, PEER_GATE_CHUNK), :]

            def head(hd, _):
                q = jnp.dot(hc, wq_ref[hd], preferred_element_type=F32).astype(BF16)
                sa = _dot_nt(keys_ref[2 * hd], q[:, :PEER_KEYS])
                sb = _dot_nt(keys_ref[2 * hd + 1], q[:, PEER_KEYS:])
                wa, count, wb, rank = _peer_gates(sa, sb)
                wa_ref[ci, hd] = wa
                cnt_ref[ci, hd] = count
                wb_ref[ci, hd] = wb.astype(BF16)
                rank_ref[ci, hd] = rank.astype(BF16)
                return 0

            lax.fori_loop(0, PEER_HEADS, head, 0)
            return 0

        lax.fori_loop(0, n_chunks, chunk, 0)

    rows_per_step = PEER_EXPERT_BLOCK // PEER_KEYS
    act = _dot_nt(u_ref[...], h_ref[...])
    act = jax.nn.gelu(act).astype(BF16)
    for il in range(rows_per_step):
        i = e * rows_per_step + il
        for ci in range(n_chunks):
            gate = jnp.zeros((PEER_KEYS, PEER_GATE_CHUNK), BF16)
            for hd in range(PEER_HEADS):
                wa = wa_ref[ci, hd, pl.ds(i, 1), :].astype(BF16)
                cnt = cnt_ref[ci, hd, pl.ds(i, 1), :].astype(BF16)
                gate = gate + jnp.where(rank_ref[ci, hd] <= cnt, wb_ref[ci, hd] * wa,
                                        jnp.zeros((), BF16))
            a = act[il * PEER_KEYS:(il + 1) * PEER_KEYS,
                    ci * PEER_GATE_CHUNK:(ci + 1) * PEER_GATE_CHUNK]
            p_ref[il * PEER_KEYS:(il + 1) * PEER_KEYS,
                  ci * PEER_GATE_CHUNK:(ci + 1) * PEER_GATE_CHUNK] = gate * a
    acc_ref[...] += jnp.dot(vt_ref[...], p_ref[...], preferred_element_type=F32)

    @pl.when(e == pl.num_programs(1) - 1)
    def _():
        y = x_ref[...] + acc_ref[...].T
        if final:
            y = _rms(y, fg_ref[...])
        y_ref[...] = y


def _peer(x2, gain, wq_heads, keys, u_bf, vt_bf, final_gain, final):
    n = x2.shape[0]
    n_chunks = PEER_TOKENS // PEER_GATE_CHUNK
    gate_shape = (n_chunks, PEER_HEADS, PEER_KEYS, PEER_GATE_CHUNK)
    return pl.pallas_call(
        functools.partial(_peer_kernel, final),
        grid=(n // PEER_TOKENS, PEER_EXPERTS // PEER_EXPERT_BLOCK),
        in_specs=[
            pl.BlockSpec((PEER_TOKENS, D_MODEL), lambda t, e: (t, 0)),
            pl.BlockSpec((1, D_MODEL), lambda t, e: (0, 0)),
            pl.BlockSpec((PEER_HEADS, D_MODEL, 2 * PEER_KEYS), lambda t, e: (0, 0, 0)),
            pl.BlockSpec((2 * PEER_HEADS, PEER_KEYS, PEER_KEYS), lambda t, e: (0, 0, 0)),
            pl.BlockSpec((PEER_EXPERT_BLOCK, D_MODEL), lambda t, e: (e, 0)),
            pl.BlockSpec((D_MODEL, PEER_EXPERT_BLOCK), lambda t, e: (0, e)),
            pl.BlockSpec((1, D_MODEL), lambda t, e: (0, 0)),
        ],
        out_specs=pl.BlockSpec((PEER_TOKENS, D_MODEL), lambda t, e: (t, 0)),
        out_shape=jax.ShapeDtypeStruct((n, D_MODEL), F32),
        scratch_shapes=[
            pltpu.VMEM((PEER_TOKENS, D_MODEL), BF16),
            pltpu.VMEM(gate_shape, F32),
            pltpu.VMEM(gate_shape, F32),
            pltpu.VMEM(gate_shape, BF16),
            pltpu.VMEM(gate_shape, BF16),
            pltpu.VMEM((PEER_EXPERT_BLOCK, PEER_TOKENS), BF16),
            pltpu.VMEM((D_MODEL, PEER_TOKENS), F32),
        ],
        compiler_params=_params(("arbitrary", "arbitrary")),
        name="peer",
    )(x2, gain, wq_heads, keys, u_bf, vt_bf, final_gain)


def _ssm_matrices(lam_re, lam_im, log_step, b_re, b_im, c_re, c_im):
    lam = lax.complex(lam_re, lam_im)
    step = jnp.exp(log_step)[:, None]
    lam_bar = jnp.exp(lam * step)
    b_bar = ((lam_bar - 1.0) / lam)[:, :, None] * lax.complex(b_re, b_im)
    eye = jnp.eye(SSM_GROUPS, dtype=F32)

    def embed_in(m):
        return jnp.einsum("gpc,gh->gchp", m, eye).reshape(SSM_WIDTH, SSM_HALF)

    def embed_out(m):
        return jnp.einsum("gcp,gh->gphc", m, eye).reshape(SSM_HALF, SSM_WIDTH)

    bmat = jnp.concatenate([embed_in(jnp.real(b_bar)), embed_in(jnp.imag(b_bar))], axis=1)
    cmat = jnp.concatenate([embed_out(c_re), embed_out(-c_im)], axis=0)
    lre = jnp.broadcast_to(jnp.real(lam_bar).reshape(1, SSM_HALF), (SUBLANES, SSM_HALF))
    lim = jnp.broadcast_to(jnp.imag(lam_bar).reshape(1, SSM_HALF), (SUBLANES, SSM_HALF))
    return bmat.astype(BF16), cmat.astype(BF16), lre, lim


def _in_proj_col_scale():
    s = jnp.ones((IN_WIDTH,), F32)
    s = s.at[SSM_WIDTH:SSM_WIDTH + DIL_WIDTH].set(HEAD_DIM ** -0.5)
    d0 = SSM_WIDTH + 3 * DIL_WIDTH
    s = s.at[d0:d0 + DIFF_WIDTH].set(DIFF_QK_DIM ** -0.5)
    return s.reshape(1, IN_WIDTH)


def kernel(x, norm1_g, w_in, ssm_lam_re, ssm_lam_im, ssm_log_step, ssm_b_re, ssm_b_im, ssm_c_re, ssm_c_im, ssm_d, ssm_w_glu, ssm_norm_g, dil_norm_g, diff_lam_q1, diff_lam_k1, diff_lam_q2, diff_lam_k2, diff_subln_g, w_out, norm2_g, peer_w_query, peer_sub_keys, peer_u, peer_v, final_norm_g):
    batch, seq, _ = x.shape
    n = batch * seq
    depth = w_in.shape[0]
    x2 = x.reshape(n, D_MODEL)
    col_scale = _in_proj_col_scale()
    row = lambda v: v.reshape(1, -1)
    for layer in range(depth):
        lambda_init = 0.8 - 0.6 * math.exp(-0.3 * layer)
        u_tm, dil_qkv, diff_qkv = _in_proj(x2, row(norm1_g[layer]), w_in[layer].astype(BF16),
                                           col_scale, batch, seq)
        bmat, cmat, lre, lim = _ssm_matrices(
            ssm_lam_re[layer], ssm_lam_im[layer], ssm_log_step[layer], ssm_b_re[layer],
            ssm_b_im[layer], ssm_c_re[layer], ssm_c_im[layer])
        y_ssm = _ssm(u_tm.reshape(seq * batch, SSM_WIDTH), bmat, cmat, lre, lim,
                     row(ssm_d[layer]), ssm_w_glu[layer].astype(BF16), row(ssm_norm_g[layer]),
                     batch, seq)
        dil = [_band_attention(dil_qkv, d, batch, seq) for d in DIL_DILATIONS]
        lam = (jnp.exp(jnp.sum(diff_lam_q1[layer] * diff_lam_k1[layer]))
               - jnp.exp(jnp.sum(diff_lam_q2[layer] * diff_lam_k2[layer])) + lambda_init)
        y_diff = _diff_attention(diff_qkv, lam.reshape(1), row(diff_subln_g[layer]),
                                 1.0 - lambda_init, batch, seq)
        x2 = _out_proj(x2, y_ssm.reshape(seq, batch * SSM_WIDTH), [o for o, _ in dil],
                       [l for _, l in dil], y_diff, row(dil_norm_g[layer]),
                       w_out[layer].astype(BF16), batch, seq)
        wq_heads = peer_w_query[layer].reshape(D_MODEL, PEER_HEADS, 2 * PEER_KEYS)
        wq_heads = wq_heads.transpose(1, 0, 2).astype(BF16)
        keys = peer_sub_keys[layer].reshape(2 * PEER_HEADS, PEER_KEYS, PEER_KEYS).astype(BF16)
        x2 = _peer(x2, row(norm2_g[layer]), wq_heads, keys, peer_u[layer].astype(BF16),
                   peer_v[layer].T.astype(BF16), row(final_norm_g), layer == depth - 1)
    return x2.reshape(batch, seq, D_MODEL)
```

```python
import functools
import math

import jax
import jax.numpy as jnp
from jax import lax
from jax.experimental import pallas as pl
from jax.experimental.pallas import tpu as pltpu

F32 = jnp.float32
BF16 = jnp.bfloat16

D_MODEL = 1024
HEAD_DIM = 64
SSM_WIDTH = 384
SSM_GROUP_CH = 16
SSM_GROUPS = 24
SSM_STATE = 64
SSM_HALF = SSM_GROUPS * SSM_STATE
DIL_WIDTH = 384
DIL_HEADS = 6
DIL_DILATIONS = (1, 4, 16)
DIFF_WIDTH = 256
DIFF_HEADS = 4
DIFF_QK_DIM = 32
IN_WIDTH = 2304
PEER_HEADS = 8
PEER_KEYS = 128
PEER_TOPK = 16
PEER_EXPERTS = PEER_KEYS * PEER_KEYS
NORM_EPS = 1e-6

LANES = 128
SUBLANES = 8
VMEM_LIMIT_BYTES = 56 * 1024 * 1024

PROJ_TOKENS = 512
SSM_CHUNK = 128
ATT_BLOCK = 128
DIFF_BLOCK = 256
PEER_TOKENS = 1024
PEER_GATE_CHUNK = 256
PEER_EXPERT_BLOCK = 256

NEG_INF = float("-inf")


def _rms(x, gain):
    return x * lax.rsqrt(jnp.mean(x * x, axis=-1, keepdims=True) + NORM_EPS) * gain


def _dot_nt(a, b):
    return lax.dot_general(a, b, (((1,), (1,)), ((), ())), preferred_element_type=F32)


def _params(semantics):
    return pltpu.CompilerParams(dimension_semantics=semantics, vmem_limit_bytes=VMEM_LIMIT_BYTES)


def _in_proj_kernel(x_ref, g_ref, w_ref, scale_ref, u_ref, dil_ref, diff_ref):
    h = _rms(x_ref[...], g_ref[...])
    p = jnp.dot(h.astype(BF16), w_ref[...], preferred_element_type=F32) * scale_ref[...]
    u_ref[...] = p[:, :SSM_WIDTH].astype(BF16)
    dil_ref[...] = p[:, SSM_WIDTH:SSM_WIDTH + 3 * DIL_WIDTH].astype(BF16)
    diff_ref[...] = p[:, SSM_WIDTH + 3 * DIL_WIDTH:].astype(BF16)


def _in_proj(x2, gain, w_in, col_scale, batch, seq):
    n = batch * seq
    blocks_per_seq = seq // PROJ_TOKENS
    return pl.pallas_call(
        _in_proj_kernel,
        grid=(n // PROJ_TOKENS,),
        in_specs=[
            pl.BlockSpec((PROJ_TOKENS, D_MODEL), lambda i: (i, 0)),
            pl.BlockSpec((1, D_MODEL), lambda i: (0, 0)),
            pl.BlockSpec((D_MODEL, IN_WIDTH), lambda i: (0, 0)),
            pl.BlockSpec((1, IN_WIDTH), lambda i: (0, 0)),
        ],
        out_specs=[
            pl.BlockSpec((PROJ_TOKENS, SSM_WIDTH),
                         lambda i: (i % blocks_per_seq, i // blocks_per_seq)),
            pl.BlockSpec((PROJ_TOKENS, 3 * DIL_WIDTH), lambda i: (i, 0)),
            pl.BlockSpec((PROJ_TOKENS, 3 * DIFF_WIDTH), lambda i: (i, 0)),
        ],
        out_shape=[
            jax.ShapeDtypeStruct((seq, batch * SSM_WIDTH), BF16),
            jax.ShapeDtypeStruct((n, 3 * DIL_WIDTH), BF16),
            jax.ShapeDtypeStruct((n, 3 * DIFF_WIDTH), BF16),
        ],
        compiler_params=_params(("arbitrary",)),
        name="in_proj",
    )(x2, gain, w_in, col_scale)


def _ssm_kernel(u_ref, bmat_ref, cmat_ref, lre_ref, lim_ref, d_ref, wglu_ref, g_ref,
                y_ref, s_ref, state_ref):
    rows = u_ref.shape[0]
    sub = 256

    @pl.when(pl.program_id(0) == 0)
    def _():
        state_ref[...] = jnp.zeros_like(state_ref)

    for c in range(rows // sub):
        s_ref[c * sub:(c + 1) * sub, :] = jnp.dot(
            u_ref[c * sub:(c + 1) * sub, :], bmat_ref[...], preferred_element_type=F32)

    lre = lre_ref[...]
    lim = lim_ref[...]

    def step(t, carry):
        sre, sim = carry
        r0 = pl.multiple_of(t * SUBLANES, SUBLANES)
        bre = s_ref[pl.ds(r0, SUBLANES), :SSM_HALF]
        bim = s_ref[pl.ds(r0, SUBLANES), SSM_HALF:]
        nre = lre * sre - lim * sim + bre
        nim = lre * sim + lim * sre + bim
        s_ref[pl.ds(r0, SUBLANES), :SSM_HALF] = nre
        s_ref[pl.ds(r0, SUBLANES), SSM_HALF:] = nim
        return nre, nim

    sre, sim = lax.fori_loop(0, rows // SUBLANES, step,
                             (state_ref[:, :SSM_HALF], state_ref[:, SSM_HALF:]), unroll=2)
    state_ref[:, :SSM_HALF] = sre
    state_ref[:, SSM_HALF:] = sim

    for c in range(rows // sub):
        sl = slice(c * sub, (c + 1) * sub)
        y = jnp.dot(s_ref[sl, :].astype(BF16), cmat_ref[...], preferred_element_type=F32)
        y = y + d_ref[...] * u_ref[sl, :].astype(F32)
        g = jax.nn.gelu(y)
        z = jnp.dot(g.astype(BF16), wglu_ref[...], preferred_element_type=F32)
        y = g * jax.nn.sigmoid(z)
        y_ref[sl, :] = _rms(y, g_ref[...]).astype(BF16)


def _ssm(u_tm, bmat, cmat, lre, lim, d_skip, w_glu, gain, batch, seq):
    assert batch == SUBLANES, "the S5 scan keeps one batch per sublane"
    rows = SSM_CHUNK * batch
    const = lambda i: (0, 0)
    return pl.pallas_call(
        _ssm_kernel,
        grid=(seq // SSM_CHUNK,),
        in_specs=[
            pl.BlockSpec((rows, SSM_WIDTH), lambda i: (i, 0)),
            pl.BlockSpec((SSM_WIDTH, 2 * SSM_HALF), const),
            pl.BlockSpec((2 * SSM_HALF, SSM_WIDTH), const),
            pl.BlockSpec((SUBLANES, SSM_HALF), const),
            pl.BlockSpec((SUBLANES, SSM_HALF), const),
            pl.BlockSpec((1, SSM_WIDTH), const),
            pl.BlockSpec((SSM_WIDTH, SSM_WIDTH), const),
            pl.BlockSpec((1, SSM_WIDTH), const),
        ],
        out_specs=pl.BlockSpec((rows, SSM_WIDTH), lambda i: (i, 0)),
        out_shape=jax.ShapeDtypeStruct((seq * batch, SSM_WIDTH), BF16),
        scratch_shapes=[
            pltpu.VMEM((rows, 2 * SSM_HALF), F32),
            pltpu.VMEM((SUBLANES, 2 * SSM_HALF), F32),
        ],
        compiler_params=_params(("arbitrary",)),
        name="ssm",
    )(u_tm, bmat, cmat, lre, lim, d_skip, w_glu, gain)


def _band_kernel(has_prev, *refs):
    if has_prev:
        q_ref, kc_ref, vc_ref, kp_ref, vp_ref, o_ref, lse_ref = refs
    else:
        q_ref, kc_ref, vc_ref, o_ref, lse_ref = refs
    blk = pl.program_id(2)
    row = lax.broadcasted_iota(jnp.int32, (ATT_BLOCK, ATT_BLOCK), 0)
    col = lax.broadcasted_iota(jnp.int32, (ATT_BLOCK, ATT_BLOCK), 1)
    cur_ok = col <= row
    if has_prev:
        prev_ok = jnp.logical_and(col >= row, blk > 0)
    lane = lax.broadcasted_iota(jnp.int32, (ATT_BLOCK, LANES), 1)
    lse_tile = jnp.zeros((ATT_BLOCK, LANES), F32)
    heads = [slice(h * HEAD_DIM, (h + 1) * HEAD_DIM) for h in range(DIL_HEADS)]
    sc_all = [_dot_nt(q_ref[0, :, hs], kc_ref[0, :, hs]) for hs in heads]
    if has_prev:
        sp_all = [_dot_nt(q_ref[0, :, hs], kp_ref[0, :, hs]) for hs in heads]
    pc_all, pp_all, den_all = [], [], []
    for h in range(DIL_HEADS):
        sc = jnp.where(cur_ok, sc_all[h], NEG_INF)
        m = jnp.max(sc, axis=-1, keepdims=True)
        if has_prev:
            sp = jnp.where(prev_ok, sp_all[h], NEG_INF)
            m = jnp.maximum(m, jnp.max(sp, axis=-1, keepdims=True))
        pc = jnp.exp(sc - m)
        den = jnp.sum(pc, axis=-1, keepdims=True)
        if has_prev:
            pp = jnp.exp(sp - m)
            den = den + jnp.sum(pp, axis=-1, keepdims=True)
            pp_all.append(pp.astype(BF16))
        pc_all.append(pc.astype(BF16))
        den_all.append(den)
        lse_tile = jnp.where(lane == h, m + jnp.log(den), lse_tile)
    for h, hs in enumerate(heads):
        acc = jnp.dot(pc_all[h], vc_ref[0, :, hs], preferred_element_type=F32)
        if has_prev:
            acc = acc + jnp.dot(pp_all[h], vp_ref[0, :, hs], preferred_element_type=F32)
        o_ref[0, :, hs] = (acc / den_all[h]).astype(BF16)
    lse_ref[0] = lse_tile


def _band_attention(qkv, dilation, batch, seq):
    rows = seq // dilation
    nblk = rows // ATT_BLOCK
    has_prev = nblk > 1
    view = qkv.reshape(batch, rows, dilation * 3 * DIL_WIDTH)
    blk = (1, ATT_BLOCK, DIL_WIDTH)
    in_specs = [
        pl.BlockSpec(blk, lambda b, r, i: (b, i, 3 * r)),
        pl.BlockSpec(blk, lambda b, r, i: (b, i, 3 * r + 1)),
        pl.BlockSpec(blk, lambda b, r, i: (b, i, 3 * r + 2)),
    ]
    args = [view, view, view]
    if has_prev:
        in_specs += [
            pl.BlockSpec(blk, lambda b, r, i: (b, jnp.maximum(i - 1, 0), 3 * r + 1)),
            pl.BlockSpec(blk, lambda b, r, i: (b, jnp.maximum(i - 1, 0), 3 * r + 2)),
        ]
        args += [view, view]
    out, lse = pl.pallas_call(
        functools.partial(_band_kernel, has_prev),
        grid=(batch, dilation, nblk),
        in_specs=in_specs,
        out_specs=[
            pl.BlockSpec(blk, lambda b, r, i: (b, i, r)),
            pl.BlockSpec((1, ATT_BLOCK, LANES), lambda b, r, i: (b, i, r)),
        ],
        out_shape=[
            jax.ShapeDtypeStruct((batch, rows, dilation * DIL_WIDTH), BF16),
            jax.ShapeDtypeStruct((batch, rows, dilation * LANES), F32),
        ],
        compiler_params=_params(("arbitrary", "arbitrary", "arbitrary")),
        name=f"dilated_d{dilation}",
    )(*args)
    n = batch * seq
    return out.reshape(n, DIL_WIDTH), lse.reshape(n, LANES)


def _diff_kernel(out_scale, lam_ref, q_ref, k_ref, v_ref, g_ref, o_ref, qt_ref, vt_ref, ot_ref,
                 qm_ref, m_ref, l_ref, acc_ref):
    seq = q_ref.shape[0]
    nblk = seq // DIFF_BLOCK
    lam = lam_ref[0]
    for i in range(nblk):
        rows = slice(i * DIFF_BLOCK, (i + 1) * DIFF_BLOCK)
        qt_ref[i] = q_ref[rows, :].astype(F32).T.astype(BF16)
        vt_ref[i] = v_ref[rows, :].astype(F32).T.astype(BF16)
    krow = lax.broadcasted_iota(jnp.int32, (DIFF_BLOCK, DIFF_BLOCK), 0)
    qcol = lax.broadcasted_iota(jnp.int32, (DIFF_BLOCK, DIFF_BLOCK), 1)
    causal = krow <= qcol
    chan = lax.broadcasted_iota(jnp.int32, (DIFF_WIDTH, DIFF_BLOCK), 0)

    n_maps = 2 * DIFF_HEADS

    def attend(kj, mask):
        k0 = pl.multiple_of(kj * DIFF_BLOCK, DIFF_BLOCK)
        k = k_ref[pl.ds(k0, DIFF_BLOCK), :]
        scores = [jnp.dot(k, qm_ref[c], preferred_element_type=F32) for c in range(n_maps)]
        probs, alphas = [], []
        for c in range(n_maps):
            s = scores[c]
            if mask:
                s = jnp.where(causal, s, NEG_INF)
            m_old = m_ref[c]
            m_new = jnp.maximum(m_old, jnp.max(s, axis=0, keepdims=True))
            alpha = jnp.exp(m_old - m_new)
            p = jnp.exp(s - m_new)
            m_ref[c] = m_new
            l_ref[c] = alpha * l_ref[c] + jnp.sum(p, axis=0, keepdims=True)
            probs.append(p.astype(BF16))
            alphas.append(alpha)
        for c in range(n_maps):
            h = c // 2
            vt = vt_ref[kj, h * HEAD_DIM:(h + 1) * HEAD_DIM, :]
            acc_ref[c] = alphas[c] * acc_ref[c] + jnp.dot(vt, probs[c],
                                                          preferred_element_type=F32)

    def q_block(qi, _):
        qt = qt_ref[qi]
        for c in range(n_maps):
            lo = c * DIFF_QK_DIM
            qm_ref[c] = jnp.where(jnp.logical_and(chan >= lo, chan < lo + DIFF_QK_DIM), qt,
                                  jnp.zeros((), BF16))
        m_ref[...] = jnp.full(m_ref.shape, NEG_INF, F32)
        l_ref[...] = jnp.zeros_like(l_ref)
        acc_ref[...] = jnp.zeros_like(acc_ref)

        def k_block(kj, _):
            attend(kj, False)
            return 0

        lax.fori_loop(0, qi, k_block, 0)
        attend(qi, True)
        for h in range(DIFF_HEADS):
            o = (acc_ref[2 * h] / l_ref[2 * h]
                 - lam * (acc_ref[2 * h + 1] / l_ref[2 * h + 1]))
            o = o * lax.rsqrt(jnp.mean(o * o, axis=0, keepdims=True) + NORM_EPS)
            ot_ref[qi, h * HEAD_DIM:(h + 1) * HEAD_DIM, :] = o * (g_ref[...] * out_scale)
        return 0

    lax.fori_loop(0, nblk, q_block, 0)

    for i in range(nblk):
        o_ref[i * DIFF_BLOCK:(i + 1) * DIFF_BLOCK, :] = ot_ref[i].T.astype(BF16)


def _diff_attention(qkv, lam, subln_g_col, out_scale, batch, seq):
    n = batch * seq
    blk = (seq, DIFF_WIDTH)
    nblk = seq // DIFF_BLOCK
    return pl.pallas_call(
        functools.partial(_diff_kernel, out_scale),
        grid=(batch,),
        in_specs=[
            pl.BlockSpec(memory_space=pltpu.SMEM),
            pl.BlockSpec(blk, lambda b: (b, 0)),
            pl.BlockSpec(blk, lambda b: (b, 1)),
            pl.BlockSpec(blk, lambda b: (b, 2)),
            pl.BlockSpec((HEAD_DIM, 1), lambda b: (0, 0)),
        ],
        out_specs=pl.BlockSpec(blk, lambda b: (b, 0)),
        out_shape=jax.ShapeDtypeStruct((n, DIFF_WIDTH), BF16),
        scratch_shapes=[
            pltpu.VMEM((nblk, DIFF_WIDTH, DIFF_BLOCK), BF16),
            pltpu.VMEM((nblk, DIFF_WIDTH, DIFF_BLOCK), BF16),
            pltpu.VMEM((nblk, DIFF_WIDTH, DIFF_BLOCK), F32),
            pltpu.VMEM((2 * DIFF_HEADS, DIFF_WIDTH, DIFF_BLOCK), BF16),
            pltpu.VMEM((2 * DIFF_HEADS, 1, DIFF_BLOCK), F32),
            pltpu.VMEM((2 * DIFF_HEADS, 1, DIFF_BLOCK), F32),
            pltpu.VMEM((2 * DIFF_HEADS, HEAD_DIM, DIFF_BLOCK), F32),
        ],
        compiler_params=_params(("arbitrary",)),
        name="diff_attention",
    )(lam, qkv, qkv, qkv, subln_g_col)


def _out_proj_kernel(x_ref, ssm_ref, o1_ref, o2_ref, o3_ref, l1_ref, l2_ref, l3_ref, diff_ref,
                     g_ref, w_ref, y_ref):
    l1, l2, l3 = l1_ref[...], l2_ref[...], l3_ref[...]
    m = jnp.maximum(jnp.maximum(l1, l2), l3)
    e1, e2, e3 = jnp.exp(l1 - m), jnp.exp(l2 - m), jnp.exp(l3 - m)
    inv = 1.0 / (e1 + e2 + e3)
    head_of_col = lax.broadcasted_iota(jnp.int32, (LANES, DIL_WIDTH), 1) // HEAD_DIM
    spread = (lax.broadcasted_iota(jnp.int32, (LANES, DIL_WIDTH), 0) == head_of_col).astype(BF16)
    mix = jnp.zeros((x_ref.shape[0], DIL_WIDTH), F32)
    for e, o_ref in ((e1, o1_ref), (e2, o2_ref), (e3, o3_ref)):
        w = jnp.dot((e * inv).astype(BF16), spread, preferred_element_type=F32)
        mix = mix + w * o_ref[...].astype(F32)
    y_dil = _rms(mix, g_ref[...]).astype(BF16)
    acc = jnp.dot(ssm_ref[...], w_ref[:SSM_WIDTH, :], preferred_element_type=F32)
    acc = acc + jnp.dot(y_dil, w_ref[SSM_WIDTH:SSM_WIDTH + DIL_WIDTH, :],
                        preferred_element_type=F32)
    acc = acc + jnp.dot(diff_ref[...], w_ref[SSM_WIDTH + DIL_WIDTH:, :],
                        preferred_element_type=F32)
    y_ref[...] = x_ref[...] + acc


def _out_proj(x2, y_ssm_tm, dil_outs, dil_lses, y_diff, dil_gain, w_out, batch, seq):
    n = batch * seq
    blocks_per_seq = seq // PROJ_TOKENS
    tok = lambda w: pl.BlockSpec((PROJ_TOKENS, w), lambda i: (i, 0))
    return pl.pallas_call(
        _out_proj_kernel,
        grid=(n // PROJ_TOKENS,),
        in_specs=[
            tok(D_MODEL),
            pl.BlockSpec((PROJ_TOKENS, SSM_WIDTH),
                         lambda i: (i % blocks_per_seq, i // blocks_per_seq)),
            tok(DIL_WIDTH), tok(DIL_WIDTH), tok(DIL_WIDTH),
            tok(LANES), tok(LANES), tok(LANES),
            tok(DIFF_WIDTH),
            pl.BlockSpec((1, DIL_WIDTH), lambda i: (0, 0)),
            pl.BlockSpec((D_MODEL, D_MODEL), lambda i: (0, 0)),
        ],
        out_specs=tok(D_MODEL),
        out_shape=jax.ShapeDtypeStruct((n, D_MODEL), F32),
        compiler_params=_params(("arbitrary",)),
        name="out_proj",
    )(x2, y_ssm_tm, *dil_outs, *dil_lses, y_diff, dil_gain, w_out)


def _sorting_network(n):
    pairs = []
    p = 1
    while p < n:
        k = p
        while k >= 1:
            for j in range(k % p, n - k, 2 * k):
                for i in range(min(k, n - j - k)):
                    if (i + j) // (2 * p) == (i + j + k) // (2 * p):
                        pairs.append((i + j, i + j + k))
            k //= 2
        p *= 2
    return pairs


_SORT16 = _sorting_network(PEER_TOPK)
_BITONIC16 = [(i, i + s) for s in (8, 4, 2, 1) for i in range(PEER_TOPK) if (i // s) % 2 == 0]


def _top16_desc(vals):
    v = list(vals)
    for a, b in _SORT16:
        hi, lo = jnp.maximum(v[a], v[b]), jnp.minimum(v[a], v[b])
        v[a], v[b] = hi, lo
    for shift in (4, 2, 1):
        other = [pltpu.roll(x, shift, 0) for x in v]
        v = [jnp.maximum(v[k], other[PEER_TOPK - 1 - k]) for k in range(PEER_TOPK)]
        for a, b in _BITONIC16:
            hi, lo = jnp.maximum(v[a], v[b]), jnp.minimum(v[a], v[b])
            v[a], v[b] = hi, lo
    return v


def _peer_gates(sa, sb):
    t = sa.shape[1]
    a_top = _top16_desc([sa[SUBLANES * v:SUBLANES * (v + 1), :] for v in range(PEER_KEYS // SUBLANES)])
    b_top = _top16_desc([sb[SUBLANES * v:SUBLANES * (v + 1), :] for v in range(PEER_KEYS // SUBLANES)])
    sub = lax.broadcasted_iota(jnp.int32, (SUBLANES, t), 0)

    def pack(rows):
        out = rows[0]
        for s in range(1, SUBLANES):
            out = jnp.where(sub == s, rows[s], out)
        return out

    b_lo, b_hi, a_hi = pack(b_top[:8]), pack(b_top[8:]), pack(a_top[8:])
    cands = [a_top[0] + b_lo, a_top[0] + b_hi]
    cands += [a_top[k] + b_lo for k in range(1, 8)]
    cands += [a_hi + b_top[0]]
    valid = [None, None] + [sub < (PEER_TOPK // (k + 1)) for k in range(1, 8)] + [None]
    cands = [c if ok is None else jnp.where(ok, c, NEG_INF) for c, ok in zip(cands, valid)]
    pad = jnp.full((SUBLANES, t), NEG_INF, F32)
    tau = _top16_desc(cands + [pad] * (PEER_TOPK - len(cands)))[PEER_TOPK - 1]
    top = a_top[0] + b_top[0]
    z = jnp.zeros((SUBLANES, t), F32)
    for c in cands:
        z = z + jnp.where(c >= tau, jnp.exp(c - top), 0.0)
    z = jnp.sum(z, axis=0, keepdims=True)

    a_thr, b_thr = a_top[PEER_TOPK - 1][:1], b_top[PEER_TOPK - 1][:1]
    wa = jnp.where(sa >= a_thr, jnp.exp(sa - a_top[0][:1]), 0.0) / z
    wb = jnp.where(sb >= b_thr, jnp.exp(sb - b_top[0][:1]), 0.0)
    count = jnp.zeros_like(sa)
    rank = jnp.zeros_like(sb)
    for k in range(PEER_TOPK):
        bk = b_top[k][:1]
        count = count + jnp.where(sa + bk >= tau[:1], 1.0, 0.0)
        rank = rank + jnp.where(bk >= sb, 1.0, 0.0)
    return wa, count, wb, rank


def _peer_kernel(final, x_ref, g_ref, wq_ref, keys_ref, u_ref, vt_ref, fg_ref, y_ref,
                 h_ref, wa_ref, cnt_ref, wb_ref, rank_ref, p_ref, acc_ref):
    e = pl.program_id(1)
    tb = x_ref.shape[0]
    n_chunks = tb // PEER_GATE_CHUNK

    @pl.when(e == 0)
    def _():
        h_ref[...] = _rms(x_ref[...], g_ref[...]).astype(BF16)
        acc_ref[...] = jnp.zeros_like(acc_ref)

        def chunk(ci, _):
            r0 = pl.multiple_of(ci * PEER_GATE_CHUNK, PEER_GATE_CHUNK)
            hc = h_ref[pl.ds(r0, PEER_GATE_CHUNK), :]

            def head(hd, _):
                q = jnp.dot(hc, wq_ref[hd], preferred_element_type=F32).astype(BF16)
                sa = _dot_nt(keys_ref[2 * hd], q[:, :PEER_KEYS])
                sb = _dot_nt(keys_ref[2 * hd + 1], q[:, PEER_KEYS:])
                wa, count, wb, rank = _peer_gates(sa, sb)
                wa_ref[ci, hd] = wa
                cnt_ref[ci, hd] = count
                wb_ref[ci, hd] = wb.astype(BF16)
                rank_ref[ci, hd] = rank.astype(BF16)
                return 0

            lax.fori_loop(0, PEER_HEADS, head, 0)
            return 0

        lax.fori_loop(0, n_chunks, chunk, 0)

    rows_per_step = PEER_EXPERT_BLOCK // PEER_KEYS
    act = _dot_nt(u_ref[...], h_ref[...])
    act = jax.nn.gelu(act).astype(BF16)
    for il in range(rows_per_step):
        i = e * rows_per_step + il
        for ci in range(n_chunks):
            gate = jnp.zeros((PEER_KEYS, PEER_GATE_CHUNK), BF16)
            for hd in range(PEER_HEADS):
                wa = wa_ref[ci, hd, pl.ds(i, 1), :].astype(BF16)
                cnt = cnt_ref[ci, hd, pl.ds(i, 1), :].astype(BF16)
                gate = gate + jnp.where(rank_ref[ci, hd] <= cnt, wb_ref[ci, hd] * wa,
                                        jnp.zeros((), BF16))
            a = act[il * PEER_KEYS:(il + 1) * PEER_KEYS,
                    ci * PEER_GATE_CHUNK:(ci + 1) * PEER_GATE_CHUNK]
            p_ref[il * PEER_KEYS:(il + 1) * PEER_KEYS,
                  ci * PEER_GATE_CHUNK:(ci + 1) * PEER_GATE_CHUNK] = gate * a
    acc_ref[...] += jnp.dot(vt_ref[...], p_ref[...], preferred_element_type=F32)

    @pl.when(e == pl.num_programs(1) - 1)
    def _():
        y = x_ref[...] + acc_ref[...].T
        if final:
            y = _rms(y, fg_ref[...])
        y_ref[...] = y


def _peer(x2, gain, wq_heads, keys, u_bf, vt_bf, final_gain, final):
    n = x2.shape[0]
    n_chunks = PEER_TOKENS // PEER_GATE_CHUNK
    gate_shape = (n_chunks, PEER_HEADS, PEER_KEYS, PEER_GATE_CHUNK)
    return pl.pallas_call(
        functools.partial(_peer_kernel, final),
        grid=(n // PEER_TOKENS, PEER_EXPERTS // PEER_EXPERT_BLOCK),
        in_specs=[
            pl.BlockSpec((PEER_TOKENS, D_MODEL), lambda t, e: (t, 0)),
            pl.BlockSpec((1, D_MODEL), lambda t, e: (0, 0)),
            pl.BlockSpec((PEER_HEADS, D_MODEL, 2 * PEER_KEYS), lambda t, e: (0, 0, 0)),
            pl.BlockSpec((2 * PEER_HEADS, PEER_KEYS, PEER_KEYS), lambda t, e: (0, 0, 0)),
            pl.BlockSpec((PEER_EXPERT_BLOCK, D_MODEL), lambda t, e: (e, 0)),
            pl.BlockSpec((D_MODEL, PEER_EXPERT_BLOCK), lambda t, e: (0, e)),
            pl.BlockSpec((1, D_MODEL), lambda t, e: (0, 0)),
        ],
        out_specs=pl.BlockSpec((PEER_TOKENS, D_MODEL), lambda t, e: (t, 0)),
        out_shape=jax.ShapeDtypeStruct((n, D_MODEL), F32),
        scratch_shapes=[
            pltpu.VMEM((PEER_TOKENS, D_MODEL), BF16),
            pltpu.VMEM(gate_shape, F32),
            pltpu.VMEM(gate_shape, F32),
            pltpu.VMEM(gate_shape, BF16),
            pltpu.VMEM(gate_shape, BF16),
            pltpu.VMEM((PEER_EXPERT_BLOCK, PEER_TOKENS), BF16),
            pltpu.VMEM((D_MODEL, PEER_TOKENS), F32),
        ],
        compiler_params=_params(("arbitrary", "arbitrary")),
        name="peer",
    )(x2, gain, wq_heads, keys, u_bf, vt_bf, final_gain)


def _ssm_matrices(lam_re, lam_im, log_step, b_re, b_im, c_re, c_im):
    lam = lax.complex(lam_re, lam_im)
    step = jnp.exp(log_step)[:, None]
    lam_bar = jnp.exp(lam * step)
    b_bar = ((lam_bar - 1.0) / lam)[:, :, None] * lax.complex(b_re, b_im)
    eye = jnp.eye(SSM_GROUPS, dtype=F32)

    def embed_in(m):
        return jnp.einsum("gpc,gh->gchp", m, eye).reshape(SSM_WIDTH, SSM_HALF)

    def embed_out(m):
        return jnp.einsum("gcp,gh->gphc", m, eye).reshape(SSM_HALF, SSM_WIDTH)

    bmat = jnp.concatenate([embed_in(jnp.real(b_bar)), embed_in(jnp.imag(b_bar))], axis=1)
    cmat = jnp.concatenate([embed_out(c_re), embed_out(-c_im)], axis=0)
    lre = jnp.broadcast_to(jnp.real(lam_bar).reshape(1, SSM_HALF), (SUBLANES, SSM_HALF))
    lim = jnp.broadcast_to(jnp.imag(lam_bar).reshape(1, SSM_HALF), (SUBLANES, SSM_HALF))
    return bmat.astype(BF16), cmat.astype(BF16), lre, lim


def _in_proj_col_scale():
    s = jnp.ones((IN_WIDTH,), F32)
    s = s.at[SSM_WIDTH:SSM_WIDTH + DIL_WIDTH].set(HEAD_DIM ** -0.5)
    d0 = SSM_WIDTH + 3 * DIL_WIDTH
    s = s.at[d0:d0 + DIFF_WIDTH].set(DIFF_QK_DIM ** -0.5)
    return s.reshape(1, IN_WIDTH)


def kernel(x, norm1_g, w_in, ssm_lam_re, ssm_lam_im, ssm_log_step, ssm_b_re, ssm_b_im, ssm_c_re, ssm_c_im, ssm_d, ssm_w_glu, ssm_norm_g, dil_norm_g, diff_lam_q1, diff_lam_k1, diff_lam_q2, diff_lam_k2, diff_subln_g, w_out, norm2_g, peer_w_query, peer_sub_keys, peer_u, peer_v, final_norm_g):
    batch, seq, _ = x.shape
    n = batch * seq
    depth = w_in.shape[0]
    x2 = x.reshape(n, D_MODEL)
    col_scale = _in_proj_col_scale()
    row = lambda v: v.reshape(1, -1)
    for layer in range(depth):
        lambda_init = 0.8 - 0.6 * math.exp(-0.3 * layer)
        u_tm, dil_qkv, diff_qkv = _in_proj(x2, row(norm1_g[layer]), w_in[layer].astype(BF16),
                                           col_scale, batch, seq)
        bmat, cmat, lre, lim = _ssm_matrices(
            ssm_lam_re[layer], ssm_lam_im[layer], ssm_log_step[layer], ssm_b_re[layer],
            ssm_b_im[layer], ssm_c_re[layer], ssm_c_im[layer])
        y_ssm = _ssm(u_tm.reshape(seq * batch, SSM_WIDTH), bmat, cmat, lre, lim,
                     row(ssm_d[layer]), ssm_w_glu[layer].astype(BF16), row(ssm_norm_g[layer]),
                     batch, seq)
        dil = [_band_attention(dil_qkv, d, batch, seq) for d in DIL_DILATIONS]
        lam = (jnp.exp(jnp.sum(diff_lam_q1[layer] * diff_lam_k1[layer]))
               - jnp.exp(jnp.sum(diff_lam_q2[layer] * diff_lam_k2[layer])) + lambda_init)
        y_diff = _diff_attention(diff_qkv, lam.reshape(1), diff_subln_g[layer].reshape(-1, 1),
                                 1.0 - lambda_init, batch, seq)
        x2 = _out_proj(x2, y_ssm.reshape(seq, batch * SSM_WIDTH), [o for o, _ in dil],
                       [l for _, l in dil], y_diff, row(dil_norm_g[layer]),
                       w_out[layer].astype(BF16), batch, seq)
        wq_heads = peer_w_query[layer].reshape(D_MODEL, PEER_HEADS, 2 * PEER_KEYS)
        wq_heads = wq_heads.transpose(1, 0, 2).astype(BF16)
        keys = peer_sub_keys[layer].reshape(2 * PEER_HEADS, PEER_KEYS, PEER_KEYS).astype(BF16)
        x2 = _peer(x2, row(norm2_g[layer]), wq_heads, keys, peer_u[layer].astype(BF16),
                   peer_v[layer].T.astype(BF16), row(final_norm_g), layer == depth - 1)
    return x2.reshape(batch, seq, D_MODEL)
```

```python
import functools
import math

import jax
import jax.numpy as jnp
from jax import lax
from jax.experimental import pallas as pl
from jax.experimental.pallas import tpu as pltpu

F32 = jnp.float32
BF16 = jnp.bfloat16

D_MODEL = 1024
HEAD_DIM = 64
SSM_WIDTH = 384
SSM_GROUP_CH = 16
SSM_GROUPS = 24
SSM_STATE = 64
SSM_HALF = SSM_GROUPS * SSM_STATE
DIL_WIDTH = 384
DIL_HEADS = 6
DIL_DILATIONS = (1, 4, 16)
DIFF_WIDTH = 256
DIFF_HEADS = 4
DIFF_QK_DIM = 32
IN_WIDTH = 2304
PEER_HEADS = 8
PEER_KEYS = 128
PEER_TOPK = 16
PEER_EXPERTS = PEER_KEYS * PEER_KEYS
NORM_EPS = 1e-6

LANES = 128
SUBLANES = 8
VMEM_LIMIT_BYTES = 56 * 1024 * 1024

PROJ_TOKENS = 512
SSM_CHUNK = 128
ATT_BLOCK = 128
DIFF_BLOCK = 256
PEER_TOKENS = 1024
PEER_GATE_CHUNK = LANES
PEER_EXPERT_BLOCK = 512
PEER_TOKEN_TILE = 256
PEER_SCORE_TOKENS = 256

NEG_INF = float("-inf")


def _rms(x, gain):
    return x * lax.rsqrt(jnp.mean(x * x, axis=-1, keepdims=True) + NORM_EPS) * gain


def _dot_nt(a, b):
    return lax.dot_general(a, b, (((1,), (1,)), ((), ())), preferred_element_type=F32)


def _params(semantics):
    return pltpu.CompilerParams(dimension_semantics=semantics, vmem_limit_bytes=VMEM_LIMIT_BYTES)


def _in_proj_kernel(x_ref, g_ref, w_ref, scale_ref, u_ref, dil_ref, diff_ref):
    h = _rms(x_ref[...], g_ref[...])
    p = jnp.dot(h.astype(BF16), w_ref[...], preferred_element_type=F32) * scale_ref[...]
    u_ref[...] = p[:, :SSM_WIDTH].astype(BF16)
    dil_ref[...] = p[:, SSM_WIDTH:SSM_WIDTH + 3 * DIL_WIDTH].astype(BF16)
    diff_ref[...] = p[:, SSM_WIDTH + 3 * DIL_WIDTH:].astype(BF16)


def _in_proj(x2, gain, w_in, col_scale, batch, seq):
    n = batch * seq
    blocks_per_seq = seq // PROJ_TOKENS
    return pl.pallas_call(
        _in_proj_kernel,
        grid=(n // PROJ_TOKENS,),
        in_specs=[
            pl.BlockSpec((PROJ_TOKENS, D_MODEL), lambda i: (i, 0)),
            pl.BlockSpec((1, D_MODEL), lambda i: (0, 0)),
            pl.BlockSpec((D_MODEL, IN_WIDTH), lambda i: (0, 0)),
            pl.BlockSpec((1, IN_WIDTH), lambda i: (0, 0)),
        ],
        out_specs=[
            pl.BlockSpec((PROJ_TOKENS, SSM_WIDTH),
                         lambda i: (i % blocks_per_seq, i // blocks_per_seq)),
            pl.BlockSpec((PROJ_TOKENS, 3 * DIL_WIDTH), lambda i: (i, 0)),
            pl.BlockSpec((PROJ_TOKENS, 3 * DIFF_WIDTH), lambda i: (i, 0)),
        ],
        out_shape=[
            jax.ShapeDtypeStruct((seq, batch * SSM_WIDTH), BF16),
            jax.ShapeDtypeStruct((n, 3 * DIL_WIDTH), BF16),
            jax.ShapeDtypeStruct((n, 3 * DIFF_WIDTH), BF16),
        ],
        compiler_params=_params(("arbitrary",)),
        name="in_proj",
    )(x2, gain, w_in, col_scale)


def _ssm_kernel(u_ref, bmat_ref, cmat_ref, lre_ref, lim_ref, d_ref, wglu_ref, g_ref,
                y_ref, s_ref, state_ref):
    rows = u_ref.shape[0]
    sub = 256

    @pl.when(pl.program_id(0) == 0)
    def _():
        state_ref[...] = jnp.zeros_like(state_ref)

    for c in range(rows // sub):
        s_ref[c * sub:(c + 1) * sub, :] = jnp.dot(
            u_ref[c * sub:(c + 1) * sub, :], bmat_ref[...], preferred_element_type=F32)

    lre = lre_ref[...]
    lim = lim_ref[...]

    def step(t, carry):
        sre, sim = carry
        r0 = pl.multiple_of(t * SUBLANES, SUBLANES)
        bre = s_ref[pl.ds(r0, SUBLANES), :SSM_HALF]
        bim = s_ref[pl.ds(r0, SUBLANES), SSM_HALF:]
        nre = lre * sre - lim * sim + bre
        nim = lre * sim + lim * sre + bim
        s_ref[pl.ds(r0, SUBLANES), :SSM_HALF] = nre
        s_ref[pl.ds(r0, SUBLANES), SSM_HALF:] = nim
        return nre, nim

    sre, sim = lax.fori_loop(0, rows // SUBLANES, step,
                             (state_ref[:, :SSM_HALF], state_ref[:, SSM_HALF:]), unroll=2)
    state_ref[:, :SSM_HALF] = sre
    state_ref[:, SSM_HALF:] = sim

    for c in range(rows // sub):
        sl = slice(c * sub, (c + 1) * sub)
        y = jnp.dot(s_ref[sl, :].astype(BF16), cmat_ref[...], preferred_element_type=F32)
        y = y + d_ref[...] * u_ref[sl, :].astype(F32)
        g = jax.nn.gelu(y)
        z = jnp.dot(g.astype(BF16), wglu_ref[...], preferred_element_type=F32)
        y = g * jax.nn.sigmoid(z)
        y_ref[sl, :] = _rms(y, g_ref[...]).astype(BF16)


def _ssm(u_tm, bmat, cmat, lre, lim, d_skip, w_glu, gain, batch, seq):
    assert batch == SUBLANES, "the S5 scan keeps one batch per sublane"
    rows = SSM_CHUNK * batch
    const = lambda i: (0, 0)
    return pl.pallas_call(
        _ssm_kernel,
        grid=(seq // SSM_CHUNK,),
        in_specs=[
            pl.BlockSpec((rows, SSM_WIDTH), lambda i: (i, 0)),
            pl.BlockSpec((SSM_WIDTH, 2 * SSM_HALF), const),
            pl.BlockSpec((2 * SSM_HALF, SSM_WIDTH), const),
            pl.BlockSpec((SUBLANES, SSM_HALF), const),
            pl.BlockSpec((SUBLANES, SSM_HALF), const),
            pl.BlockSpec((1, SSM_WIDTH), const),
            pl.BlockSpec((SSM_WIDTH, SSM_WIDTH), const),
            pl.BlockSpec((1, SSM_WIDTH), const),
        ],
        out_specs=pl.BlockSpec((rows, SSM_WIDTH), lambda i: (i, 0)),
        out_shape=jax.ShapeDtypeStruct((seq * batch, SSM_WIDTH), BF16),
        scratch_shapes=[
            pltpu.VMEM((rows, 2 * SSM_HALF), F32),
            pltpu.VMEM((SUBLANES, 2 * SSM_HALF), F32),
        ],
        compiler_params=_params(("arbitrary",)),
        name="ssm",
    )(u_tm, bmat, cmat, lre, lim, d_skip, w_glu, gain)


def _band_kernel(has_prev, *refs):
    if has_prev:
        q_ref, kc_ref, vc_ref, kp_ref, vp_ref, o_ref, lse_ref = refs
    else:
        q_ref, kc_ref, vc_ref, o_ref, lse_ref = refs
    blk = pl.program_id(2)
    row = lax.broadcasted_iota(jnp.int32, (ATT_BLOCK, ATT_BLOCK), 0)
    col = lax.broadcasted_iota(jnp.int32, (ATT_BLOCK, ATT_BLOCK), 1)
    cur_ok = col <= row
    if has_prev:
        prev_ok = jnp.logical_and(col >= row, blk > 0)
    lane = lax.broadcasted_iota(jnp.int32, (ATT_BLOCK, LANES), 1)
    lse_tile = jnp.zeros((ATT_BLOCK, LANES), F32)
    heads = [slice(h * HEAD_DIM, (h + 1) * HEAD_DIM) for h in range(DIL_HEADS)]
    sc_all = [_dot_nt(q_ref[0, :, hs], kc_ref[0, :, hs]) for hs in heads]
    if has_prev:
        sp_all = [_dot_nt(q_ref[0, :, hs], kp_ref[0, :, hs]) for hs in heads]
    pc_all, pp_all, den_all = [], [], []
    for h in range(DIL_HEADS):
        sc = jnp.where(cur_ok, sc_all[h], NEG_INF)
        m = jnp.max(sc, axis=-1, keepdims=True)
        if has_prev:
            sp = jnp.where(prev_ok, sp_all[h], NEG_INF)
            m = jnp.maximum(m, jnp.max(sp, axis=-1, keepdims=True))
        pc = jnp.exp(sc - m)
        den = jnp.sum(pc, axis=-1, keepdims=True)
        if has_prev:
            pp = jnp.exp(sp - m)
            den = den + jnp.sum(pp, axis=-1, keepdims=True)
            pp_all.append(pp.astype(BF16))
        pc_all.append(pc.astype(BF16))
        den_all.append(den)
        lse_tile = jnp.where(lane == h, m + jnp.log(den), lse_tile)
    for h, hs in enumerate(heads):
        acc = jnp.dot(pc_all[h], vc_ref[0, :, hs], preferred_element_type=F32)
        if has_prev:
            acc = acc + jnp.dot(pp_all[h], vp_ref[0, :, hs], preferred_element_type=F32)
        o_ref[0, :, hs] = (acc / den_all[h]).astype(BF16)
    lse_ref[0] = lse_tile


def _band_attention(qkv, dilation, batch, seq):
    rows = seq // dilation
    nblk = rows // ATT_BLOCK
    has_prev = nblk > 1
    view = qkv.reshape(batch, rows, dilation * 3 * DIL_WIDTH)
    blk = (1, ATT_BLOCK, DIL_WIDTH)
    in_specs = [
        pl.BlockSpec(blk, lambda b, r, i: (b, i, 3 * r)),
        pl.BlockSpec(blk, lambda b, r, i: (b, i, 3 * r + 1)),
        pl.BlockSpec(blk, lambda b, r, i: (b, i, 3 * r + 2)),
    ]
    args = [view, view, view]
    if has_prev:
        in_specs += [
            pl.BlockSpec(blk, lambda b, r, i: (b, jnp.maximum(i - 1, 0), 3 * r + 1)),
            pl.BlockSpec(blk, lambda b, r, i: (b, jnp.maximum(i - 1, 0), 3 * r + 2)),
        ]
        args += [view, view]
    out, lse = pl.pallas_call(
        functools.partial(_band_kernel, has_prev),
        grid=(batch, dilation, nblk),
        in_specs=in_specs,
        out_specs=[
            pl.BlockSpec(blk, lambda b, r, i: (b, i, r)),
            pl.BlockSpec((1, ATT_BLOCK, LANES), lambda b, r, i: (b, i, r)),
        ],
        out_shape=[
            jax.ShapeDtypeStruct((batch, rows, dilation * DIL_WIDTH), BF16),
            jax.ShapeDtypeStruct((batch, rows, dilation * LANES), F32),
        ],
        compiler_params=_params(("arbitrary", "arbitrary", "arbitrary")),
        name=f"dilated_d{dilation}",
    )(*args)
    n = batch * seq
    return out.reshape(n, DIL_WIDTH), lse.reshape(n, LANES)


def _diff_kernel(out_scale, lam_ref, q_ref, k_ref, v_ref, g_ref, o_ref, qt_ref, vt_ref, ot_ref,
                 qm_ref, m_ref, l_ref, acc_ref):
    seq = q_ref.shape[0]
    nblk = seq // DIFF_BLOCK
    lam = lam_ref[0]
    for i in range(nblk):
        rows = slice(i * DIFF_BLOCK, (i + 1) * DIFF_BLOCK)
        qt_ref[i] = q_ref[rows, :].astype(F32).T.astype(BF16)
        vt_ref[i] = v_ref[rows, :].astype(F32).T.astype(BF16)
    krow = lax.broadcasted_iota(jnp.int32, (DIFF_BLOCK, DIFF_BLOCK), 0)
    qcol = lax.broadcasted_iota(jnp.int32, (DIFF_BLOCK, DIFF_BLOCK), 1)
    causal = krow <= qcol
    chan = lax.broadcasted_iota(jnp.int32, (DIFF_WIDTH, DIFF_BLOCK), 0)

    n_maps = 2 * DIFF_HEADS

    def attend(kj, mask):
        k0 = pl.multiple_of(kj * DIFF_BLOCK, DIFF_BLOCK)
        k = k_ref[pl.ds(k0, DIFF_BLOCK), :]
        scores = [jnp.dot(k, qm_ref[c], preferred_element_type=F32) for c in range(n_maps)]
        probs, alphas = [], []
        for c in range(n_maps):
            s = scores[c]
            if mask:
                s = jnp.where(causal, s, NEG_INF)
            m_old = m_ref[c]
            m_new = jnp.maximum(m_old, jnp.max(s, axis=0, keepdims=True))
            alpha = jnp.exp(m_old - m_new)
            p = jnp.exp(s - m_new)
            m_ref[c] = m_new
            l_ref[c] = alpha * l_ref[c] + jnp.sum(p, axis=0, keepdims=True)
            probs.append(p.astype(BF16))
            alphas.append(alpha)
        for c in range(n_maps):
            h = c // 2
            vt = vt_ref[kj, h * HEAD_DIM:(h + 1) * HEAD_DIM, :]
            acc_ref[c] = alphas[c] * acc_ref[c] + jnp.dot(vt, probs[c],
                                                          preferred_element_type=F32)

    def q_block(qi, _):
        qt = qt_ref[qi]
        for c in range(n_maps):
            lo = c * DIFF_QK_DIM
            qm_ref[c] = jnp.where(jnp.logical_and(chan >= lo, chan < lo + DIFF_QK_DIM), qt,
                                  jnp.zeros((), BF16))
        m_ref[...] = jnp.full(m_ref.shape, NEG_INF, F32)
        l_ref[...] = jnp.zeros_like(l_ref)
        acc_ref[...] = jnp.zeros_like(acc_ref)

        def k_block(kj, _):
            attend(kj, False)
            return 0

        lax.fori_loop(0, qi, k_block, 0)
        attend(qi, True)
        for h in range(DIFF_HEADS):
            o = (acc_ref[2 * h] / l_ref[2 * h]
                 - lam * (acc_ref[2 * h + 1] / l_ref[2 * h + 1]))
            o = o * lax.rsqrt(jnp.mean(o * o, axis=0, keepdims=True) + NORM_EPS)
            ot_ref[qi, h * HEAD_DIM:(h + 1) * HEAD_DIM, :] = o * (g_ref[...] * out_scale)
        return 0

    lax.fori_loop(0, nblk, q_block, 0)

    for i in range(nblk):
        o_ref[i * DIFF_BLOCK:(i + 1) * DIFF_BLOCK, :] = ot_ref[i].T.astype(BF16)


def _diff_attention(qkv, lam, subln_g_col, out_scale, batch, seq):
    n = batch * seq
    blk = (seq, DIFF_WIDTH)
    nblk = seq // DIFF_BLOCK
    return pl.pallas_call(
        functools.partial(_diff_kernel, out_scale),
        grid=(batch,),
        in_specs=[
            pl.BlockSpec(memory_space=pltpu.SMEM),
            pl.BlockSpec(blk, lambda b: (b, 0)),
            pl.BlockSpec(blk, lambda b: (b, 1)),
            pl.BlockSpec(blk, lambda b: (b, 2)),
            pl.BlockSpec((HEAD_DIM, 1), lambda b: (0, 0)),
        ],
        out_specs=pl.BlockSpec(blk, lambda b: (b, 0)),
        out_shape=jax.ShapeDtypeStruct((n, DIFF_WIDTH), BF16),
        scratch_shapes=[
            pltpu.VMEM((nblk, DIFF_WIDTH, DIFF_BLOCK), BF16),
            pltpu.VMEM((nblk, DIFF_WIDTH, DIFF_BLOCK), BF16),
            pltpu.VMEM((nblk, DIFF_WIDTH, DIFF_BLOCK), F32),
            pltpu.VMEM((2 * DIFF_HEADS, DIFF_WIDTH, DIFF_BLOCK), BF16),
            pltpu.VMEM((2 * DIFF_HEADS, 1, DIFF_BLOCK), F32),
            pltpu.VMEM((2 * DIFF_HEADS, 1, DIFF_BLOCK), F32),
            pltpu.VMEM((2 * DIFF_HEADS, HEAD_DIM, DIFF_BLOCK), F32),
        ],
        compiler_params=_params(("arbitrary",)),
        name="diff_attention",
    )(lam, qkv, qkv, qkv, subln_g_col)


def _out_proj_kernel(x_ref, ssm_ref, o1_ref, o2_ref, o3_ref, l1_ref, l2_ref, l3_ref, diff_ref,
                     g_ref, w_ref, y_ref):
    l1, l2, l3 = l1_ref[...], l2_ref[...], l3_ref[...]
    m = jnp.maximum(jnp.maximum(l1, l2), l3)
    e1, e2, e3 = jnp.exp(l1 - m), jnp.exp(l2 - m), jnp.exp(l3 - m)
    inv = 1.0 / (e1 + e2 + e3)
    head_of_col = lax.broadcasted_iota(jnp.int32, (LANES, DIL_WIDTH), 1) // HEAD_DIM
    spread = (lax.broadcasted_iota(jnp.int32, (LANES, DIL_WIDTH), 0) == head_of_col).astype(BF16)
    mix = jnp.zeros((x_ref.shape[0], DIL_WIDTH), F32)
    for e, o_ref in ((e1, o1_ref), (e2, o2_ref), (e3, o3_ref)):
        w = jnp.dot((e * inv).astype(BF16), spread, preferred_element_type=F32)
        mix = mix + w * o_ref[...].astype(F32)
    y_dil = _rms(mix, g_ref[...]).astype(BF16)
    acc = jnp.dot(ssm_ref[...], w_ref[:SSM_WIDTH, :], preferred_element_type=F32)
    acc = acc + jnp.dot(y_dil, w_ref[SSM_WIDTH:SSM_WIDTH + DIL_WIDTH, :],
                        preferred_element_type=F32)
    acc = acc + jnp.dot(diff_ref[...], w_ref[SSM_WIDTH + DIL_WIDTH:, :],
                        preferred_element_type=F32)
    y_ref[...] = x_ref[...] + acc


def _out_proj(x2, y_ssm_tm, dil_outs, dil_lses, y_diff, dil_gain, w_out, batch, seq):
    n = batch * seq
    blocks_per_seq = seq // PROJ_TOKENS
    tok = lambda w: pl.BlockSpec((PROJ_TOKENS, w), lambda i: (i, 0))
    return pl.pallas_call(
        _out_proj_kernel,
        grid=(n // PROJ_TOKENS,),
        in_specs=[
            tok(D_MODEL),
            pl.BlockSpec((PROJ_TOKENS, SSM_WIDTH),
                         lambda i: (i % blocks_per_seq, i // blocks_per_seq)),
            tok(DIL_WIDTH), tok(DIL_WIDTH), tok(DIL_WIDTH),
            tok(LANES), tok(LANES), tok(LANES),
            tok(DIFF_WIDTH),
            pl.BlockSpec((1, DIL_WIDTH), lambda i: (0, 0)),
            pl.BlockSpec((D_MODEL, D_MODEL), lambda i: (0, 0)),
        ],
        out_specs=tok(D_MODEL),
        out_shape=jax.ShapeDtypeStruct((n, D_MODEL), F32),
        compiler_params=_params(("arbitrary",)),
        name="out_proj",
    )(x2, y_ssm_tm, *dil_outs, *dil_lses, y_diff, dil_gain, w_out)


def _sorting_network(n):
    pairs = []
    p = 1
    while p < n:
        k = p
        while k >= 1:
            for j in range(k % p, n - k, 2 * k):
                for i in range(min(k, n - j - k)):
                    if (i + j) // (2 * p) == (i + j + k) // (2 * p):
                        pairs.append((i + j, i + j + k))
            k //= 2
        p *= 2
    return pairs


_SORT16 = _sorting_network(PEER_TOPK)
_BITONIC16 = [(i, i + s) for s in (8, 4, 2, 1) for i in range(PEER_TOPK) if (i // s) % 2 == 0]


def _top16_desc(vals):
    v = list(vals)
    for a, b in _SORT16:
        hi, lo = jnp.maximum(v[a], v[b]), jnp.minimum(v[a], v[b])
        v[a], v[b] = hi, lo
    for shift in (4, 2, 1):
        other = [pltpu.roll(x, shift, 0) for x in v]
        v = [jnp.maximum(v[k], other[PEER_TOPK - 1 - k]) for k in range(PEER_TOPK)]
        for a, b in _BITONIC16:
            hi, lo = jnp.maximum(v[a], v[b]), jnp.minimum(v[a], v[b])
            v[a], v[b] = hi, lo
    return v


def _peer_gates(sa, sb):
    t = sa.shape[1]
    a_top = _top16_desc([sa[SUBLANES * v:SUBLANES * (v + 1), :] for v in range(PEER_KEYS // SUBLANES)])
    b_top = _top16_desc([sb[SUBLANES * v:SUBLANES * (v + 1), :] for v in range(PEER_KEYS // SUBLANES)])
    sub = lax.broadcasted_iota(jnp.int32, (SUBLANES, t), 0)

    def pack(rows):
        out = rows[0]
        for s in range(1, SUBLANES):
            out = jnp.where(sub == s, rows[s], out)
        return out

    b_lo, b_hi, a_hi = pack(b_top[:8]), pack(b_top[8:]), pack(a_top[8:])
    cands = [a_top[0] + b_lo, a_top[0] + b_hi]
    cands += [a_top[k] + b_lo for k in range(1, 8)]
    cands += [a_hi + b_top[0]]
    valid = [None, None] + [sub < (PEER_TOPK // (k + 1)) for k in range(1, 8)] + [None]
    cands = [c if ok is None else jnp.where(ok, c, NEG_INF) for c, ok in zip(cands, valid)]
    pad = jnp.full((SUBLANES, t), NEG_INF, F32)
    tau = _top16_desc(cands + [pad] * (PEER_TOPK - len(cands)))[PEER_TOPK - 1]
    top = a_top[0] + b_top[0]
    z = jnp.zeros((SUBLANES, t), F32)
    for c in cands:
        z = z + jnp.where(c >= tau, jnp.exp(c - top), 0.0)
    for shift in (4, 2, 1):
        z = z + pltpu.roll(z, shift, 0)
    inv_z = 1.0 / z
    last = PEER_TOPK - 1

    def search(test):
        total = None
        bits = []
        for level, weight in enumerate((8, 4, 2, 1)):
            leaves = [b_top[m] for m in range(weight - 1, last, 2 * weight)]
            for bit in reversed(bits):
                leaves = [jnp.where(bit, leaves[2 * n + 1], leaves[2 * n])
                          for n in range(len(leaves) // 2)]
            bit = test(leaves[0])
            bits.append(bit)
            term = jnp.where(bit, float(weight), 0.0)
            total = term if total is None else total + term
        return total

    wa, count, wb, rank = [], [], [], []
    for v in range(PEER_KEYS // SUBLANES):
        xa = sa[SUBLANES * v:SUBLANES * (v + 1), :]
        xb = sb[SUBLANES * v:SUBLANES * (v + 1), :]
        wa.append(jnp.where(xa >= a_top[last], jnp.exp(xa - a_top[0]), 0.0) * inv_z)
        wb.append(jnp.where(xb >= b_top[last], jnp.exp(xb - b_top[0]), 0.0))
        count.append(search(lambda b: xa + b >= tau)
                     + jnp.where(xa + b_top[last] >= tau, 1.0, 0.0))
        rank.append(search(lambda b: b >= xb) + jnp.where(b_top[last] >= xb, 1.0, 0.0))
    cat = lambda parts: jnp.concatenate(parts, axis=0)
    return cat(wa), cat(count), cat(wb), cat(rank)


def _gelu_tanh(a):
    c = math.sqrt(2.0 / math.pi)
    inner = a * (a * a * (c * 0.044715) + c)
    return (a * 0.5) * (jnp.tanh(inner) + 1.0)


def _peer_kernel(final, x_ref, g_ref, wq_ref, keys_ref, u_ref, vt_ref, fg_ref, y_ref,
                 h_ref, sc_ref, wa_ref, cnt_ref, wb_ref, rank_ref, a0_ref, a1_ref, p_ref, acc_ref):
    s = pl.program_id(1)
    n_blocks = PEER_EXPERTS // PEER_EXPERT_BLOCK
    tb = x_ref.shape[0]
    n_chunks = tb // PEER_GATE_CHUNK
    rows_per_step = PEER_EXPERT_BLOCK // PEER_KEYS

    @pl.when(s == 0)
    def _():
        h_ref[...] = _rms(x_ref[...], g_ref[...]).astype(BF16)
        acc_ref[...] = jnp.zeros_like(acc_ref)
        a1_ref[...] = jnp.zeros_like(a1_ref)

        chunks_per_stage = PEER_SCORE_TOKENS // PEER_GATE_CHUNK

        def stage(si, _):
            r0 = pl.multiple_of(si * PEER_SCORE_TOKENS, PEER_SCORE_TOKENS)
            hs = h_ref[pl.ds(r0, PEER_SCORE_TOKENS), :]
            qs = [jnp.dot(hs, wq_ref[hd], preferred_element_type=F32).astype(BF16)
                  for hd in range(PEER_HEADS)]
            for hd in range(PEER_HEADS):
                for half in range(2):
                    sc = _dot_nt(keys_ref[2 * hd + half],
                                 qs[hd][:, half * PEER_KEYS:(half + 1) * PEER_KEYS])
                    for c in range(chunks_per_stage):
                        sc_ref[c, 2 * hd + half] = sc[:, c * PEER_GATE_CHUNK:
                                                      (c + 1) * PEER_GATE_CHUNK]

            def unit(k, _):
                c = k // (PEER_HEADS // 2)
                ci = si * chunks_per_stage + c
                for hd_local in range(2):
                    hd = (k % (PEER_HEADS // 2)) * 2 + hd_local
                    wa, count, wb, rank = _peer_gates(sc_ref[c, 2 * hd], sc_ref[c, 2 * hd + 1])
                    wa_ref[ci, hd] = wa
                    cnt_ref[ci, hd] = count
                    wb_ref[ci, hd] = wb.astype(BF16)
                    rank_ref[ci, hd] = rank.astype(BF16)
                return 0

            lax.fori_loop(0, chunks_per_stage * PEER_HEADS // 2, unit, 0)
            return 0

        lax.fori_loop(0, tb // PEER_SCORE_TOKENS, stage, 0)

    packed_rows = 2 * SUBLANES
    tiles = PEER_KEYS // packed_rows

    def step(par):
        a_new, a_old = (a0_ref, a1_ref) if par == 0 else (a1_ref, a0_ref)
        gate_block = jnp.clip(s - 1, 0, n_blocks - 1)
        for t0 in range(0, tb, PEER_TOKEN_TILE):
            tok = slice(t0, t0 + PEER_TOKEN_TILE)
            a_new[:, tok] = _dot_nt(u_ref[...], h_ref[tok, :]).astype(BF16)
            for ci in range(t0 // PEER_GATE_CHUNK, (t0 + PEER_TOKEN_TILE) // PEER_GATE_CHUNK):
                cols = slice(ci * PEER_GATE_CHUNK, (ci + 1) * PEER_GATE_CHUNK)
                for il in range(rows_per_step):
                    i = gate_block * rows_per_step + il
                    rows = slice(il * PEER_KEYS, (il + 1) * PEER_KEYS)
                    gate = jnp.zeros((tiles, packed_rows, PEER_GATE_CHUNK), BF16)
                    for hd in range(PEER_HEADS):
                        row = lambda ref: jnp.broadcast_to(
                            ref[ci, hd, pl.ds(i, 1), :],
                            (packed_rows, PEER_GATE_CHUNK)).astype(BF16)[None]
                        rank = rank_ref[ci, hd].reshape(tiles, packed_rows, PEER_GATE_CHUNK)
                        wb = wb_ref[ci, hd].reshape(tiles, packed_rows, PEER_GATE_CHUNK)
                        gate = gate + jnp.where(rank <= row(cnt_ref), wb * row(wa_ref),
                                                jnp.zeros((), BF16))
                    gate = gate.reshape(PEER_KEYS, PEER_GATE_CHUNK)
                    p_ref[rows, cols] = gate * _gelu_tanh(a_old[rows, cols])
            acc_ref[:, tok] += jnp.dot(vt_ref[...], p_ref[:, tok], preferred_element_type=F32)

    @pl.when(s % 2 == 0)
    def _():
        step(0)

    @pl.when(s % 2 == 1)
    def _():
        step(1)

    @pl.when(s == pl.num_programs(1) - 1)
    def _():
        y = x_ref[...] + acc_ref[...].T
        if final:
            y = _rms(y, fg_ref[...])
        y_ref[...] = y


def _peer(x2, gain, wq_heads, keys, u_bf, vt_bf, final_gain, final):
    n = x2.shape[0]
    n_chunks = PEER_TOKENS // PEER_GATE_CHUNK
    n_blocks = PEER_EXPERTS // PEER_EXPERT_BLOCK
    gate_shape = (n_chunks, PEER_HEADS, PEER_KEYS, PEER_GATE_CHUNK)
    act_block = lambda s: jnp.minimum(s, n_blocks - 1)
    out_block = lambda s: jnp.maximum(s - 1, 0)
    once = pl.Buffered(1)
    return pl.pallas_call(
        functools.partial(_peer_kernel, final),
        grid=(n // PEER_TOKENS, n_blocks + 1),
        in_specs=[
            pl.BlockSpec((PEER_TOKENS, D_MODEL), lambda t, s: (t, 0), pipeline_mode=once),
            pl.BlockSpec((1, D_MODEL), lambda t, s: (0, 0)),
            pl.BlockSpec((PEER_HEADS, D_MODEL, 2 * PEER_KEYS), lambda t, s: (0, 0, 0),
                         pipeline_mode=once),
            pl.BlockSpec((2 * PEER_HEADS, PEER_KEYS, PEER_KEYS), lambda t, s: (0, 0, 0)),
            pl.BlockSpec((PEER_EXPERT_BLOCK, D_MODEL), lambda t, s: (act_block(s), 0)),
            pl.BlockSpec((D_MODEL, PEER_EXPERT_BLOCK), lambda t, s: (0, out_block(s))),
            pl.BlockSpec((1, D_MODEL), lambda t, s: (0, 0)),
        ],
        out_specs=pl.BlockSpec((PEER_TOKENS, D_MODEL), lambda t, s: (t, 0)),
        out_shape=jax.ShapeDtypeStruct((n, D_MODEL), F32),
        scratch_shapes=[
            pltpu.VMEM((PEER_TOKENS, D_MODEL), BF16),
            pltpu.VMEM((PEER_SCORE_TOKENS // PEER_GATE_CHUNK, 2 * PEER_HEADS, PEER_KEYS,
                        PEER_GATE_CHUNK), F32),
            pltpu.VMEM(gate_shape, F32),
            pltpu.VMEM(gate_shape, F32),
            pltpu.VMEM(gate_shape, BF16),
            pltpu.VMEM(gate_shape, BF16),
            pltpu.VMEM((PEER_EXPERT_BLOCK, PEER_TOKENS), BF16),
            pltpu.VMEM((PEER_EXPERT_BLOCK, PEER_TOKENS), BF16),
            pltpu.VMEM((PEER_EXPERT_BLOCK, PEER_TOKENS), BF16),
            pltpu.VMEM((D_MODEL, PEER_TOKENS), F32),
        ],
        compiler_params=_params(("arbitrary", "arbitrary")),
        name="peer",
    )(x2, gain, wq_heads, keys, u_bf, vt_bf, final_gain)


def _ssm_matrices(lam_re, lam_im, log_step, b_re, b_im, c_re, c_im):
    lam = lax.complex(lam_re, lam_im)
    step = jnp.exp(log_step)[:, None]
    lam_bar = jnp.exp(lam * step)
    b_bar = ((lam_bar - 1.0) / lam)[:, :, None] * lax.complex(b_re, b_im)
    eye = jnp.eye(SSM_GROUPS, dtype=F32)

    def embed_in(m):
        return jnp.einsum("gpc,gh->gchp", m, eye).reshape(SSM_WIDTH, SSM_HALF)

    def embed_out(m):
        return jnp.einsum("gcp,gh->gphc", m, eye).reshape(SSM_HALF, SSM_WIDTH)

    bmat = jnp.concatenate([embed_in(jnp.real(b_bar)), embed_in(jnp.imag(b_bar))], axis=1)
    cmat = jnp.concatenate([embed_out(c_re), embed_out(-c_im)], axis=0)
    lre = jnp.broadcast_to(jnp.real(lam_bar).reshape(1, SSM_HALF), (SUBLANES, SSM_HALF))
    lim = jnp.broadcast_to(jnp.imag(lam_bar).reshape(1, SSM_HALF), (SUBLANES, SSM_HALF))
    return bmat.astype(BF16), cmat.astype(BF16), lre, lim


def _in_proj_col_scale():
    s = jnp.ones((IN_WIDTH,), F32)
    s = s.at[SSM_WIDTH:SSM_WIDTH + DIL_WIDTH].set(HEAD_DIM ** -0.5)
    d0 = SSM_WIDTH + 3 * DIL_WIDTH
    s = s.at[d0:d0 + DIFF_WIDTH].set(DIFF_QK_DIM ** -0.5)
    return s.reshape(1, IN_WIDTH)


def kernel(x, norm1_g, w_in, ssm_lam_re, ssm_lam_im, ssm_log_step, ssm_b_re, ssm_b_im, ssm_c_re, ssm_c_im, ssm_d, ssm_w_glu, ssm_norm_g, dil_norm_g, diff_lam_q1, diff_lam_k1, diff_lam_q2, diff_lam_k2, diff_subln_g, w_out, norm2_g, peer_w_query, peer_sub_keys, peer_u, peer_v, final_norm_g):
    batch, seq, _ = x.shape
    n = batch * seq
    depth = w_in.shape[0]
    x2 = x.reshape(n, D_MODEL)
    col_scale = _in_proj_col_scale()
    row = lambda v: v.reshape(1, -1)
    for layer in range(depth):
        lambda_init = 0.8 - 0.6 * math.exp(-0.3 * layer)
        u_tm, dil_qkv, diff_qkv = _in_proj(x2, row(norm1_g[layer]), w_in[layer].astype(BF16),
                                           col_scale, batch, seq)
        bmat, cmat, lre, lim = _ssm_matrices(
            ssm_lam_re[layer], ssm_lam_im[layer], ssm_log_step[layer], ssm_b_re[layer],
            ssm_b_im[layer], ssm_c_re[layer], ssm_c_im[layer])
        y_ssm = _ssm(u_tm.reshape(seq * batch, SSM_WIDTH), bmat, cmat, lre, lim,
                     row(ssm_d[layer]), ssm_w_glu[layer].astype(BF16), row(ssm_norm_g[layer]),
                     batch, seq)
        dil = [_band_attention(dil_qkv, d, batch, seq) for d in DIL_DILATIONS]
        lam = (jnp.exp(jnp.sum(diff_lam_q1[layer] * diff_lam_k1[layer]))
               - jnp.exp(jnp.sum(diff_lam_q2[layer] * diff_lam_k2[layer])) + lambda_init)
        y_diff = _diff_attention(diff_qkv, lam.reshape(1), diff_subln_g[layer].reshape(-1, 1),
                                 1.0 - lambda_init, batch, seq)
        x2 = _out_proj(x2, y_ssm.reshape(seq, batch * SSM_WIDTH), [o for o, _ in dil],
                       [l for _, l in dil], y_diff, row(dil_norm_g[layer]),
                       w_out[layer].astype(BF16), batch, seq)
        wq_heads = peer_w_query[layer].reshape(D_MODEL, PEER_HEADS, 2 * PEER_KEYS)
        wq_heads = wq_heads.transpose(1, 0, 2).astype(BF16)
        keys = peer_sub_keys[layer].reshape(2 * PEER_HEADS, PEER_KEYS, PEER_KEYS).astype(BF16)
        x2 = _peer(x2, row(norm2_g[layer]), wq_heads, keys, peer_u[layer].astype(BF16),
                   peer_v[layer].T.astype(BF16), row(final_norm_g), layer == depth - 1)
    return x2.reshape(batch, seq, D_MODEL)
```

```python
import functools
import math

import jax
import jax.numpy as jnp
from jax import lax
from jax.experimental import pallas as pl
from jax.experimental.pallas import tpu as pltpu

F32 = jnp.float32
BF16 = jnp.bfloat16

D_MODEL = 1024
HEAD_DIM = 64
SSM_WIDTH = 384
SSM_GROUP_CH = 16
SSM_GROUPS = 24
SSM_STATE = 64
SSM_HALF = SSM_GROUPS * SSM_STATE
DIL_WIDTH = 384
DIL_HEADS = 6
DIL_DILATIONS = (1, 4, 16)
DIFF_WIDTH = 256
DIFF_HEADS = 4
DIFF_QK_DIM = 32
IN_WIDTH = 2304
PEER_HEADS = 8
PEER_KEYS = 128
PEER_TOPK = 16
PEER_EXPERTS = PEER_KEYS * PEER_KEYS
NORM_EPS = 1e-6

LANES = 128
SUBLANES = 8
VMEM_LIMIT_BYTES = 56 * 1024 * 1024

PROJ_TOKENS = 512
SSM_CHUNK = 128
ATT_BLOCK = 128
DIFF_BLOCK = 256
PEER_TOKENS = 1024
PEER_GATE_CHUNK = LANES
PEER_EXPERT_BLOCK = 512
PEER_TOKEN_TILE = 256
PEER_SCORE_TOKENS = 256

NEG_INF = float("-inf")


def _rms(x, gain):
    return x * lax.rsqrt(jnp.mean(x * x, axis=-1, keepdims=True) + NORM_EPS) * gain


def _dot_nt(a, b):
    return lax.dot_general(a, b, (((1,), (1,)), ((), ())), preferred_element_type=F32)


def _params(semantics):
    return pltpu.CompilerParams(dimension_semantics=semantics, vmem_limit_bytes=VMEM_LIMIT_BYTES)


def _in_proj_kernel(x_ref, g_ref, w_ref, scale_ref, u_ref, dil_ref, diff_ref):
    h = _rms(x_ref[...], g_ref[...])
    p = jnp.dot(h.astype(BF16), w_ref[...], preferred_element_type=F32) * scale_ref[...]
    u_ref[...] = p[:, :SSM_WIDTH].astype(BF16)
    for t in range(dil_ref.shape[0]):
        dil_ref[t] = p[:, SSM_WIDTH + t * LANES:SSM_WIDTH + (t + 1) * LANES]
    diff_ref[...] = p[:, SSM_WIDTH + 3 * DIL_WIDTH:].astype(BF16)


def _in_proj(x2, gain, w_in, col_scale, batch, seq):
    n = batch * seq
    blocks_per_seq = seq // PROJ_TOKENS
    return pl.pallas_call(
        _in_proj_kernel,
        grid=(n // PROJ_TOKENS,),
        in_specs=[
            pl.BlockSpec((PROJ_TOKENS, D_MODEL), lambda i: (i, 0)),
            pl.BlockSpec((1, D_MODEL), lambda i: (0, 0)),
            pl.BlockSpec((D_MODEL, IN_WIDTH), lambda i: (0, 0)),
            pl.BlockSpec((1, IN_WIDTH), lambda i: (0, 0)),
        ],
        out_specs=[
            pl.BlockSpec((PROJ_TOKENS, SSM_WIDTH),
                         lambda i: (i % blocks_per_seq, i // blocks_per_seq)),
            pl.BlockSpec((3 * DIL_WIDTH // LANES, PROJ_TOKENS, LANES), lambda i: (0, i, 0)),
            pl.BlockSpec((PROJ_TOKENS, 3 * DIFF_WIDTH), lambda i: (i, 0)),
        ],
        out_shape=[
            jax.ShapeDtypeStruct((seq, batch * SSM_WIDTH), BF16),
            jax.ShapeDtypeStruct((3 * DIL_WIDTH // LANES, n, LANES), F32),
            jax.ShapeDtypeStruct((n, 3 * DIFF_WIDTH), BF16),
        ],
        compiler_params=_params(("arbitrary",)),
        name="in_proj",
    )(x2, gain, w_in, col_scale)


def _ssm_kernel(u_ref, bmat_ref, cmat_ref, lre_ref, lim_ref, d_ref, wglu_ref, g_ref,
                y_ref, s_ref, state_ref):
    rows = u_ref.shape[0]
    sub = 256

    @pl.when(pl.program_id(0) == 0)
    def _():
        state_ref[...] = jnp.zeros_like(state_ref)

    for c in range(rows // sub):
        s_ref[c * sub:(c + 1) * sub, :] = jnp.dot(
            u_ref[c * sub:(c + 1) * sub, :], bmat_ref[...], preferred_element_type=F32)

    lre = lre_ref[...]
    lim = lim_ref[...]

    def step(t, carry):
        sre, sim = carry
        r0 = pl.multiple_of(t * SUBLANES, SUBLANES)
        bre = s_ref[pl.ds(r0, SUBLANES), :SSM_HALF]
        bim = s_ref[pl.ds(r0, SUBLANES), SSM_HALF:]
        nre = lre * sre - lim * sim + bre
        nim = lre * sim + lim * sre + bim
        s_ref[pl.ds(r0, SUBLANES), :SSM_HALF] = nre
        s_ref[pl.ds(r0, SUBLANES), SSM_HALF:] = nim
        return nre, nim

    sre, sim = lax.fori_loop(0, rows // SUBLANES, step,
                             (state_ref[:, :SSM_HALF], state_ref[:, SSM_HALF:]), unroll=2)
    state_ref[:, :SSM_HALF] = sre
    state_ref[:, SSM_HALF:] = sim

    for c in range(rows // sub):
        sl = slice(c * sub, (c + 1) * sub)
        y = jnp.dot(s_ref[sl, :].astype(BF16), cmat_ref[...], preferred_element_type=F32)
        y = y + d_ref[...] * u_ref[sl, :].astype(F32)
        g = jax.nn.gelu(y)
        z = jnp.dot(g.astype(BF16), wglu_ref[...], preferred_element_type=F32)
        y = g * jax.nn.sigmoid(z)
        y_ref[sl, :] = _rms(y, g_ref[...]).astype(BF16)


def _ssm(u_tm, bmat, cmat, lre, lim, d_skip, w_glu, gain, batch, seq):
    assert batch == SUBLANES, "the S5 scan keeps one batch per sublane"
    rows = SSM_CHUNK * batch
    const = lambda i: (0, 0)
    return pl.pallas_call(
        _ssm_kernel,
        grid=(seq // SSM_CHUNK,),
        in_specs=[
            pl.BlockSpec((rows, SSM_WIDTH), lambda i: (i, 0)),
            pl.BlockSpec((SSM_WIDTH, 2 * SSM_HALF), const),
            pl.BlockSpec((2 * SSM_HALF, SSM_WIDTH), const),
            pl.BlockSpec((SUBLANES, SSM_HALF), const),
            pl.BlockSpec((SUBLANES, SSM_HALF), const),
            pl.BlockSpec((1, SSM_WIDTH), const),
            pl.BlockSpec((SSM_WIDTH, SSM_WIDTH), const),
            pl.BlockSpec((1, SSM_WIDTH), const),
        ],
        out_specs=pl.BlockSpec((rows, SSM_WIDTH), lambda i: (i, 0)),
        out_shape=jax.ShapeDtypeStruct((seq * batch, SSM_WIDTH), BF16),
        scratch_shapes=[
            pltpu.VMEM((rows, 2 * SSM_HALF), F32),
            pltpu.VMEM((SUBLANES, 2 * SSM_HALF), F32),
        ],
        compiler_params=_params(("arbitrary",)),
        name="ssm",
    )(u_tm, bmat, cmat, lre, lim, d_skip, w_glu, gain)


def _dilated_kernel(x_ref, g_ref, o_ref, m_ref, l_ref, acc_ref):
    seq = x_ref.shape[1]
    q_tiles = DIL_WIDTH // LANES
    heads_per_tile = LANES // HEAD_DIM
    m_ref[...] = jnp.full(m_ref.shape, NEG_INF, F32)
    l_ref[...] = jnp.zeros_like(l_ref)
    acc_ref[...] = jnp.zeros_like(acc_ref)
    row = lax.broadcasted_iota(jnp.int32, (ATT_BLOCK, ATT_BLOCK), 0)
    col = lax.broadcasted_iota(jnp.int32, (ATT_BLOCK, ATT_BLOCK), 1)
    lane = lax.broadcasted_iota(jnp.int32, (ATT_BLOCK, LANES), 1)
    cur_ok = col <= row
    spread = [(row == heads_per_tile * t + col // HEAD_DIM).astype(BF16) for t in range(q_tiles)]
    in_head = [col // HEAD_DIM == k for k in range(heads_per_tile)]
    assert heads_per_tile == 2

    def unit(dil, start, prev_start, prev_live):
        rows = pl.ds(start, ATT_BLOCK, stride=dil)
        load = lambda t, r: x_ref[t, r, :].astype(BF16)
        q = [load(t, rows) for t in range(q_tiles)]
        kc = [load(q_tiles + t, rows) for t in range(q_tiles)]
        vc = [load(2 * q_tiles + t, rows) for t in range(q_tiles)]
        has_prev = prev_start is not None
        if has_prev:
            prows = pl.ds(prev_start, ATT_BLOCK, stride=dil)
            kp = [load(q_tiles + t, prows) for t in range(q_tiles)]
            vp = [load(2 * q_tiles + t, prows) for t in range(q_tiles)]
            prev_ok = jnp.logical_and(col >= row, prev_live)
        tile_of = [h // heads_per_tile for h in range(DIL_HEADS)]
        qm = [jnp.where(in_head[h % heads_per_tile], q[tile_of[h]], jnp.zeros((), BF16))
              for h in range(DIL_HEADS)]
        sc_all = [_dot_nt(qm[h], kc[tile_of[h]]) for h in range(DIL_HEADS)]
        if has_prev:
            sp_all = [_dot_nt(qm[h], kp[tile_of[h]]) for h in range(DIL_HEADS)]
        pc_all, pp_all = [], []
        m_new_p = jnp.zeros((ATT_BLOCK, LANES), F32)
        l_new_p = jnp.zeros((ATT_BLOCK, LANES), F32)
        for h in range(DIL_HEADS):
            sc = jnp.where(cur_ok, sc_all[h], NEG_INF)
            m = jnp.max(sc, axis=-1, keepdims=True)
            if has_prev:
                sp = jnp.where(prev_ok, sp_all[h], NEG_INF)
                m = jnp.maximum(m, jnp.max(sp, axis=-1, keepdims=True))
            pc = jnp.exp(sc - m)
            den = jnp.sum(pc, axis=-1, keepdims=True)
            if has_prev:
                pp = jnp.exp(sp - m)
                den = den + jnp.sum(pp, axis=-1, keepdims=True)
                pp_all.append(pp.astype(BF16))
            pc_all.append(pc.astype(BF16))
            m_new_p = jnp.where(lane == h, m, m_new_p)
            l_new_p = jnp.where(lane == h, den, l_new_p)
        m_old = m_ref[rows, :]
        m_new = jnp.maximum(m_old, m_new_p)
        w_old = jnp.exp(m_old - m_new)
        w_pat = jnp.exp(m_new_p - m_new)
        m_ref[rows, :] = m_new
        l_ref[rows, :] = w_old * l_ref[rows, :] + w_pat * l_new_p
        for t in range(q_tiles):
            parts = []
            for h in range(heads_per_tile * t, heads_per_tile * (t + 1)):
                part = jnp.dot(pc_all[h], vc[t], preferred_element_type=F32)
                if has_prev:
                    part = part + jnp.dot(pp_all[h], vp[t], preferred_element_type=F32)
                parts.append(part)
            new = jnp.where(in_head[0], parts[0], parts[1])
            s_old = jnp.dot(w_old.astype(BF16), spread[t], preferred_element_type=F32)
            s_pat = jnp.dot(w_pat.astype(BF16), spread[t], preferred_element_type=F32)
            acc_ref[t, rows, :] = s_old * acc_ref[t, rows, :] + s_pat * new

    for dil in reversed(DIL_DILATIONS):
        sub_len = seq // dil
        nblk = sub_len // ATT_BLOCK
        units = dil * nblk

        def pair(u, _, dil=dil, nblk=nblk):
            for k in range(2):
                idx = 2 * u + k
                res, blk = idx % dil, idx // dil
                start = res + dil * ATT_BLOCK * blk
                if nblk == 1:
                    unit(dil, start, None, None)
                else:
                    prev = res + dil * ATT_BLOCK * jnp.maximum(blk - 1, 0)
                    unit(dil, start, prev, blk > 0)
            return 0

        lax.fori_loop(0, units // 2, pair, 0)

    def finish(blk, _):
        rows = pl.ds(pl.multiple_of(blk * ATT_BLOCK, ATT_BLOCK), ATT_BLOCK)
        inv = jnp.where(lane < DIL_HEADS, 1.0 / l_ref[rows, :], 0.0)
        inv_hi = inv.astype(BF16)
        inv_lo = (inv - inv_hi.astype(F32)).astype(BF16)
        ys = []
        for t in range(q_tiles):
            scale = (jnp.dot(inv_hi, spread[t], preferred_element_type=F32)
                     + jnp.dot(inv_lo, spread[t], preferred_element_type=F32))
            ys.append(acc_ref[t, rows, :] * scale)
        sq = sum(jnp.sum(y * y, axis=-1, keepdims=True) for y in ys)
        norm = lax.rsqrt(sq * (1.0 / DIL_WIDTH) + NORM_EPS)
        for t in range(q_tiles):
            cols = slice(t * LANES, (t + 1) * LANES)
            o_ref[rows, cols] = (ys[t] * norm * g_ref[:, cols]).astype(BF16)
        return 0

    lax.fori_loop(0, seq // ATT_BLOCK, finish, 0)


def _dilated_attention(qkv_tiles, gain, batch, seq):
    n = batch * seq
    n_tiles = 3 * DIL_WIDTH // LANES
    return pl.pallas_call(
        _dilated_kernel,
        grid=(batch,),
        in_specs=[
            pl.BlockSpec((n_tiles, seq, LANES), lambda b: (0, b, 0)),
            pl.BlockSpec((1, DIL_WIDTH), lambda b: (0, 0)),
        ],
        out_specs=pl.BlockSpec((seq, DIL_WIDTH), lambda b: (b, 0)),
        out_shape=jax.ShapeDtypeStruct((n, DIL_WIDTH), BF16),
        scratch_shapes=[
            pltpu.VMEM((seq, LANES), F32),
            pltpu.VMEM((seq, LANES), F32),
            pltpu.VMEM((DIL_WIDTH // LANES, seq, LANES), F32),
        ],
        compiler_params=_params(("arbitrary",)),
        name="dilated_attention",
    )(qkv_tiles, gain)


def _diff_kernel(out_scale, lam_ref, q_ref, k_ref, v_ref, g_ref, o_ref, qt_ref, vt_ref, ot_ref,
                 qm_ref, m_ref, l_ref, acc_ref):
    seq = q_ref.shape[0]
    nblk = seq // DIFF_BLOCK
    lam = lam_ref[0]
    for i in range(nblk):
        rows = slice(i * DIFF_BLOCK, (i + 1) * DIFF_BLOCK)
        qt_ref[i] = q_ref[rows, :].astype(F32).T.astype(BF16)
        vt_ref[i] = v_ref[rows, :].astype(F32).T.astype(BF16)
    krow = lax.broadcasted_iota(jnp.int32, (DIFF_BLOCK, DIFF_BLOCK), 0)
    qcol = lax.broadcasted_iota(jnp.int32, (DIFF_BLOCK, DIFF_BLOCK), 1)
    causal = krow <= qcol
    chan = lax.broadcasted_iota(jnp.int32, (DIFF_WIDTH, DIFF_BLOCK), 0)

    n_maps = 2 * DIFF_HEADS

    def attend(kj, mask):
        k0 = pl.multiple_of(kj * DIFF_BLOCK, DIFF_BLOCK)
        k = k_ref[pl.ds(k0, DIFF_BLOCK), :]
        scores = [jnp.dot(k, qm_ref[c], preferred_element_type=F32) for c in range(n_maps)]
        probs, alphas = [], []
        for c in range(n_maps):
            s = scores[c]
            if mask:
                s = jnp.where(causal, s, NEG_INF)
            m_old = m_ref[c]
            m_new = jnp.maximum(m_old, jnp.max(s, axis=0, keepdims=True))
            alpha = jnp.exp(m_old - m_new)
            p = jnp.exp(s - m_new)
            m_ref[c] = m_new
            l_ref[c] = alpha * l_ref[c] + jnp.sum(p, axis=0, keepdims=True)
            probs.append(p.astype(BF16))
            alphas.append(alpha)
        for c in range(n_maps):
            h = c // 2
            vt = vt_ref[kj, h * HEAD_DIM:(h + 1) * HEAD_DIM, :]
            acc_ref[c] = alphas[c] * acc_ref[c] + jnp.dot(vt, probs[c],
                                                          preferred_element_type=F32)

    def q_block(qi, _):
        qt = qt_ref[qi]
        for c in range(n_maps):
            lo = c * DIFF_QK_DIM
            qm_ref[c] = jnp.where(jnp.logical_and(chan >= lo, chan < lo + DIFF_QK_DIM), qt,
                                  jnp.zeros((), BF16))
        m_ref[...] = jnp.full(m_ref.shape, NEG_INF, F32)
        l_ref[...] = jnp.zeros_like(l_ref)
        acc_ref[...] = jnp.zeros_like(acc_ref)

        def k_block(kj, _):
            attend(kj, False)
            return 0

        lax.fori_loop(0, qi, k_block, 0)
        attend(qi, True)
        for h in range(DIFF_HEADS):
            o = (acc_ref[2 * h] / l_ref[2 * h]
                 - lam * (acc_ref[2 * h + 1] / l_ref[2 * h + 1]))
            o = o * lax.rsqrt(jnp.mean(o * o, axis=0, keepdims=True) + NORM_EPS)
            ot_ref[qi, h * HEAD_DIM:(h + 1) * HEAD_DIM, :] = o * (g_ref[...] * out_scale)
        return 0

    lax.fori_loop(0, nblk, q_block, 0)

    for i in range(nblk):
        o_ref[i * DIFF_BLOCK:(i + 1) * DIFF_BLOCK, :] = ot_ref[i].T.astype(BF16)


def _diff_attention(qkv, lam, subln_g_col, out_scale, batch, seq):
    n = batch * seq
    blk = (seq, DIFF_WIDTH)
    nblk = seq // DIFF_BLOCK
    return pl.pallas_call(
        functools.partial(_diff_kernel, out_scale),
        grid=(batch,),
        in_specs=[
            pl.BlockSpec(memory_space=pltpu.SMEM),
            pl.BlockSpec(blk, lambda b: (b, 0)),
            pl.BlockSpec(blk, lambda b: (b, 1)),
            pl.BlockSpec(blk, lambda b: (b, 2)),
            pl.BlockSpec((HEAD_DIM, 1), lambda b: (0, 0)),
        ],
        out_specs=pl.BlockSpec(blk, lambda b: (b, 0)),
        out_shape=jax.ShapeDtypeStruct((n, DIFF_WIDTH), BF16),
        scratch_shapes=[
            pltpu.VMEM((nblk, DIFF_WIDTH, DIFF_BLOCK), BF16),
            pltpu.VMEM((nblk, DIFF_WIDTH, DIFF_BLOCK), BF16),
            pltpu.VMEM((nblk, DIFF_WIDTH, DIFF_BLOCK), F32),
            pltpu.VMEM((2 * DIFF_HEADS, DIFF_WIDTH, DIFF_BLOCK), BF16),
            pltpu.VMEM((2 * DIFF_HEADS, 1, DIFF_BLOCK), F32),
            pltpu.VMEM((2 * DIFF_HEADS, 1, DIFF_BLOCK), F32),
            pltpu.VMEM((2 * DIFF_HEADS, HEAD_DIM, DIFF_BLOCK), F32),
        ],
        compiler_params=_params(("arbitrary",)),
        name="diff_attention",
    )(lam, qkv, qkv, qkv, subln_g_col)


def _out_proj_kernel(x_ref, ssm_ref, dil_ref, diff_ref, w_ref, y_ref):
    acc = jnp.dot(ssm_ref[...], w_ref[:SSM_WIDTH, :], preferred_element_type=F32)
    acc = acc + jnp.dot(dil_ref[...], w_ref[SSM_WIDTH:SSM_WIDTH + DIL_WIDTH, :],
                        preferred_element_type=F32)
    acc = acc + jnp.dot(diff_ref[...], w_ref[SSM_WIDTH + DIL_WIDTH:, :],
                        preferred_element_type=F32)
    y_ref[...] = x_ref[...] + acc


def _out_proj(x2, y_ssm_tm, y_dil, y_diff, w_out, batch, seq):
    n = batch * seq
    blocks_per_seq = seq // PROJ_TOKENS
    tok = lambda w: pl.BlockSpec((PROJ_TOKENS, w), lambda i: (i, 0))
    return pl.pallas_call(
        _out_proj_kernel,
        grid=(n // PROJ_TOKENS,),
        in_specs=[
            tok(D_MODEL),
            pl.BlockSpec((PROJ_TOKENS, SSM_WIDTH),
                         lambda i: (i % blocks_per_seq, i // blocks_per_seq)),
            tok(DIL_WIDTH),
            tok(DIFF_WIDTH),
            pl.BlockSpec((D_MODEL, D_MODEL), lambda i: (0, 0)),
        ],
        out_specs=tok(D_MODEL),
        out_shape=jax.ShapeDtypeStruct((n, D_MODEL), F32),
        compiler_params=_params(("arbitrary",)),
        name="out_proj",
    )(x2, y_ssm_tm, y_dil, y_diff, w_out)


def _sorting_network(n):
    pairs = []
    p = 1
    while p < n:
        k = p
        while k >= 1:
            for j in range(k % p, n - k, 2 * k):
                for i in range(min(k, n - j - k)):
                    if (i + j) // (2 * p) == (i + j + k) // (2 * p):
                        pairs.append((i + j, i + j + k))
            k //= 2
        p *= 2
    return pairs


_SORT16 = _sorting_network(PEER_TOPK)
_BITONIC16 = [(i, i + s) for s in (8, 4, 2, 1) for i in range(PEER_TOPK) if (i // s) % 2 == 0]


def _top16_desc(vals):
    v = list(vals)
    for a, b in _SORT16:
        hi, lo = jnp.maximum(v[a], v[b]), jnp.minimum(v[a], v[b])
        v[a], v[b] = hi, lo
    for shift in (4, 2, 1):
        other = [pltpu.roll(x, shift, 0) for x in v]
        v = [jnp.maximum(v[k], other[PEER_TOPK - 1 - k]) for k in range(PEER_TOPK)]
        for a, b in _BITONIC16:
            hi, lo = jnp.maximum(v[a], v[b]), jnp.minimum(v[a], v[b])
            v[a], v[b] = hi, lo
    return v


def _peer_gates(sa, sb):
    t = sa.shape[1]
    a_top = _top16_desc([sa[SUBLANES * v:SUBLANES * (v + 1), :] for v in range(PEER_KEYS // SUBLANES)])
    b_top = _top16_desc([sb[SUBLANES * v:SUBLANES * (v + 1), :] for v in range(PEER_KEYS // SUBLANES)])
    sub = lax.broadcasted_iota(jnp.int32, (SUBLANES, t), 0)

    def pack(rows):
        out = rows[0]
        for s in range(1, SUBLANES):
            out = jnp.where(sub == s, rows[s], out)
        return out

    b_lo, b_hi, a_hi = pack(b_top[:8]), pack(b_top[8:]), pack(a_top[8:])
    cands = [a_top[0] + b_lo, a_top[0] + b_hi]
    cands += [a_top[k] + b_lo for k in range(1, 8)]
    cands += [a_hi + b_top[0]]
    valid = [None, None] + [sub < (PEER_TOPK // (k + 1)) for k in range(1, 8)] + [None]
    cands = [c if ok is None else jnp.where(ok, c, NEG_INF) for c, ok in zip(cands, valid)]
    pad = jnp.full((SUBLANES, t), NEG_INF, F32)
    tau = _top16_desc(cands + [pad] * (PEER_TOPK - len(cands)))[PEER_TOPK - 1]
    top = a_top[0] + b_top[0]
    z = jnp.zeros((SUBLANES, t), F32)
    for c in cands:
        z = z + jnp.where(c >= tau, jnp.exp(c - top), 0.0)
    for shift in (4, 2, 1):
        z = z + pltpu.roll(z, shift, 0)
    inv_z = 1.0 / z
    last = PEER_TOPK - 1

    def search(test):
        total = None
        bits = []
        for level, weight in enumerate((8, 4, 2, 1)):
            leaves = [b_top[m] for m in range(weight - 1, last, 2 * weight)]
            for bit in reversed(bits):
                leaves = [jnp.where(bit, leaves[2 * n + 1], leaves[2 * n])
                          for n in range(len(leaves) // 2)]
            bit = test(leaves[0])
            bits.append(bit)
            term = jnp.where(bit, float(weight), 0.0)
            total = term if total is None else total + term
        return total

    wa, count, wb, rank = [], [], [], []
    for v in range(PEER_KEYS // SUBLANES):
        xa = sa[SUBLANES * v:SUBLANES * (v + 1), :]
        xb = sb[SUBLANES * v:SUBLANES * (v + 1), :]
        wa.append(jnp.where(xa >= a_top[last], jnp.exp(xa - a_top[0]), 0.0) * inv_z)
        wb.append(jnp.where(xb >= b_top[last], jnp.exp(xb - b_top[0]), 0.0))
        count.append(search(lambda b: xa + b >= tau)
                     + jnp.where(xa + b_top[last] >= tau, 1.0, 0.0))
        rank.append(search(lambda b: b >= xb) + jnp.where(b_top[last] >= xb, 1.0, 0.0))
    cat = lambda parts: jnp.concatenate(parts, axis=0)
    return cat(wa), cat(count), cat(wb), cat(rank)


def _gelu_tanh(a):
    c = math.sqrt(2.0 / math.pi)
    inner = a * (a * a * (c * 0.044715) + c)
    return (a * 0.5) * (jnp.tanh(inner) + 1.0)


def _peer_kernel(final, x_ref, g_ref, wq_ref, keys_ref, u_ref, vt_ref, fg_ref, y_ref,
                 h_ref, sc_ref, wa_ref, cnt_ref, wb_ref, rank_ref, a0_ref, a1_ref, p_ref, acc_ref):
    s = pl.program_id(1)
    n_blocks = PEER_EXPERTS // PEER_EXPERT_BLOCK
    tb = x_ref.shape[0]
    n_chunks = tb // PEER_GATE_CHUNK
    rows_per_step = PEER_EXPERT_BLOCK // PEER_KEYS

    @pl.when(s == 0)
    def _():
        h_ref[...] = _rms(x_ref[...], g_ref[...]).astype(BF16)
        acc_ref[...] = jnp.zeros_like(acc_ref)
        a1_ref[...] = jnp.zeros_like(a1_ref)

        chunks_per_stage = PEER_SCORE_TOKENS // PEER_GATE_CHUNK

        def stage(si, _):
            r0 = pl.multiple_of(si * PEER_SCORE_TOKENS, PEER_SCORE_TOKENS)
            hs = h_ref[pl.ds(r0, PEER_SCORE_TOKENS), :]
            qs = [jnp.dot(hs, wq_ref[hd], preferred_element_type=F32).astype(BF16)
                  for hd in range(PEER_HEADS)]
            for hd in range(PEER_HEADS):
                for half in range(2):
                    sc = _dot_nt(keys_ref[2 * hd + half],
                                 qs[hd][:, half * PEER_KEYS:(half + 1) * PEER_KEYS])
                    for c in range(chunks_per_stage):
                        sc_ref[c, 2 * hd + half] = sc[:, c * PEER_GATE_CHUNK:
                                                      (c + 1) * PEER_GATE_CHUNK]

            def unit(k, _):
                c = k // (PEER_HEADS // 2)
                ci = si * chunks_per_stage + c
                for hd_local in range(2):
                    hd = (k % (PEER_HEADS // 2)) * 2 + hd_local
                    wa, count, wb, rank = _peer_gates(sc_ref[c, 2 * hd], sc_ref[c, 2 * hd + 1])
                    wa_ref[ci, hd] = wa
                    cnt_ref[ci, hd] = count
                    wb_ref[ci, hd] = wb.astype(BF16)
                    rank_ref[ci, hd] = rank.astype(BF16)
                return 0

            lax.fori_loop(0, chunks_per_stage * PEER_HEADS // 2, unit, 0)
            return 0

        lax.fori_loop(0, tb // PEER_SCORE_TOKENS, stage, 0)

    packed_rows = 2 * SUBLANES
    tiles = PEER_KEYS // packed_rows

    def step(par):
        a_new, a_old = (a0_ref, a1_ref) if par == 0 else (a1_ref, a0_ref)
        gate_block = jnp.clip(s - 1, 0, n_blocks - 1)
        for t0 in range(0, tb, PEER_TOKEN_TILE):
            tok = slice(t0, t0 + PEER_TOKEN_TILE)
            a_new[:, tok] = _dot_nt(u_ref[...], h_ref[tok, :]).astype(BF16)
            for ci in range(t0 // PEER_GATE_CHUNK, (t0 + PEER_TOKEN_TILE) // PEER_GATE_CHUNK):
                cols = slice(ci * PEER_GATE_CHUNK, (ci + 1) * PEER_GATE_CHUNK)
                for il in range(rows_per_step):
                    i = gate_block * rows_per_step + il
                    rows = slice(il * PEER_KEYS, (il + 1) * PEER_KEYS)
                    gate = jnp.zeros((tiles, packed_rows, PEER_GATE_CHUNK), BF16)
                    for hd in range(PEER_HEADS):
                        row = lambda ref: jnp.broadcast_to(
                            ref[ci, hd, pl.ds(i, 1), :],
                            (packed_rows, PEER_GATE_CHUNK)).astype(BF16)[None]
                        rank = rank_ref[ci, hd].reshape(tiles, packed_rows, PEER_GATE_CHUNK)
                        wb = wb_ref[ci, hd].reshape(tiles, packed_rows, PEER_GATE_CHUNK)
                        gate = gate + jnp.where(rank <= row(cnt_ref), wb * row(wa_ref),
                                                jnp.zeros((), BF16))
                    gate = gate.reshape(PEER_KEYS, PEER_GATE_CHUNK)
                    p_ref[rows, cols] = gate * _gelu_tanh(a_old[rows, cols])
            acc_ref[:, tok] += jnp.dot(vt_ref[...], p_ref[:, tok], preferred_element_type=F32)

    @pl.when(s % 2 == 0)
    def _():
        step(0)

    @pl.when(s % 2 == 1)
    def _():
        step(1)

    @pl.when(s == pl.num_programs(1) - 1)
    def _():
        y = x_ref[...] + acc_ref[...].T
        if final:
            y = _rms(y, fg_ref[...])
        y_ref[...] = y


def _peer(x2, gain, wq_heads, keys, u_bf, vt_bf, final_gain, final):
    n = x2.shape[0]
    n_chunks = PEER_TOKENS // PEER_GATE_CHUNK
    n_blocks = PEER_EXPERTS // PEER_EXPERT_BLOCK
    gate_shape = (n_chunks, PEER_HEADS, PEER_KEYS, PEER_GATE_CHUNK)
    act_block = lambda s: jnp.minimum(s, n_blocks - 1)
    out_block = lambda s: jnp.maximum(s - 1, 0)
    once = pl.Buffered(1)
    return pl.pallas_call(
        functools.partial(_peer_kernel, final),
        grid=(n // PEER_TOKENS, n_blocks + 1),
        in_specs=[
            pl.BlockSpec((PEER_TOKENS, D_MODEL), lambda t, s: (t, 0), pipeline_mode=once),
            pl.BlockSpec((1, D_MODEL), lambda t, s: (0, 0)),
            pl.BlockSpec((PEER_HEADS, D_MODEL, 2 * PEER_KEYS), lambda t, s: (0, 0, 0),
                         pipeline_mode=once),
            pl.BlockSpec((2 * PEER_HEADS, PEER_KEYS, PEER_KEYS), lambda t, s: (0, 0, 0)),
            pl.BlockSpec((PEER_EXPERT_BLOCK, D_MODEL), lambda t, s: (act_block(s), 0)),
            pl.BlockSpec((None, D_MODEL, PEER_EXPERT_BLOCK), lambda t, s: (out_block(s), 0, 0)),
            pl.BlockSpec((1, D_MODEL), lambda t, s: (0, 0)),
        ],
        out_specs=pl.BlockSpec((PEER_TOKENS, D_MODEL), lambda t, s: (t, 0)),
        out_shape=jax.ShapeDtypeStruct((n, D_MODEL), F32),
        scratch_shapes=[
            pltpu.VMEM((PEER_TOKENS, D_MODEL), BF16),
            pltpu.VMEM((PEER_SCORE_TOKENS // PEER_GATE_CHUNK, 2 * PEER_HEADS, PEER_KEYS,
                        PEER_GATE_CHUNK), F32),
            pltpu.VMEM(gate_shape, F32),
            pltpu.VMEM(gate_shape, F32),
            pltpu.VMEM(gate_shape, BF16),
            pltpu.VMEM(gate_shape, BF16),
            pltpu.VMEM((PEER_EXPERT_BLOCK, PEER_TOKENS), BF16),
            pltpu.VMEM((PEER_EXPERT_BLOCK, PEER_TOKENS), BF16),
            pltpu.VMEM((PEER_EXPERT_BLOCK, PEER_TOKENS), BF16),
            pltpu.VMEM((D_MODEL, PEER_TOKENS), F32),
        ],
        compiler_params=_params(("arbitrary", "arbitrary")),
        name="peer",
    )(x2, gain, wq_heads, keys, u_bf, vt_bf, final_gain)


def _ssm_matrices(lam_re, lam_im, log_step, b_re, b_im, c_re, c_im):
    lam = lax.complex(lam_re, lam_im)
    step = jnp.exp(log_step)[:, None]
    lam_bar = jnp.exp(lam * step)
    b_bar = ((lam_bar - 1.0) / lam)[:, :, None] * lax.complex(b_re, b_im)
    eye = jnp.eye(SSM_GROUPS, dtype=F32)

    def embed_in(m):
        return jnp.einsum("gpc,gh->gchp", m, eye).reshape(SSM_WIDTH, SSM_HALF)

    def embed_out(m):
        return jnp.einsum("gcp,gh->gphc", m, eye).reshape(SSM_HALF, SSM_WIDTH)

    bmat = jnp.concatenate([embed_in(jnp.real(b_bar)), embed_in(jnp.imag(b_bar))], axis=1)
    cmat = jnp.concatenate([embed_out(c_re), embed_out(-c_im)], axis=0)
    lre = jnp.broadcast_to(jnp.real(lam_bar).reshape(1, SSM_HALF), (SUBLANES, SSM_HALF))
    lim = jnp.broadcast_to(jnp.imag(lam_bar).reshape(1, SSM_HALF), (SUBLANES, SSM_HALF))
    return bmat.astype(BF16), cmat.astype(BF16), lre, lim


def _expert_out_blocks(v_exp):
    blocks = v_exp.reshape(PEER_EXPERTS // PEER_EXPERT_BLOCK, PEER_EXPERT_BLOCK, D_MODEL)
    return blocks.transpose(0, 2, 1).astype(BF16)


def _in_proj_col_scale():
    s = jnp.ones((IN_WIDTH,), F32)
    s = s.at[SSM_WIDTH:SSM_WIDTH + DIL_WIDTH].set(HEAD_DIM ** -0.5)
    d0 = SSM_WIDTH + 3 * DIL_WIDTH
    s = s.at[d0:d0 + DIFF_WIDTH].set(DIFF_QK_DIM ** -0.5)
    return s.reshape(1, IN_WIDTH)


def kernel(x, norm1_g, w_in, ssm_lam_re, ssm_lam_im, ssm_log_step, ssm_b_re, ssm_b_im, ssm_c_re, ssm_c_im, ssm_d, ssm_w_glu, ssm_norm_g, dil_norm_g, diff_lam_q1, diff_lam_k1, diff_lam_q2, diff_lam_k2, diff_subln_g, w_out, norm2_g, peer_w_query, peer_sub_keys, peer_u, peer_v, final_norm_g):
    batch, seq, _ = x.shape
    n = batch * seq
    depth = w_in.shape[0]
    x2 = x.reshape(n, D_MODEL)
    col_scale = _in_proj_col_scale()
    row = lambda v: v.reshape(1, -1)
    for layer in range(depth):
        lambda_init = 0.8 - 0.6 * math.exp(-0.3 * layer)
        u_tm, dil_qkv, diff_qkv = _in_proj(x2, row(norm1_g[layer]), w_in[layer].astype(BF16),
                                           col_scale, batch, seq)
        bmat, cmat, lre, lim = _ssm_matrices(
            ssm_lam_re[layer], ssm_lam_im[layer], ssm_log_step[layer], ssm_b_re[layer],
            ssm_b_im[layer], ssm_c_re[layer], ssm_c_im[layer])
        y_ssm = _ssm(u_tm.reshape(seq * batch, SSM_WIDTH), bmat, cmat, lre, lim,
                     row(ssm_d[layer]), ssm_w_glu[layer].astype(BF16), row(ssm_norm_g[layer]),
                     batch, seq)
        y_dil = _dilated_attention(dil_qkv, row(dil_norm_g[layer]), batch, seq)
        lam = (jnp.exp(jnp.sum(diff_lam_q1[layer] * diff_lam_k1[layer]))
               - jnp.exp(jnp.sum(diff_lam_q2[layer] * diff_lam_k2[layer])) + lambda_init)
        y_diff = _diff_attention(diff_qkv, lam.reshape(1), diff_subln_g[layer].reshape(-1, 1),
                                 1.0 - lambda_init, batch, seq)
        x2 = _out_proj(x2, y_ssm.reshape(seq, batch * SSM_WIDTH), y_dil, y_diff,
                       w_out[layer].astype(BF16), batch, seq)
        wq_heads = peer_w_query[layer].reshape(D_MODEL, PEER_HEADS, 2 * PEER_KEYS)
        wq_heads = wq_heads.transpose(1, 0, 2).astype(BF16)
        keys = peer_sub_keys[layer].reshape(2 * PEER_HEADS, PEER_KEYS, PEER_KEYS).astype(BF16)
        x2 = _peer(x2, row(norm2_g[layer]), wq_heads, keys, peer_u[layer].astype(BF16),
                   _expert_out_blocks(peer_v[layer]), row(final_norm_g), layer == depth - 1)
    return x2.reshape(batch, seq, D_MODEL)
```

```python
import functools
import math

import jax
import jax.numpy as jnp
from jax import lax
from jax.experimental import pallas as pl
from jax.experimental.pallas import tpu as pltpu

F32 = jnp.float32
BF16 = jnp.bfloat16

D_MODEL = 1024
HEAD_DIM = 64
SSM_WIDTH = 384
SSM_GROUP_CH = 16
SSM_GROUPS = 24
SSM_STATE = 64
SSM_HALF = SSM_GROUPS * SSM_STATE
DIL_WIDTH = 384
DIL_HEADS = 6
DIL_DILATIONS = (1, 4, 16)
DIFF_WIDTH = 256
DIFF_HEADS = 4
DIFF_QK_DIM = 32
IN_WIDTH = 2304
PEER_HEADS = 8
PEER_KEYS = 128
PEER_TOPK = 16
PEER_EXPERTS = PEER_KEYS * PEER_KEYS
NORM_EPS = 1e-6

LANES = 128
SUBLANES = 8
VMEM_LIMIT_BYTES = 56 * 1024 * 1024

PROJ_TOKENS = 512
SSM_CHUNK = 128
ATT_BLOCK = 128
DIFF_BLOCK = 256
PEER_TOKENS = 1024
PEER_GATE_CHUNK = LANES
PEER_EXPERT_BLOCK = 512
PEER_TOKEN_TILE = 256
PEER_SCORE_TOKENS = 256

NEG_INF = float("-inf")


def _rms(x, gain):
    return x * lax.rsqrt(jnp.mean(x * x, axis=-1, keepdims=True) + NORM_EPS) * gain


def _dot_nt(a, b):
    return lax.dot_general(a, b, (((1,), (1,)), ((), ())), preferred_element_type=F32)


def _params(semantics):
    return pltpu.CompilerParams(dimension_semantics=semantics, vmem_limit_bytes=VMEM_LIMIT_BYTES)


def _in_proj_kernel(x_ref, g_ref, w_ref, scale_ref, u_ref, dil_ref, diff_ref):
    h = _rms(x_ref[...], g_ref[...])
    p = jnp.dot(h.astype(BF16), w_ref[...], preferred_element_type=F32) * scale_ref[...]
    u_ref[...] = p[:, :SSM_WIDTH].astype(BF16)
    for t in range(dil_ref.shape[0]):
        dil_ref[t] = p[:, SSM_WIDTH + t * LANES:SSM_WIDTH + (t + 1) * LANES]
    diff_ref[...] = p[:, SSM_WIDTH + 3 * DIL_WIDTH:].astype(BF16)


def _in_proj(x2, gain, w_in, col_scale, batch, seq):
    n = batch * seq
    blocks_per_seq = seq // PROJ_TOKENS
    return pl.pallas_call(
        _in_proj_kernel,
        grid=(n // PROJ_TOKENS,),
        in_specs=[
            pl.BlockSpec((PROJ_TOKENS, D_MODEL), lambda i: (i, 0)),
            pl.BlockSpec((1, D_MODEL), lambda i: (0, 0)),
            pl.BlockSpec((D_MODEL, IN_WIDTH), lambda i: (0, 0)),
            pl.BlockSpec((1, IN_WIDTH), lambda i: (0, 0)),
        ],
        out_specs=[
            pl.BlockSpec((PROJ_TOKENS, SSM_WIDTH),
                         lambda i: (i % blocks_per_seq, i // blocks_per_seq)),
            pl.BlockSpec((3 * DIL_WIDTH // LANES, PROJ_TOKENS, LANES), lambda i: (0, i, 0)),
            pl.BlockSpec((PROJ_TOKENS, 3 * DIFF_WIDTH), lambda i: (i, 0)),
        ],
        out_shape=[
            jax.ShapeDtypeStruct((seq, batch * SSM_WIDTH), BF16),
            jax.ShapeDtypeStruct((3 * DIL_WIDTH // LANES, n, LANES), F32),
            jax.ShapeDtypeStruct((n, 3 * DIFF_WIDTH), BF16),
        ],
        compiler_params=_params(("arbitrary",)),
        name="in_proj",
    )(x2, gain, w_in, col_scale)


def _ssm_kernel(u_ref, bmat_ref, cmat_ref, lre_ref, lim_ref, d_ref, wglu_ref, g_ref,
                y_ref, s_ref, state_ref):
    rows = u_ref.shape[0]
    sub = 256

    @pl.when(pl.program_id(0) == 0)
    def _():
        state_ref[...] = jnp.zeros_like(state_ref)

    span = (LANES // SSM_GROUP_CH) * SSM_STATE
    blocks = [(slice(k * LANES, (k + 1) * LANES), slice(off + k * span, off + (k + 1) * span))
              for k in range(SSM_WIDTH // LANES) for off in (0, SSM_HALF)]
    for c in range(rows // sub):
        sl = slice(c * sub, (c + 1) * sub)
        for ch, st in blocks:
            s_ref[sl, st] = jnp.dot(u_ref[sl, ch], bmat_ref[ch, st], preferred_element_type=F32)

    lre = lre_ref[...]
    lim = lim_ref[...]

    def step(t, carry):
        sre, sim = carry
        r0 = pl.multiple_of(t * SUBLANES, SUBLANES)
        bre = s_ref[pl.ds(r0, SUBLANES), :SSM_HALF]
        bim = s_ref[pl.ds(r0, SUBLANES), SSM_HALF:]
        nre = lre * sre - lim * sim + bre
        nim = lre * sim + lim * sre + bim
        s_ref[pl.ds(r0, SUBLANES), :SSM_HALF] = nre
        s_ref[pl.ds(r0, SUBLANES), SSM_HALF:] = nim
        return nre, nim

    sre, sim = lax.fori_loop(0, rows // SUBLANES, step,
                             (state_ref[:, :SSM_HALF], state_ref[:, SSM_HALF:]), unroll=2)
    state_ref[:, :SSM_HALF] = sre
    state_ref[:, SSM_HALF:] = sim

    for c in range(rows // sub):
        sl = slice(c * sub, (c + 1) * sub)
        tiles = []
        for k in range(SSM_WIDTH // LANES):
            (ch, st_re), (_, st_im) = blocks[2 * k], blocks[2 * k + 1]
            tiles.append(
                jnp.dot(s_ref[sl, st_re].astype(BF16), cmat_ref[st_re, ch],
                        preferred_element_type=F32)
                + jnp.dot(s_ref[sl, st_im].astype(BF16), cmat_ref[st_im, ch],
                          preferred_element_type=F32))
        y = jnp.concatenate(tiles, axis=1)
        y = y + d_ref[...] * u_ref[sl, :].astype(F32)
        g = jax.nn.gelu(y)
        z = jnp.dot(g.astype(BF16), wglu_ref[...], preferred_element_type=F32)
        y = g * jax.nn.sigmoid(z)
        y_ref[sl, :] = _rms(y, g_ref[...]).astype(BF16)


def _ssm(u_tm, bmat, cmat, lre, lim, d_skip, w_glu, gain, batch, seq):
    assert batch == SUBLANES, "the S5 scan keeps one batch per sublane"
    rows = SSM_CHUNK * batch
    const = lambda i: (0, 0)
    return pl.pallas_call(
        _ssm_kernel,
        grid=(seq // SSM_CHUNK,),
        in_specs=[
            pl.BlockSpec((rows, SSM_WIDTH), lambda i: (i, 0)),
            pl.BlockSpec((SSM_WIDTH, 2 * SSM_HALF), const),
            pl.BlockSpec((2 * SSM_HALF, SSM_WIDTH), const),
            pl.BlockSpec((SUBLANES, SSM_HALF), const),
            pl.BlockSpec((SUBLANES, SSM_HALF), const),
            pl.BlockSpec((1, SSM_WIDTH), const),
            pl.BlockSpec((SSM_WIDTH, SSM_WIDTH), const),
            pl.BlockSpec((1, SSM_WIDTH), const),
        ],
        out_specs=pl.BlockSpec((rows, SSM_WIDTH), lambda i: (i, 0)),
        out_shape=jax.ShapeDtypeStruct((seq * batch, SSM_WIDTH), BF16),
        scratch_shapes=[
            pltpu.VMEM((rows, 2 * SSM_HALF), F32),
            pltpu.VMEM((SUBLANES, 2 * SSM_HALF), F32),
        ],
        compiler_params=_params(("arbitrary",)),
        name="ssm",
    )(u_tm, bmat, cmat, lre, lim, d_skip, w_glu, gain)


def _dilated_kernel(x_ref, g_ref, o_ref, m_ref, l_ref, acc_ref):
    seq = x_ref.shape[1]
    q_tiles = DIL_WIDTH // LANES
    heads_per_tile = LANES // HEAD_DIM
    m_ref[...] = jnp.full(m_ref.shape, NEG_INF, F32)
    l_ref[...] = jnp.zeros_like(l_ref)
    acc_ref[...] = jnp.zeros_like(acc_ref)
    row = lax.broadcasted_iota(jnp.int32, (ATT_BLOCK, ATT_BLOCK), 0)
    col = lax.broadcasted_iota(jnp.int32, (ATT_BLOCK, ATT_BLOCK), 1)
    lane = lax.broadcasted_iota(jnp.int32, (ATT_BLOCK, LANES), 1)
    cur_ok = col <= row
    spread = [(row == heads_per_tile * t + col // HEAD_DIM).astype(BF16) for t in range(q_tiles)]
    in_head = [col // HEAD_DIM == k for k in range(heads_per_tile)]
    assert heads_per_tile == 2

    def unit(dil, start, prev_start, prev_live):
        rows = pl.ds(start, ATT_BLOCK, stride=dil)
        load = lambda t, r: x_ref[t, r, :].astype(BF16)
        q = [load(t, rows) for t in range(q_tiles)]
        kc = [load(q_tiles + t, rows) for t in range(q_tiles)]
        vc = [load(2 * q_tiles + t, rows) for t in range(q_tiles)]
        has_prev = prev_start is not None
        if has_prev:
            prows = pl.ds(prev_start, ATT_BLOCK, stride=dil)
            kp = [load(q_tiles + t, prows) for t in range(q_tiles)]
            vp = [load(2 * q_tiles + t, prows) for t in range(q_tiles)]
            prev_ok = jnp.logical_and(col >= row, prev_live)
        tile_of = [h // heads_per_tile for h in range(DIL_HEADS)]
        qm = [jnp.where(in_head[h % heads_per_tile], q[tile_of[h]], jnp.zeros((), BF16))
              for h in range(DIL_HEADS)]
        sc_all = [_dot_nt(qm[h], kc[tile_of[h]]) for h in range(DIL_HEADS)]
        if has_prev:
            sp_all = [_dot_nt(qm[h], kp[tile_of[h]]) for h in range(DIL_HEADS)]
        pc_all, pp_all = [], []
        m_new_p = jnp.zeros((ATT_BLOCK, LANES), F32)
        l_new_p = jnp.zeros((ATT_BLOCK, LANES), F32)
        for h in range(DIL_HEADS):
            sc = jnp.where(cur_ok, sc_all[h], NEG_INF)
            m = jnp.max(sc, axis=-1, keepdims=True)
            if has_prev:
                sp = jnp.where(prev_ok, sp_all[h], NEG_INF)
                m = jnp.maximum(m, jnp.max(sp, axis=-1, keepdims=True))
            pc = jnp.exp(sc - m)
            den = jnp.sum(pc, axis=-1, keepdims=True)
            if has_prev:
                pp = jnp.exp(sp - m)
                den = den + jnp.sum(pp, axis=-1, keepdims=True)
                pp_all.append(pp.astype(BF16))
            pc_all.append(pc.astype(BF16))
            m_new_p = jnp.where(lane == h, m, m_new_p)
            l_new_p = jnp.where(lane == h, den, l_new_p)
        m_old = m_ref[rows, :]
        m_new = jnp.maximum(m_old, m_new_p)
        w_old = jnp.exp(m_old - m_new)
        w_pat = jnp.exp(m_new_p - m_new)
        m_ref[rows, :] = m_new
        l_ref[rows, :] = w_old * l_ref[rows, :] + w_pat * l_new_p
        for t in range(q_tiles):
            parts = []
            for h in range(heads_per_tile * t, heads_per_tile * (t + 1)):
                part = jnp.dot(pc_all[h], vc[t], preferred_element_type=F32)
                if has_prev:
                    part = part + jnp.dot(pp_all[h], vp[t], preferred_element_type=F32)
                parts.append(part)
            new = jnp.where(in_head[0], parts[0], parts[1])
            s_old = jnp.dot(w_old.astype(BF16), spread[t], preferred_element_type=F32)
            s_pat = jnp.dot(w_pat.astype(BF16), spread[t], preferred_element_type=F32)
            acc_ref[t, rows, :] = s_old * acc_ref[t, rows, :] + s_pat * new

    for dil in reversed(DIL_DILATIONS):
        sub_len = seq // dil
        nblk = sub_len // ATT_BLOCK
        units = dil * nblk

        def pair(u, _, dil=dil, nblk=nblk):
            for k in range(2):
                idx = 2 * u + k
                res, blk = idx % dil, idx // dil
                start = res + dil * ATT_BLOCK * blk
                if nblk == 1:
                    unit(dil, start, None, None)
                else:
                    prev = res + dil * ATT_BLOCK * jnp.maximum(blk - 1, 0)
                    unit(dil, start, prev, blk > 0)
            return 0

        lax.fori_loop(0, units // 2, pair, 0)

    def finish(blk, _):
        rows = pl.ds(pl.multiple_of(blk * ATT_BLOCK, ATT_BLOCK), ATT_BLOCK)
        inv = jnp.where(lane < DIL_HEADS, 1.0 / l_ref[rows, :], 0.0)
        inv_hi = inv.astype(BF16)
        inv_lo = (inv - inv_hi.astype(F32)).astype(BF16)
        ys = []
        for t in range(q_tiles):
            scale = (jnp.dot(inv_hi, spread[t], preferred_element_type=F32)
                     + jnp.dot(inv_lo, spread[t], preferred_element_type=F32))
            ys.append(acc_ref[t, rows, :] * scale)
        sq = sum(jnp.sum(y * y, axis=-1, keepdims=True) for y in ys)
        norm = lax.rsqrt(sq * (1.0 / DIL_WIDTH) + NORM_EPS)
        for t in range(q_tiles):
            cols = slice(t * LANES, (t + 1) * LANES)
            o_ref[rows, cols] = (ys[t] * norm * g_ref[:, cols]).astype(BF16)
        return 0

    lax.fori_loop(0, seq // ATT_BLOCK, finish, 0)


def _dilated_attention(qkv_tiles, gain, batch, seq):
    n = batch * seq
    n_tiles = 3 * DIL_WIDTH // LANES
    return pl.pallas_call(
        _dilated_kernel,
        grid=(batch,),
        in_specs=[
            pl.BlockSpec((n_tiles, seq, LANES), lambda b: (0, b, 0)),
            pl.BlockSpec((1, DIL_WIDTH), lambda b: (0, 0)),
        ],
        out_specs=pl.BlockSpec((seq, DIL_WIDTH), lambda b: (b, 0)),
        out_shape=jax.ShapeDtypeStruct((n, DIL_WIDTH), BF16),
        scratch_shapes=[
            pltpu.VMEM((seq, LANES), F32),
            pltpu.VMEM((seq, LANES), F32),
            pltpu.VMEM((DIL_WIDTH // LANES, seq, LANES), F32),
        ],
        compiler_params=_params(("arbitrary",)),
        name="dilated_attention",
    )(qkv_tiles, gain)


def _diff_kernel(out_scale, lam_ref, q_ref, k_ref, v_ref, g_ref, o_ref, qt_ref, vt_ref, ot_ref,
                 qm_ref, m_ref, l_ref, acc_ref):
    seq = q_ref.shape[0]
    nblk = seq // DIFF_BLOCK
    lam = lam_ref[0]
    for i in range(nblk):
        rows = slice(i * DIFF_BLOCK, (i + 1) * DIFF_BLOCK)
        qt_ref[i] = q_ref[rows, :].astype(F32).T.astype(BF16)
        vt_ref[i] = v_ref[rows, :].astype(F32).T.astype(BF16)
    krow = lax.broadcasted_iota(jnp.int32, (DIFF_BLOCK, DIFF_BLOCK), 0)
    qcol = lax.broadcasted_iota(jnp.int32, (DIFF_BLOCK, DIFF_BLOCK), 1)
    causal = krow <= qcol
    chan = lax.broadcasted_iota(jnp.int32, (DIFF_WIDTH, DIFF_BLOCK), 0)

    n_maps = 2 * DIFF_HEADS

    def attend(kj, mask):
        k0 = pl.multiple_of(kj * DIFF_BLOCK, DIFF_BLOCK)
        k = k_ref[pl.ds(k0, DIFF_BLOCK), :]
        scores = [jnp.dot(k, qm_ref[c], preferred_element_type=F32) for c in range(n_maps)]
        probs, alphas = [], []
        for c in range(n_maps):
            s = scores[c]
            if mask:
                s = jnp.where(causal, s, NEG_INF)
            m_old = m_ref[c]
            m_new = jnp.maximum(m_old, jnp.max(s, axis=0, keepdims=True))
            alpha = jnp.exp(m_old - m_new)
            p = jnp.exp(s - m_new)
            m_ref[c] = m_new
            l_ref[c] = alpha * l_ref[c] + jnp.sum(p, axis=0, keepdims=True)
            probs.append(p.astype(BF16))
            alphas.append(alpha)
        for c in range(n_maps):
            h = c // 2
            vt = vt_ref[kj, h * HEAD_DIM:(h + 1) * HEAD_DIM, :]
            acc_ref[c] = alphas[c] * acc_ref[c] + jnp.dot(vt, probs[c],
                                                          preferred_element_type=F32)

    def q_block(qi, _):
        qt = qt_ref[qi]
        for c in range(n_maps):
            lo = c * DIFF_QK_DIM
            qm_ref[c] = jnp.where(jnp.logical_and(chan >= lo, chan < lo + DIFF_QK_DIM), qt,
                                  jnp.zeros((), BF16))
        m_ref[...] = jnp.full(m_ref.shape, NEG_INF, F32)
        l_ref[...] = jnp.zeros_like(l_ref)
        acc_ref[...] = jnp.zeros_like(acc_ref)

        def k_block(kj, _):
            attend(kj, False)
            return 0

        lax.fori_loop(0, qi, k_block, 0)
        attend(qi, True)
        for h in range(DIFF_HEADS):
            o = (acc_ref[2 * h] / l_ref[2 * h]
                 - lam * (acc_ref[2 * h + 1] / l_ref[2 * h + 1]))
            o = o * lax.rsqrt(jnp.mean(o * o, axis=0, keepdims=True) + NORM_EPS)
            ot_ref[qi, h * HEAD_DIM:(h + 1) * HEAD_DIM, :] = o * (g_ref[...] * out_scale)
        return 0

    lax.fori_loop(0, nblk, q_block, 0)

    for i in range(nblk):
        o_ref[i * DIFF_BLOCK:(i + 1) * DIFF_BLOCK, :] = ot_ref[i].T.astype(BF16)


def _diff_attention(qkv, lam, subln_g_col, out_scale, batch, seq):
    n = batch * seq
    blk = (seq, DIFF_WIDTH)
    nblk = seq // DIFF_BLOCK
    return pl.pallas_call(
        functools.partial(_diff_kernel, out_scale),
        grid=(batch,),
        in_specs=[
            pl.BlockSpec(memory_space=pltpu.SMEM),
            pl.BlockSpec(blk, lambda b: (b, 0)),
            pl.BlockSpec(blk, lambda b: (b, 1)),
            pl.BlockSpec(blk, lambda b: (b, 2)),
            pl.BlockSpec((HEAD_DIM, 1), lambda b: (0, 0)),
        ],
        out_specs=pl.BlockSpec(blk, lambda b: (b, 0)),
        out_shape=jax.ShapeDtypeStruct((n, DIFF_WIDTH), BF16),
        scratch_shapes=[
            pltpu.VMEM((nblk, DIFF_WIDTH, DIFF_BLOCK), BF16),
            pltpu.VMEM((nblk, DIFF_WIDTH, DIFF_BLOCK), BF16),
            pltpu.VMEM((nblk, DIFF_WIDTH, DIFF_BLOCK), F32),
            pltpu.VMEM((2 * DIFF_HEADS, DIFF_WIDTH, DIFF_BLOCK), BF16),
            pltpu.VMEM((2 * DIFF_HEADS, 1, DIFF_BLOCK), F32),
            pltpu.VMEM((2 * DIFF_HEADS, 1, DIFF_BLOCK), F32),
            pltpu.VMEM((2 * DIFF_HEADS, HEAD_DIM, DIFF_BLOCK), F32),
        ],
        compiler_params=_params(("arbitrary",)),
        name="diff_attention",
    )(lam, qkv, qkv, qkv, subln_g_col)


def _out_proj_kernel(x_ref, ssm_ref, dil_ref, diff_ref, w_ref, y_ref):
    acc = jnp.dot(ssm_ref[...], w_ref[:SSM_WIDTH, :], preferred_element_type=F32)
    acc = acc + jnp.dot(dil_ref[...], w_ref[SSM_WIDTH:SSM_WIDTH + DIL_WIDTH, :],
                        preferred_element_type=F32)
    acc = acc + jnp.dot(diff_ref[...], w_ref[SSM_WIDTH + DIL_WIDTH:, :],
                        preferred_element_type=F32)
    y_ref[...] = x_ref[...] + acc


def _out_proj(x2, y_ssm_tm, y_dil, y_diff, w_out, batch, seq):
    n = batch * seq
    blocks_per_seq = seq // PROJ_TOKENS
    tok = lambda w: pl.BlockSpec((PROJ_TOKENS, w), lambda i: (i, 0))
    return pl.pallas_call(
        _out_proj_kernel,
        grid=(n // PROJ_TOKENS,),
        in_specs=[
            tok(D_MODEL),
            pl.BlockSpec((PROJ_TOKENS, SSM_WIDTH),
                         lambda i: (i % blocks_per_seq, i // blocks_per_seq)),
            tok(DIL_WIDTH),
            tok(DIFF_WIDTH),
            pl.BlockSpec((D_MODEL, D_MODEL), lambda i: (0, 0)),
        ],
        out_specs=tok(D_MODEL),
        out_shape=jax.ShapeDtypeStruct((n, D_MODEL), F32),
        compiler_params=_params(("arbitrary",)),
        name="out_proj",
    )(x2, y_ssm_tm, y_dil, y_diff, w_out)


def _sorting_network(n):
    pairs = []
    p = 1
    while p < n:
        k = p
        while k >= 1:
            for j in range(k % p, n - k, 2 * k):
                for i in range(min(k, n - j - k)):
                    if (i + j) // (2 * p) == (i + j + k) // (2 * p):
                        pairs.append((i + j, i + j + k))
            k //= 2
        p *= 2
    return pairs


_SORT16 = _sorting_network(PEER_TOPK)
_BITONIC16 = [(i, i + s) for s in (8, 4, 2, 1) for i in range(PEER_TOPK) if (i // s) % 2 == 0]


def _top16_desc(vals):
    v = list(vals)
    for a, b in _SORT16:
        hi, lo = jnp.maximum(v[a], v[b]), jnp.minimum(v[a], v[b])
        v[a], v[b] = hi, lo
    for shift in (4, 2, 1):
        other = [pltpu.roll(x, shift, 0) for x in v]
        v = [jnp.maximum(v[k], other[PEER_TOPK - 1 - k]) for k in range(PEER_TOPK)]
        for a, b in _BITONIC16:
            hi, lo = jnp.maximum(v[a], v[b]), jnp.minimum(v[a], v[b])
            v[a], v[b] = hi, lo
    return v


def _peer_gates(sa, sb):
    t = sa.shape[1]
    a_top = _top16_desc([sa[SUBLANES * v:SUBLANES * (v + 1), :] for v in range(PEER_KEYS // SUBLANES)])
    b_top = _top16_desc([sb[SUBLANES * v:SUBLANES * (v + 1), :] for v in range(PEER_KEYS // SUBLANES)])
    sub = lax.broadcasted_iota(jnp.int32, (SUBLANES, t), 0)

    def pack(rows):
        out = rows[0]
        for s in range(1, SUBLANES):
            out = jnp.where(sub == s, rows[s], out)
        return out

    b_lo, b_hi, a_hi = pack(b_top[:8]), pack(b_top[8:]), pack(a_top[8:])
    cands = [a_top[0] + b_lo, a_top[0] + b_hi]
    cands += [a_top[k] + b_lo for k in range(1, 8)]
    cands += [a_hi + b_top[0]]
    valid = [None, None] + [sub < (PEER_TOPK // (k + 1)) for k in range(1, 8)] + [None]
    cands = [c if ok is None else jnp.where(ok, c, NEG_INF) for c, ok in zip(cands, valid)]
    pad = jnp.full((SUBLANES, t), NEG_INF, F32)
    tau = _top16_desc(cands + [pad] * (PEER_TOPK - len(cands)))[PEER_TOPK - 1]
    top = a_top[0] + b_top[0]
    z = jnp.zeros((SUBLANES, t), F32)
    for c in cands:
        z = z + jnp.where(c >= tau, jnp.exp(c - top), 0.0)
    for shift in (4, 2, 1):
        z = z + pltpu.roll(z, shift, 0)
    inv_z = 1.0 / z
    last = PEER_TOPK - 1

    def search(test):
        total = None
        bits = []
        for level, weight in enumerate((8, 4, 2, 1)):
            leaves = [b_top[m] for m in range(weight - 1, last, 2 * weight)]
            for bit in reversed(bits):
                leaves = [jnp.where(bit, leaves[2 * n + 1], leaves[2 * n])
                          for n in range(len(leaves) // 2)]
            bit = test(leaves[0])
            bits.append(bit)
            term = jnp.where(bit, float(weight), 0.0)
            total = term if total is None else total + term
        return total

    wa, count, wb, rank = [], [], [], []
    for v in range(PEER_KEYS // SUBLANES):
        xa = sa[SUBLANES * v:SUBLANES * (v + 1), :]
        xb = sb[SUBLANES * v:SUBLANES * (v + 1), :]
        wa.append(jnp.where(xa >= a_top[last], jnp.exp(xa - a_top[0]), 0.0) * inv_z)
        wb.append(jnp.where(xb >= b_top[last], jnp.exp(xb - b_top[0]), 0.0))
        count.append(search(lambda b: xa + b >= tau)
                     + jnp.where(xa + b_top[last] >= tau, 1.0, 0.0))
        rank.append(search(lambda b: b >= xb) + jnp.where(b_top[last] >= xb, 1.0, 0.0))
    cat = lambda parts: jnp.concatenate(parts, axis=0)
    return cat(wa), cat(count), cat(wb), cat(rank)


def _gelu_tanh(a):
    c = math.sqrt(2.0 / math.pi)
    inner = a * (a * a * (c * 0.044715) + c)
    return (a * 0.5) * (jnp.tanh(inner) + 1.0)


def _peer_kernel(final, x_ref, g_ref, wq_ref, keys_ref, u_ref, vt_ref, fg_ref, y_ref,
                 h_ref, sc_ref, wa_ref, cnt_ref, wb_ref, rank_ref, a0_ref, a1_ref, p_ref, acc_ref):
    s = pl.program_id(1)
    n_blocks = PEER_EXPERTS // PEER_EXPERT_BLOCK
    tb = x_ref.shape[0]
    n_chunks = tb // PEER_GATE_CHUNK
    rows_per_step = PEER_EXPERT_BLOCK // PEER_KEYS

    @pl.when(s == 0)
    def _():
        h_ref[...] = _rms(x_ref[...], g_ref[...]).astype(BF16)
        acc_ref[...] = jnp.zeros_like(acc_ref)

        chunks_per_stage = PEER_SCORE_TOKENS // PEER_GATE_CHUNK

        def stage(si, _):
            r0 = pl.multiple_of(si * PEER_SCORE_TOKENS, PEER_SCORE_TOKENS)
            hs = h_ref[pl.ds(r0, PEER_SCORE_TOKENS), :]
            qs = [jnp.dot(hs, wq_ref[hd], preferred_element_type=F32).astype(BF16)
                  for hd in range(PEER_HEADS)]
            for hd in range(PEER_HEADS):
                for half in range(2):
                    sc = _dot_nt(keys_ref[2 * hd + half],
                                 qs[hd][:, half * PEER_KEYS:(half + 1) * PEER_KEYS])
                    for c in range(chunks_per_stage):
                        sc_ref[c, 2 * hd + half] = sc[:, c * PEER_GATE_CHUNK:
                                                      (c + 1) * PEER_GATE_CHUNK]

            def unit(k, _):
                c = k // (PEER_HEADS // 2)
                ci = si * chunks_per_stage + c
                for hd_local in range(2):
                    hd = (k % (PEER_HEADS // 2)) * 2 + hd_local
                    wa, count, wb, rank = _peer_gates(sc_ref[c, 2 * hd], sc_ref[c, 2 * hd + 1])
                    wa_ref[ci, hd] = wa
                    cnt_ref[ci, hd] = count
                    wb_ref[ci, hd] = wb.astype(BF16)
                    rank_ref[ci, hd] = rank.astype(BF16)
                return 0

            lax.fori_loop(0, chunks_per_stage * PEER_HEADS // 2, unit, 0)
            return 0

        lax.fori_loop(0, tb // PEER_SCORE_TOKENS, stage, 0)

    packed_rows = 2 * SUBLANES
    tiles = PEER_KEYS // packed_rows

    def step(par, activate=True, gate_prev=True):
        a_new, a_old = (a0_ref, a1_ref) if par == 0 else (a1_ref, a0_ref)
        gate_block = s - 1
        for t0 in range(0, tb, PEER_TOKEN_TILE):
            tok = slice(t0, t0 + PEER_TOKEN_TILE)
            if activate:
                a_new[:, tok] = _dot_nt(u_ref[...], h_ref[tok, :]).astype(BF16)
            if not gate_prev:
                continue
            for ci in range(t0 // PEER_GATE_CHUNK, (t0 + PEER_TOKEN_TILE) // PEER_GATE_CHUNK):
                cols = slice(ci * PEER_GATE_CHUNK, (ci + 1) * PEER_GATE_CHUNK)
                for il in range(rows_per_step):
                    i = gate_block * rows_per_step + il
                    rows = slice(il * PEER_KEYS, (il + 1) * PEER_KEYS)
                    gate = jnp.zeros((tiles, packed_rows, PEER_GATE_CHUNK), BF16)
                    for hd in range(PEER_HEADS):
                        row = lambda ref: jnp.broadcast_to(
                            ref[ci, hd, pl.ds(i, 1), :],
                            (packed_rows, PEER_GATE_CHUNK)).astype(BF16)[None]
                        rank = rank_ref[ci, hd].reshape(tiles, packed_rows, PEER_GATE_CHUNK)
                        wb = wb_ref[ci, hd].reshape(tiles, packed_rows, PEER_GATE_CHUNK)
                        gate = gate + jnp.where(rank <= row(cnt_ref), wb * row(wa_ref),
                                                jnp.zeros((), BF16))
                    gate = gate.reshape(PEER_KEYS, PEER_GATE_CHUNK)
                    p_ref[rows, cols] = gate * _gelu_tanh(a_old[rows, cols])
            acc_ref[:, tok] += jnp.dot(vt_ref[...], p_ref[:, tok], preferred_element_type=F32)

    assert n_blocks % 2 == 0
    last = n_blocks
    inner = jnp.logical_and(s > 0, s < last)

    @pl.when(s == 0)
    def _():
        step(0, gate_prev=False)

    @pl.when(jnp.logical_and(inner, s % 2 == 0))
    def _():
        step(0)

    @pl.when(jnp.logical_and(inner, s % 2 == 1))
    def _():
        step(1)

    @pl.when(s == last)
    def _():
        step(0, activate=False)
        y = x_ref[...] + acc_ref[...].T
        if final:
            y = _rms(y, fg_ref[...])
        y_ref[...] = y


def _peer(x2, gain, wq_heads, keys, u_bf, vt_bf, final_gain, final):
    n = x2.shape[0]
    n_chunks = PEER_TOKENS // PEER_GATE_CHUNK
    n_blocks = PEER_EXPERTS // PEER_EXPERT_BLOCK
    gate_shape = (n_chunks, PEER_HEADS, PEER_KEYS, PEER_GATE_CHUNK)
    act_block = lambda s: jnp.minimum(s, n_blocks - 1)
    out_block = lambda s: jnp.maximum(s - 1, 0)
    once = pl.Buffered(1)
    return pl.pallas_call(
        functools.partial(_peer_kernel, final),
        grid=(n // PEER_TOKENS, n_blocks + 1),
        in_specs=[
            pl.BlockSpec((PEER_TOKENS, D_MODEL), lambda t, s: (t, 0), pipeline_mode=once),
            pl.BlockSpec((1, D_MODEL), lambda t, s: (0, 0)),
            pl.BlockSpec((PEER_HEADS, D_MODEL, 2 * PEER_KEYS), lambda t, s: (0, 0, 0),
                         pipeline_mode=once),
            pl.BlockSpec((2 * PEER_HEADS, PEER_KEYS, PEER_KEYS), lambda t, s: (0, 0, 0)),
            pl.BlockSpec((PEER_EXPERT_BLOCK, D_MODEL), lambda t, s: (act_block(s), 0)),
            pl.BlockSpec((None, D_MODEL, PEER_EXPERT_BLOCK), lambda t, s: (out_block(s), 0, 0)),
            pl.BlockSpec((1, D_MODEL), lambda t, s: (0, 0)),
        ],
        out_specs=pl.BlockSpec((PEER_TOKENS, D_MODEL), lambda t, s: (t, 0)),
        out_shape=jax.ShapeDtypeStruct((n, D_MODEL), F32),
        scratch_shapes=[
            pltpu.VMEM((PEER_TOKENS, D_MODEL), BF16),
            pltpu.VMEM((PEER_SCORE_TOKENS // PEER_GATE_CHUNK, 2 * PEER_HEADS, PEER_KEYS,
                        PEER_GATE_CHUNK), F32),
            pltpu.VMEM(gate_shape, F32),
            pltpu.VMEM(gate_shape, F32),
            pltpu.VMEM(gate_shape, BF16),
            pltpu.VMEM(gate_shape, BF16),
            pltpu.VMEM((PEER_EXPERT_BLOCK, PEER_TOKENS), BF16),
            pltpu.VMEM((PEER_EXPERT_BLOCK, PEER_TOKENS), BF16),
            pltpu.VMEM((PEER_EXPERT_BLOCK, PEER_TOKENS), BF16),
            pltpu.VMEM((D_MODEL, PEER_TOKENS), F32),
        ],
        compiler_params=_params(("arbitrary", "arbitrary")),
        name="peer",
    )(x2, gain, wq_heads, keys, u_bf, vt_bf, final_gain)


def _ssm_matrices(lam_re, lam_im, log_step, b_re, b_im, c_re, c_im):
    lam = lax.complex(lam_re, lam_im)
    step = jnp.exp(log_step)[:, None]
    lam_bar = jnp.exp(lam * step)
    b_bar = ((lam_bar - 1.0) / lam)[:, :, None] * lax.complex(b_re, b_im)
    eye = jnp.eye(SSM_GROUPS, dtype=F32)

    def embed_in(m):
        return jnp.einsum("gpc,gh->gchp", m, eye).reshape(SSM_WIDTH, SSM_HALF)

    def embed_out(m):
        return jnp.einsum("gcp,gh->gphc", m, eye).reshape(SSM_HALF, SSM_WIDTH)

    bmat = jnp.concatenate([embed_in(jnp.real(b_bar)), embed_in(jnp.imag(b_bar))], axis=1)
    cmat = jnp.concatenate([embed_out(c_re), embed_out(-c_im)], axis=0)
    lre = jnp.broadcast_to(jnp.real(lam_bar).reshape(1, SSM_HALF), (SUBLANES, SSM_HALF))
    lim = jnp.broadcast_to(jnp.imag(lam_bar).reshape(1, SSM_HALF), (SUBLANES, SSM_HALF))
    return bmat.astype(BF16), cmat.astype(BF16), lre, lim


def _expert_out_blocks(v_exp):
    blocks = v_exp.reshape(PEER_EXPERTS // PEER_EXPERT_BLOCK, PEER_EXPERT_BLOCK, D_MODEL)
    return blocks.transpose(0, 2, 1).astype(BF16)


def _in_proj_col_scale():
    s = jnp.ones((IN_WIDTH,), F32)
    s = s.at[SSM_WIDTH:SSM_WIDTH + DIL_WIDTH].set(HEAD_DIM ** -0.5)
    d0 = SSM_WIDTH + 3 * DIL_WIDTH
    s = s.at[d0:d0 + DIFF_WIDTH].set(DIFF_QK_DIM ** -0.5)
    return s.reshape(1, IN_WIDTH)


def kernel(x, norm1_g, w_in, ssm_lam_re, ssm_lam_im, ssm_log_step, ssm_b_re, ssm_b_im, ssm_c_re, ssm_c_im, ssm_d, ssm_w_glu, ssm_norm_g, dil_norm_g, diff_lam_q1, diff_lam_k1, diff_lam_q2, diff_lam_k2, diff_subln_g, w_out, norm2_g, peer_w_query, peer_sub_keys, peer_u, peer_v, final_norm_g):
    batch, seq, _ = x.shape
    n = batch * seq
    depth = w_in.shape[0]
    x2 = x.reshape(n, D_MODEL)
    col_scale = _in_proj_col_scale()
    row = lambda v: v.reshape(1, -1)
    for layer in range(depth):
        lambda_init = 0.8 - 0.6 * math.exp(-0.3 * layer)
        u_tm, dil_qkv, diff_qkv = _in_proj(x2, row(norm1_g[layer]), w_in[layer].astype(BF16),
                                           col_scale, batch, seq)
        bmat, cmat, lre, lim = _ssm_matrices(
            ssm_lam_re[layer], ssm_lam_im[layer], ssm_log_step[layer], ssm_b_re[layer],
            ssm_b_im[layer], ssm_c_re[layer], ssm_c_im[layer])
        y_ssm = _ssm(u_tm.reshape(seq * batch, SSM_WIDTH), bmat, cmat, lre, lim,
                     row(ssm_d[layer]), ssm_w_glu[layer].astype(BF16), row(ssm_norm_g[layer]),
                     batch, seq)
        y_dil = _dilated_attention(dil_qkv, row(dil_norm_g[layer]), batch, seq)
        lam = (jnp.exp(jnp.sum(diff_lam_q1[layer] * diff_lam_k1[layer]))
               - jnp.exp(jnp.sum(diff_lam_q2[layer] * diff_lam_k2[layer])) + lambda_init)
        y_diff = _diff_attention(diff_qkv, lam.reshape(1), diff_subln_g[layer].reshape(-1, 1),
                                 1.0 - lambda_init, batch, seq)
        x2 = _out_proj(x2, y_ssm.reshape(seq, batch * SSM_WIDTH), y_dil, y_diff,
                       w_out[layer].astype(BF16), batch, seq)
        wq_heads = peer_w_query[layer].reshape(D_MODEL, PEER_HEADS, 2 * PEER_KEYS)
        wq_heads = wq_heads.transpose(1, 0, 2).astype(BF16)
        keys = peer_sub_keys[layer].reshape(2 * PEER_HEADS, PEER_KEYS, PEER_KEYS).astype(BF16)
        x2 = _peer(x2, row(norm2_g[layer]), wq_heads, keys, peer_u[layer].astype(BF16),
                   _expert_out_blocks(peer_v[layer]), row(final_norm_g), layer == depth - 1)
    return x2.reshape(batch, seq, D_MODEL)
```

```python
import functools
import math

import jax
import jax.numpy as jnp
from jax import lax
from jax.experimental import pallas as pl
from jax.experimental.pallas import tpu as pltpu

F32 = jnp.float32
BF16 = jnp.bfloat16
F8 = jnp.float8_e4m3fn
FP8_LIMIT = 240.0

D_MODEL = 1024
HEAD_DIM = 64
SSM_WIDTH = 384
SSM_GROUP_CH = 16
SSM_GROUPS = 24
SSM_STATE = 64
SSM_HALF = SSM_GROUPS * SSM_STATE
DIL_WIDTH = 384
DIL_HEADS = 6
DIL_DILATIONS = (1, 4, 16)
DIFF_WIDTH = 256
DIFF_HEADS = 4
DIFF_QK_DIM = 32
IN_WIDTH = 2304
PEER_HEADS = 8
PEER_KEYS = 128
PEER_TOPK = 16
PEER_EXPERTS = PEER_KEYS * PEER_KEYS
NORM_EPS = 1e-6

LANES = 128
SUBLANES = 8
VMEM_LIMIT_BYTES = 56 * 1024 * 1024

PROJ_TOKENS = 512
SSM_CHUNK = 128
ATT_BLOCK = 128
DIFF_BLOCK = 256
PEER_TOKENS = 1024
PEER_GATE_CHUNK = LANES
PEER_EXPERT_BLOCK = 512
PEER_TOKEN_TILE = 256
PEER_SCORE_TOKENS = 256

NEG_INF = float("-inf")


def _rms(x, gain):
    return x * lax.rsqrt(jnp.mean(x * x, axis=-1, keepdims=True) + NORM_EPS) * gain


def _dot_nt(a, b):
    return lax.dot_general(a, b, (((1,), (1,)), ((), ())), preferred_element_type=F32)


def _params(semantics):
    return pltpu.CompilerParams(dimension_semantics=semantics, vmem_limit_bytes=VMEM_LIMIT_BYTES)


def _in_proj_kernel(x_ref, g_ref, w_ref, scale_ref, u_ref, dil_ref, diff_ref):
    h = _rms(x_ref[...], g_ref[...])
    p = jnp.dot(h.astype(BF16), w_ref[...], preferred_element_type=F32) * scale_ref[...]
    u_ref[...] = p[:, :SSM_WIDTH].astype(BF16)
    for t in range(dil_ref.shape[0]):
        dil_ref[t] = p[:, SSM_WIDTH + t * LANES:SSM_WIDTH + (t + 1) * LANES]
    diff_ref[...] = p[:, SSM_WIDTH + 3 * DIL_WIDTH:].astype(BF16)


def _in_proj(x2, gain, w_in, col_scale, batch, seq):
    n = batch * seq
    blocks_per_seq = seq // PROJ_TOKENS
    return pl.pallas_call(
        _in_proj_kernel,
        grid=(n // PROJ_TOKENS,),
        in_specs=[
            pl.BlockSpec((PROJ_TOKENS, D_MODEL), lambda i: (i, 0)),
            pl.BlockSpec((1, D_MODEL), lambda i: (0, 0)),
            pl.BlockSpec((D_MODEL, IN_WIDTH), lambda i: (0, 0)),
            pl.BlockSpec((1, IN_WIDTH), lambda i: (0, 0)),
        ],
        out_specs=[
            pl.BlockSpec((PROJ_TOKENS, SSM_WIDTH),
                         lambda i: (i % blocks_per_seq, i // blocks_per_seq)),
            pl.BlockSpec((3 * DIL_WIDTH // LANES, PROJ_TOKENS, LANES), lambda i: (0, i, 0)),
            pl.BlockSpec((PROJ_TOKENS, 3 * DIFF_WIDTH), lambda i: (i, 0)),
        ],
        out_shape=[
            jax.ShapeDtypeStruct((seq, batch * SSM_WIDTH), BF16),
            jax.ShapeDtypeStruct((3 * DIL_WIDTH // LANES, n, LANES), F32),
            jax.ShapeDtypeStruct((n, 3 * DIFF_WIDTH), BF16),
        ],
        compiler_params=_params(("arbitrary",)),
        name="in_proj",
    )(x2, gain, w_in, col_scale)


def _ssm_kernel(u_ref, bmat_ref, cmat_ref, lre_ref, lim_ref, d_ref, wglu_ref, g_ref,
                y_ref, s_ref, state_ref):
    rows = u_ref.shape[0]
    sub = 256

    @pl.when(pl.program_id(0) == 0)
    def _():
        state_ref[...] = jnp.zeros_like(state_ref)

    span = (LANES // SSM_GROUP_CH) * SSM_STATE
    blocks = [(slice(k * LANES, (k + 1) * LANES), slice(off + k * span, off + (k + 1) * span))
              for k in range(SSM_WIDTH // LANES) for off in (0, SSM_HALF)]
    for c in range(rows // sub):
        sl = slice(c * sub, (c + 1) * sub)
        for ch, st in blocks:
            s_ref[sl, st] = jnp.dot(u_ref[sl, ch], bmat_ref[ch, st], preferred_element_type=F32)

    lre = lre_ref[...]
    lim = lim_ref[...]

    def step(t, carry):
        sre, sim = carry
        r0 = pl.multiple_of(t * SUBLANES, SUBLANES)
        bre = s_ref[pl.ds(r0, SUBLANES), :SSM_HALF]
        bim = s_ref[pl.ds(r0, SUBLANES), SSM_HALF:]
        nre = lre * sre - lim * sim + bre
        nim = lre * sim + lim * sre + bim
        s_ref[pl.ds(r0, SUBLANES), :SSM_HALF] = nre
        s_ref[pl.ds(r0, SUBLANES), SSM_HALF:] = nim
        return nre, nim

    sre, sim = lax.fori_loop(0, rows // SUBLANES, step,
                             (state_ref[:, :SSM_HALF], state_ref[:, SSM_HALF:]), unroll=2)
    state_ref[:, :SSM_HALF] = sre
    state_ref[:, SSM_HALF:] = sim

    for c in range(rows // sub):
        sl = slice(c * sub, (c + 1) * sub)
        tiles = []
        for k in range(SSM_WIDTH // LANES):
            (ch, st_re), (_, st_im) = blocks[2 * k], blocks[2 * k + 1]
            tiles.append(
                jnp.dot(s_ref[sl, st_re].astype(BF16), cmat_ref[st_re, ch],
                        preferred_element_type=F32)
                + jnp.dot(s_ref[sl, st_im].astype(BF16), cmat_ref[st_im, ch],
                          preferred_element_type=F32))
        y = jnp.concatenate(tiles, axis=1)
        y = y + d_ref[...] * u_ref[sl, :].astype(F32)
        g = jax.nn.gelu(y)
        z = jnp.dot(g.astype(BF16), wglu_ref[...], preferred_element_type=F32)
        y = g * jax.nn.sigmoid(z)
        y_ref[sl, :] = _rms(y, g_ref[...]).astype(BF16)


def _ssm(u_tm, bmat, cmat, lre, lim, d_skip, w_glu, gain, batch, seq):
    assert batch == SUBLANES, "the S5 scan keeps one batch per sublane"
    rows = SSM_CHUNK * batch
    const = lambda i: (0, 0)
    return pl.pallas_call(
        _ssm_kernel,
        grid=(seq // SSM_CHUNK,),
        in_specs=[
            pl.BlockSpec((rows, SSM_WIDTH), lambda i: (i, 0)),
            pl.BlockSpec((SSM_WIDTH, 2 * SSM_HALF), const),
            pl.BlockSpec((2 * SSM_HALF, SSM_WIDTH), const),
            pl.BlockSpec((SUBLANES, SSM_HALF), const),
            pl.BlockSpec((SUBLANES, SSM_HALF), const),
            pl.BlockSpec((1, SSM_WIDTH), const),
            pl.BlockSpec((SSM_WIDTH, SSM_WIDTH), const),
            pl.BlockSpec((1, SSM_WIDTH), const),
        ],
        out_specs=pl.BlockSpec((rows, SSM_WIDTH), lambda i: (i, 0)),
        out_shape=jax.ShapeDtypeStruct((seq * batch, SSM_WIDTH), BF16),
        scratch_shapes=[
            pltpu.VMEM((rows, 2 * SSM_HALF), F32),
            pltpu.VMEM((SUBLANES, 2 * SSM_HALF), F32),
        ],
        compiler_params=_params(("arbitrary",)),
        name="ssm",
    )(u_tm, bmat, cmat, lre, lim, d_skip, w_glu, gain)


def _dilated_kernel(x_ref, g_ref, o_ref, m_ref, l_ref, acc_ref):
    seq = x_ref.shape[1]
    q_tiles = DIL_WIDTH // LANES
    heads_per_tile = LANES // HEAD_DIM
    m_ref[...] = jnp.full(m_ref.shape, NEG_INF, F32)
    l_ref[...] = jnp.zeros_like(l_ref)
    acc_ref[...] = jnp.zeros_like(acc_ref)
    row = lax.broadcasted_iota(jnp.int32, (ATT_BLOCK, ATT_BLOCK), 0)
    col = lax.broadcasted_iota(jnp.int32, (ATT_BLOCK, ATT_BLOCK), 1)
    lane = lax.broadcasted_iota(jnp.int32, (ATT_BLOCK, LANES), 1)
    cur_ok = col <= row
    spread = [(row == heads_per_tile * t + col // HEAD_DIM).astype(BF16) for t in range(q_tiles)]
    in_head = [col // HEAD_DIM == k for k in range(heads_per_tile)]
    assert heads_per_tile == 2

    def unit(dil, start, prev_start, prev_live):
        rows = pl.ds(start, ATT_BLOCK, stride=dil)
        load = lambda t, r: x_ref[t, r, :].astype(BF16)
        q = [load(t, rows) for t in range(q_tiles)]
        kc = [load(q_tiles + t, rows) for t in range(q_tiles)]
        vc = [load(2 * q_tiles + t, rows) for t in range(q_tiles)]
        has_prev = prev_start is not None
        if has_prev:
            prows = pl.ds(prev_start, ATT_BLOCK, stride=dil)
            kp = [load(q_tiles + t, prows) for t in range(q_tiles)]
            vp = [load(2 * q_tiles + t, prows) for t in range(q_tiles)]
            prev_ok = jnp.logical_and(col >= row, prev_live)
        tile_of = [h // heads_per_tile for h in range(DIL_HEADS)]
        qm = [jnp.where(in_head[h % heads_per_tile], q[tile_of[h]], jnp.zeros((), BF16))
              for h in range(DIL_HEADS)]
        sc_all = [_dot_nt(qm[h], kc[tile_of[h]]) for h in range(DIL_HEADS)]
        if has_prev:
            sp_all = [_dot_nt(qm[h], kp[tile_of[h]]) for h in range(DIL_HEADS)]
        pc_all, pp_all = [], []
        m_new_p = jnp.zeros((ATT_BLOCK, LANES), F32)
        l_new_p = jnp.zeros((ATT_BLOCK, LANES), F32)
        for h in range(DIL_HEADS):
            sc = jnp.where(cur_ok, sc_all[h], NEG_INF)
            m = jnp.max(sc, axis=-1, keepdims=True)
            if has_prev:
                sp = jnp.where(prev_ok, sp_all[h], NEG_INF)
                m = jnp.maximum(m, jnp.max(sp, axis=-1, keepdims=True))
            pc = jnp.exp(sc - m)
            den = jnp.sum(pc, axis=-1, keepdims=True)
            if has_prev:
                pp = jnp.exp(sp - m)
                den = den + jnp.sum(pp, axis=-1, keepdims=True)
                pp_all.append(pp.astype(BF16))
            pc_all.append(pc.astype(BF16))
            m_new_p = jnp.where(lane == h, m, m_new_p)
            l_new_p = jnp.where(lane == h, den, l_new_p)
        m_old = m_ref[rows, :]
        m_new = jnp.maximum(m_old, m_new_p)
        w_old = jnp.exp(m_old - m_new)
        w_pat = jnp.exp(m_new_p - m_new)
        m_ref[rows, :] = m_new
        l_ref[rows, :] = w_old * l_ref[rows, :] + w_pat * l_new_p
        for t in range(q_tiles):
            parts = []
            for h in range(heads_per_tile * t, heads_per_tile * (t + 1)):
                part = jnp.dot(pc_all[h], vc[t], preferred_element_type=F32)
                if has_prev:
                    part = part + jnp.dot(pp_all[h], vp[t], preferred_element_type=F32)
                parts.append(part)
            new = jnp.where(in_head[0], parts[0], parts[1])
            s_old = jnp.dot(w_old.astype(BF16), spread[t], preferred_element_type=F32)
            s_pat = jnp.dot(w_pat.astype(BF16), spread[t], preferred_element_type=F32)
            acc_ref[t, rows, :] = s_old * acc_ref[t, rows, :] + s_pat * new

    for dil in reversed(DIL_DILATIONS):
        sub_len = seq // dil
        nblk = sub_len // ATT_BLOCK
        units = dil * nblk

        def pair(u, _, dil=dil, nblk=nblk):
            for k in range(2):
                idx = 2 * u + k
                res, blk = idx % dil, idx // dil
                start = res + dil * ATT_BLOCK * blk
                if nblk == 1:
                    unit(dil, start, None, None)
                else:
                    prev = res + dil * ATT_BLOCK * jnp.maximum(blk - 1, 0)
                    unit(dil, start, prev, blk > 0)
            return 0

        lax.fori_loop(0, units // 2, pair, 0)

    def finish(blk, _):
        rows = pl.ds(pl.multiple_of(blk * ATT_BLOCK, ATT_BLOCK), ATT_BLOCK)
        inv = jnp.where(lane < DIL_HEADS, 1.0 / l_ref[rows, :], 0.0)
        inv_hi = inv.astype(BF16)
        inv_lo = (inv - inv_hi.astype(F32)).astype(BF16)
        ys = []
        for t in range(q_tiles):
            scale = (jnp.dot(inv_hi, spread[t], preferred_element_type=F32)
                     + jnp.dot(inv_lo, spread[t], preferred_element_type=F32))
            ys.append(acc_ref[t, rows, :] * scale)
        sq = sum(jnp.sum(y * y, axis=-1, keepdims=True) for y in ys)
        norm = lax.rsqrt(sq * (1.0 / DIL_WIDTH) + NORM_EPS)
        for t in range(q_tiles):
            cols = slice(t * LANES, (t + 1) * LANES)
            o_ref[rows, cols] = (ys[t] * norm * g_ref[:, cols]).astype(BF16)
        return 0

    lax.fori_loop(0, seq // ATT_BLOCK, finish, 0)


def _dilated_attention(qkv_tiles, gain, batch, seq):
    n = batch * seq
    n_tiles = 3 * DIL_WIDTH // LANES
    return pl.pallas_call(
        _dilated_kernel,
        grid=(batch,),
        in_specs=[
            pl.BlockSpec((n_tiles, seq, LANES), lambda b: (0, b, 0)),
            pl.BlockSpec((1, DIL_WIDTH), lambda b: (0, 0)),
        ],
        out_specs=pl.BlockSpec((seq, DIL_WIDTH), lambda b: (b, 0)),
        out_shape=jax.ShapeDtypeStruct((n, DIL_WIDTH), BF16),
        scratch_shapes=[
            pltpu.VMEM((seq, LANES), F32),
            pltpu.VMEM((seq, LANES), F32),
            pltpu.VMEM((DIL_WIDTH // LANES, seq, LANES), F32),
        ],
        compiler_params=_params(("arbitrary",)),
        name="dilated_attention",
    )(qkv_tiles, gain)


def _diff_kernel(out_scale, lam_ref, q_ref, k_ref, v_ref, g_ref, o_ref, qt_ref, vt_ref, ot_ref,
                 qm_ref, m_ref, l_ref, acc_ref):
    seq = q_ref.shape[0]
    nblk = seq // DIFF_BLOCK
    lam = lam_ref[0]
    for i in range(nblk):
        rows = slice(i * DIFF_BLOCK, (i + 1) * DIFF_BLOCK)
        qt_ref[i] = q_ref[rows, :].astype(F32).T.astype(BF16)
        vt_ref[i] = v_ref[rows, :].astype(F32).T.astype(BF16)
    krow = lax.broadcasted_iota(jnp.int32, (DIFF_BLOCK, DIFF_BLOCK), 0)
    qcol = lax.broadcasted_iota(jnp.int32, (DIFF_BLOCK, DIFF_BLOCK), 1)
    causal = krow <= qcol
    chan = lax.broadcasted_iota(jnp.int32, (DIFF_WIDTH, DIFF_BLOCK), 0)

    n_maps = 2 * DIFF_HEADS

    def attend(kj, mask):
        k0 = pl.multiple_of(kj * DIFF_BLOCK, DIFF_BLOCK)
        k = k_ref[pl.ds(k0, DIFF_BLOCK), :]
        scores = [jnp.dot(k, qm_ref[c], preferred_element_type=F32) for c in range(n_maps)]
        probs, alphas = [], []
        for c in range(n_maps):
            s = scores[c]
            if mask:
                s = jnp.where(causal, s, NEG_INF)
            m_old = m_ref[c]
            m_new = jnp.maximum(m_old, jnp.max(s, axis=0, keepdims=True))
            alpha = jnp.exp(m_old - m_new)
            p = jnp.exp(s - m_new)
            m_ref[c] = m_new
            l_ref[c] = alpha * l_ref[c] + jnp.sum(p, axis=0, keepdims=True)
            probs.append(p.astype(BF16))
            alphas.append(alpha)
        for c in range(n_maps):
            h = c // 2
            vt = vt_ref[kj, h * HEAD_DIM:(h + 1) * HEAD_DIM, :]
            acc_ref[c] = alphas[c] * acc_ref[c] + jnp.dot(vt, probs[c],
                                                          preferred_element_type=F32)

    def q_block(qi, _):
        qt = qt_ref[qi]
        for c in range(n_maps):
            lo = c * DIFF_QK_DIM
            qm_ref[c] = jnp.where(jnp.logical_and(chan >= lo, chan < lo + DIFF_QK_DIM), qt,
                                  jnp.zeros((), BF16))
        m_ref[...] = jnp.full(m_ref.shape, NEG_INF, F32)
        l_ref[...] = jnp.zeros_like(l_ref)
        acc_ref[...] = jnp.zeros_like(acc_ref)

        def k_block(kj, _):
            attend(kj, False)
            return 0

        lax.fori_loop(0, qi, k_block, 0)
        attend(qi, True)
        for h in range(DIFF_HEADS):
            o = (acc_ref[2 * h] / l_ref[2 * h]
                 - lam * (acc_ref[2 * h + 1] / l_ref[2 * h + 1]))
            o = o * lax.rsqrt(jnp.mean(o * o, axis=0, keepdims=True) + NORM_EPS)
            ot_ref[qi, h * HEAD_DIM:(h + 1) * HEAD_DIM, :] = o * (g_ref[...] * out_scale)
        return 0

    lax.fori_loop(0, nblk, q_block, 0)

    for i in range(nblk):
        o_ref[i * DIFF_BLOCK:(i + 1) * DIFF_BLOCK, :] = ot_ref[i].T.astype(BF16)


def _diff_attention(qkv, lam, subln_g_col, out_scale, batch, seq):
    n = batch * seq
    blk = (seq, DIFF_WIDTH)
    nblk = seq // DIFF_BLOCK
    return pl.pallas_call(
        functools.partial(_diff_kernel, out_scale),
        grid=(batch,),
        in_specs=[
            pl.BlockSpec(memory_space=pltpu.SMEM),
            pl.BlockSpec(blk, lambda b: (b, 0)),
            pl.BlockSpec(blk, lambda b: (b, 1)),
            pl.BlockSpec(blk, lambda b: (b, 2)),
            pl.BlockSpec((HEAD_DIM, 1), lambda b: (0, 0)),
        ],
        out_specs=pl.BlockSpec(blk, lambda b: (b, 0)),
        out_shape=jax.ShapeDtypeStruct((n, DIFF_WIDTH), BF16),
        scratch_shapes=[
            pltpu.VMEM((nblk, DIFF_WIDTH, DIFF_BLOCK), BF16),
            pltpu.VMEM((nblk, DIFF_WIDTH, DIFF_BLOCK), BF16),
            pltpu.VMEM((nblk, DIFF_WIDTH, DIFF_BLOCK), F32),
            pltpu.VMEM((2 * DIFF_HEADS, DIFF_WIDTH, DIFF_BLOCK), BF16),
            pltpu.VMEM((2 * DIFF_HEADS, 1, DIFF_BLOCK), F32),
            pltpu.VMEM((2 * DIFF_HEADS, 1, DIFF_BLOCK), F32),
            pltpu.VMEM((2 * DIFF_HEADS, HEAD_DIM, DIFF_BLOCK), F32),
        ],
        compiler_params=_params(("arbitrary",)),
        name="diff_attention",
    )(lam, qkv, qkv, qkv, subln_g_col)


def _out_proj_kernel(x_ref, ssm_ref, dil_ref, diff_ref, w_ref, y_ref):
    acc = jnp.dot(ssm_ref[...], w_ref[:SSM_WIDTH, :], preferred_element_type=F32)
    acc = acc + jnp.dot(dil_ref[...], w_ref[SSM_WIDTH:SSM_WIDTH + DIL_WIDTH, :],
                        preferred_element_type=F32)
    acc = acc + jnp.dot(diff_ref[...], w_ref[SSM_WIDTH + DIL_WIDTH:, :],
                        preferred_element_type=F32)
    y_ref[...] = x_ref[...] + acc


def _out_proj(x2, y_ssm_tm, y_dil, y_diff, w_out, batch, seq):
    n = batch * seq
    blocks_per_seq = seq // PROJ_TOKENS
    tok = lambda w: pl.BlockSpec((PROJ_TOKENS, w), lambda i: (i, 0))
    return pl.pallas_call(
        _out_proj_kernel,
        grid=(n // PROJ_TOKENS,),
        in_specs=[
            tok(D_MODEL),
            pl.BlockSpec((PROJ_TOKENS, SSM_WIDTH),
                         lambda i: (i % blocks_per_seq, i // blocks_per_seq)),
            tok(DIL_WIDTH),
            tok(DIFF_WIDTH),
            pl.BlockSpec((D_MODEL, D_MODEL), lambda i: (0, 0)),
        ],
        out_specs=tok(D_MODEL),
        out_shape=jax.ShapeDtypeStruct((n, D_MODEL), F32),
        compiler_params=_params(("arbitrary",)),
        name="out_proj",
    )(x2, y_ssm_tm, y_dil, y_diff, w_out)


def _sorting_network(n):
    pairs = []
    p = 1
    while p < n:
        k = p
        while k >= 1:
            for j in range(k % p, n - k, 2 * k):
                for i in range(min(k, n - j - k)):
                    if (i + j) // (2 * p) == (i + j + k) // (2 * p):
                        pairs.append((i + j, i + j + k))
            k //= 2
        p *= 2
    return pairs


_SORT16 = _sorting_network(PEER_TOPK)
_BITONIC16 = [(i, i + s) for s in (8, 4, 2, 1) for i in range(PEER_TOPK) if (i // s) % 2 == 0]


def _top16_desc(vals):
    v = list(vals)
    for a, b in _SORT16:
        hi, lo = jnp.maximum(v[a], v[b]), jnp.minimum(v[a], v[b])
        v[a], v[b] = hi, lo
    for shift in (4, 2, 1):
        other = [pltpu.roll(x, shift, 0) for x in v]
        v = [jnp.maximum(v[k], other[PEER_TOPK - 1 - k]) for k in range(PEER_TOPK)]
        for a, b in _BITONIC16:
            hi, lo = jnp.maximum(v[a], v[b]), jnp.minimum(v[a], v[b])
            v[a], v[b] = hi, lo
    return v


def _peer_gates(sa, sb):
    t = sa.shape[1]
    a_top = _top16_desc([sa[SUBLANES * v:SUBLANES * (v + 1), :] for v in range(PEER_KEYS // SUBLANES)])
    b_top = _top16_desc([sb[SUBLANES * v:SUBLANES * (v + 1), :] for v in range(PEER_KEYS // SUBLANES)])
    sub = lax.broadcasted_iota(jnp.int32, (SUBLANES, t), 0)

    def pack(rows):
        out = rows[0]
        for s in range(1, SUBLANES):
            out = jnp.where(sub == s, rows[s], out)
        return out

    b_lo, b_hi, a_hi = pack(b_top[:8]), pack(b_top[8:]), pack(a_top[8:])
    cands = [a_top[0] + b_lo, a_top[0] + b_hi]
    cands += [a_top[k] + b_lo for k in range(1, 8)]
    cands += [a_hi + b_top[0]]
    valid = [None, None] + [sub < (PEER_TOPK // (k + 1)) for k in range(1, 8)] + [None]
    cands = [c if ok is None else jnp.where(ok, c, NEG_INF) for c, ok in zip(cands, valid)]
    pad = jnp.full((SUBLANES, t), NEG_INF, F32)
    tau = _top16_desc(cands + [pad] * (PEER_TOPK - len(cands)))[PEER_TOPK - 1]
    top = a_top[0] + b_top[0]
    z = jnp.zeros((SUBLANES, t), F32)
    for c in cands:
        z = z + jnp.where(c >= tau, jnp.exp(c - top), 0.0)
    for shift in (4, 2, 1):
        z = z + pltpu.roll(z, shift, 0)
    inv_z = 1.0 / z
    last = PEER_TOPK - 1

    def search(test):
        total = None
        bits = []
        for level, weight in enumerate((8, 4, 2, 1)):
            leaves = [b_top[m] for m in range(weight - 1, last, 2 * weight)]
            for bit in reversed(bits):
                leaves = [jnp.where(bit, leaves[2 * n + 1], leaves[2 * n])
                          for n in range(len(leaves) // 2)]
            bit = test(leaves[0])
            bits.append(bit)
            term = jnp.where(bit, float(weight), 0.0)
            total = term if total is None else total + term
        return total

    wa, count, wb, rank = [], [], [], []
    for v in range(PEER_KEYS // SUBLANES):
        xa = sa[SUBLANES * v:SUBLANES * (v + 1), :]
        xb = sb[SUBLANES * v:SUBLANES * (v + 1), :]
        wa.append(jnp.where(xa >= a_top[last], jnp.exp(xa - a_top[0]), 0.0) * inv_z)
        wb.append(jnp.where(xb >= b_top[last], jnp.exp(xb - b_top[0]), 0.0))
        count.append(search(lambda b: xa + b >= tau)
                     + jnp.where(xa + b_top[last] >= tau, 1.0, 0.0))
        rank.append(search(lambda b: b >= xb) + jnp.where(b_top[last] >= xb, 1.0, 0.0))
    cat = lambda parts: jnp.concatenate(parts, axis=0)
    return cat(wa), cat(count), cat(wb), cat(rank)


def _gelu_tanh(a):
    c = math.sqrt(2.0 / math.pi)
    inner = a * (a * a * (c * 0.044715) + c)
    return (a * 0.5) * (jnp.tanh(inner) + 1.0)


def _peer_kernel(final, scale_ref, x_ref, g_ref, wq_ref, keys_ref, u_ref, vt_ref, fg_ref, y_ref,
                 h_ref, h8_ref, sc_ref, wa_ref, cnt_ref, wb_ref, rank_ref, a0_ref, a1_ref, p_ref,
                 acc_ref):
    h_scale, act_unscale, p_scale, out_unscale = (scale_ref[k] for k in range(4))
    s = pl.program_id(1)
    n_blocks = PEER_EXPERTS // PEER_EXPERT_BLOCK
    tb = x_ref.shape[0]
    n_chunks = tb // PEER_GATE_CHUNK
    rows_per_step = PEER_EXPERT_BLOCK // PEER_KEYS

    @pl.when(s == 0)
    def _():
        h = _rms(x_ref[...], g_ref[...])
        h_ref[...] = h.astype(BF16)
        h8_ref[...] = (h * h_scale).astype(F8)
        acc_ref[...] = jnp.zeros_like(acc_ref)

        chunks_per_stage = PEER_SCORE_TOKENS // PEER_GATE_CHUNK

        def stage(si, _):
            r0 = pl.multiple_of(si * PEER_SCORE_TOKENS, PEER_SCORE_TOKENS)
            hs = h_ref[pl.ds(r0, PEER_SCORE_TOKENS), :]
            qs = [jnp.dot(hs, wq_ref[hd], preferred_element_type=F32).astype(BF16)
                  for hd in range(PEER_HEADS)]
            for hd in range(PEER_HEADS):
                for half in range(2):
                    sc = _dot_nt(keys_ref[2 * hd + half],
                                 qs[hd][:, half * PEER_KEYS:(half + 1) * PEER_KEYS])
                    for c in range(chunks_per_stage):
                        sc_ref[c, 2 * hd + half] = sc[:, c * PEER_GATE_CHUNK:
                                                      (c + 1) * PEER_GATE_CHUNK]

            def unit(k, _):
                c = k // (PEER_HEADS // 2)
                ci = si * chunks_per_stage + c
                for hd_local in range(2):
                    hd = (k % (PEER_HEADS // 2)) * 2 + hd_local
                    wa, count, wb, rank = _peer_gates(sc_ref[c, 2 * hd], sc_ref[c, 2 * hd + 1])
                    wa_ref[ci, hd] = wa * p_scale
                    cnt_ref[ci, hd] = count
                    wb_ref[ci, hd] = wb.astype(BF16)
                    rank_ref[ci, hd] = rank.astype(BF16)
                return 0

            lax.fori_loop(0, chunks_per_stage * PEER_HEADS // 2, unit, 0)
            return 0

        lax.fori_loop(0, tb // PEER_SCORE_TOKENS, stage, 0)

    packed_rows = 2 * SUBLANES
    tiles = PEER_KEYS // packed_rows

    def step(par, activate=True, gate_prev=True):
        a_new, a_old = (a0_ref, a1_ref) if par == 0 else (a1_ref, a0_ref)
        gate_block = s - 1
        for t0 in range(0, tb, PEER_TOKEN_TILE):
            tok = slice(t0, t0 + PEER_TOKEN_TILE)
            if activate:
                a_new[:, tok] = (_dot_nt(u_ref[...], h8_ref[tok, :]) * act_unscale).astype(BF16)
            if not gate_prev:
                continue
            for ci in range(t0 // PEER_GATE_CHUNK, (t0 + PEER_TOKEN_TILE) // PEER_GATE_CHUNK):
                cols = slice(ci * PEER_GATE_CHUNK, (ci + 1) * PEER_GATE_CHUNK)
                for il in range(rows_per_step):
                    i = gate_block * rows_per_step + il
                    rows = slice(il * PEER_KEYS, (il + 1) * PEER_KEYS)
                    gate = jnp.zeros((tiles, packed_rows, PEER_GATE_CHUNK), BF16)
                    for hd in range(PEER_HEADS):
                        row = lambda ref: jnp.broadcast_to(
                            ref[ci, hd, pl.ds(i, 1), :],
                            (packed_rows, PEER_GATE_CHUNK)).astype(BF16)[None]
                        rank = rank_ref[ci, hd].reshape(tiles, packed_rows, PEER_GATE_CHUNK)
                        wb = wb_ref[ci, hd].reshape(tiles, packed_rows, PEER_GATE_CHUNK)
                        gate = gate + jnp.where(rank <= row(cnt_ref), wb * row(wa_ref),
                                                jnp.zeros((), BF16))
                    gate = gate.reshape(PEER_KEYS, PEER_GATE_CHUNK)
                    p_ref[rows, cols] = (gate * _gelu_tanh(a_old[rows, cols])).astype(F8)
            acc_ref[:, tok] += jnp.dot(vt_ref[...], p_ref[:, tok], preferred_element_type=F32)

    assert n_blocks % 2 == 0
    last = n_blocks
    inner = jnp.logical_and(s > 0, s < last)

    @pl.when(s == 0)
    def _():
        step(0, gate_prev=False)

    @pl.when(jnp.logical_and(inner, s % 2 == 0))
    def _():
        step(0)

    @pl.when(jnp.logical_and(inner, s % 2 == 1))
    def _():
        step(1)

    @pl.when(s == last)
    def _():
        step(0, activate=False)
        y = x_ref[...] + acc_ref[...].T * out_unscale
        if final:
            y = _rms(y, fg_ref[...])
        y_ref[...] = y


def _peer(x2, gain, wq_heads, keys, u_q, vt_q, scales, final_gain, final):
    n = x2.shape[0]
    n_chunks = PEER_TOKENS // PEER_GATE_CHUNK
    n_blocks = PEER_EXPERTS // PEER_EXPERT_BLOCK
    gate_shape = (n_chunks, PEER_HEADS, PEER_KEYS, PEER_GATE_CHUNK)
    act_block = lambda s: jnp.minimum(s, n_blocks - 1)
    out_block = lambda s: jnp.maximum(s - 1, 0)
    once = pl.Buffered(1)
    return pl.pallas_call(
        functools.partial(_peer_kernel, final),
        grid=(n // PEER_TOKENS, n_blocks + 1),
        in_specs=[
            pl.BlockSpec(memory_space=pltpu.SMEM),
            pl.BlockSpec((PEER_TOKENS, D_MODEL), lambda t, s: (t, 0), pipeline_mode=once),
            pl.BlockSpec((1, D_MODEL), lambda t, s: (0, 0)),
            pl.BlockSpec((PEER_HEADS, D_MODEL, 2 * PEER_KEYS), lambda t, s: (0, 0, 0),
                         pipeline_mode=once),
            pl.BlockSpec((2 * PEER_HEADS, PEER_KEYS, PEER_KEYS), lambda t, s: (0, 0, 0)),
            pl.BlockSpec((PEER_EXPERT_BLOCK, D_MODEL), lambda t, s: (act_block(s), 0)),
            pl.BlockSpec((None, D_MODEL, PEER_EXPERT_BLOCK), lambda t, s: (out_block(s), 0, 0)),
            pl.BlockSpec((1, D_MODEL), lambda t, s: (0, 0)),
        ],
        out_specs=pl.BlockSpec((PEER_TOKENS, D_MODEL), lambda t, s: (t, 0)),
        out_shape=jax.ShapeDtypeStruct((n, D_MODEL), F32),
        scratch_shapes=[
            pltpu.VMEM((PEER_TOKENS, D_MODEL), BF16),
            pltpu.VMEM((PEER_TOKENS, D_MODEL), F8),
            pltpu.VMEM((PEER_SCORE_TOKENS // PEER_GATE_CHUNK, 2 * PEER_HEADS, PEER_KEYS,
                        PEER_GATE_CHUNK), F32),
            pltpu.VMEM(gate_shape, F32),
            pltpu.VMEM(gate_shape, F32),
            pltpu.VMEM(gate_shape, BF16),
            pltpu.VMEM(gate_shape, BF16),
            pltpu.VMEM((PEER_EXPERT_BLOCK, PEER_TOKENS), BF16),
            pltpu.VMEM((PEER_EXPERT_BLOCK, PEER_TOKENS), BF16),
            pltpu.VMEM((PEER_EXPERT_BLOCK, PEER_TOKENS), F8),
            pltpu.VMEM((D_MODEL, PEER_TOKENS), F32),
        ],
        compiler_params=_params(("arbitrary", "arbitrary")),
        name="peer",
    )(scales, x2, gain, wq_heads, keys, u_q, vt_q, final_gain)


def _ssm_matrices(lam_re, lam_im, log_step, b_re, b_im, c_re, c_im):
    lam = lax.complex(lam_re, lam_im)
    step = jnp.exp(log_step)[:, None]
    lam_bar = jnp.exp(lam * step)
    b_bar = ((lam_bar - 1.0) / lam)[:, :, None] * lax.complex(b_re, b_im)
    eye = jnp.eye(SSM_GROUPS, dtype=F32)

    def embed_in(m):
        return jnp.einsum("gpc,gh->gchp", m, eye).reshape(SSM_WIDTH, SSM_HALF)

    def embed_out(m):
        return jnp.einsum("gcp,gh->gphc", m, eye).reshape(SSM_HALF, SSM_WIDTH)

    bmat = jnp.concatenate([embed_in(jnp.real(b_bar)), embed_in(jnp.imag(b_bar))], axis=1)
    cmat = jnp.concatenate([embed_out(c_re), embed_out(-c_im)], axis=0)
    lre = jnp.broadcast_to(jnp.real(lam_bar).reshape(1, SSM_HALF), (SUBLANES, SSM_HALF))
    lim = jnp.broadcast_to(jnp.imag(lam_bar).reshape(1, SSM_HALF), (SUBLANES, SSM_HALF))
    return bmat.astype(BF16), cmat.astype(BF16), lre, lim


def _pow2_scale(bound):
    return jnp.exp2(jnp.floor(jnp.log2(FP8_LIMIT / jnp.maximum(bound, 1e-30))))


def _peer_tables(u_exp, v_exp, gain):
    h_bound = math.sqrt(D_MODEL) * jnp.max(jnp.abs(gain))
    u_norm = jnp.sqrt(jnp.max(jnp.sum(u_exp * u_exp, axis=1)))
    u_scale = _pow2_scale(u_norm)
    v_scale = _pow2_scale(jnp.max(jnp.abs(v_exp)))
    h_scale = _pow2_scale(h_bound)
    p_scale = _pow2_scale(u_norm * h_bound)
    u_q = (u_exp * u_scale).astype(F8)
    blocks = v_exp.reshape(PEER_EXPERTS // PEER_EXPERT_BLOCK, PEER_EXPERT_BLOCK, D_MODEL)
    vt_q = (blocks.transpose(0, 2, 1) * v_scale).astype(F8)
    scales = jnp.stack([h_scale, 1.0 / (h_scale * u_scale), p_scale, 1.0 / (p_scale * v_scale)])
    return u_q, vt_q, scales.astype(F32)


def _in_proj_col_scale():
    s = jnp.ones((IN_WIDTH,), F32)
    s = s.at[SSM_WIDTH:SSM_WIDTH + DIL_WIDTH].set(HEAD_DIM ** -0.5)
    d0 = SSM_WIDTH + 3 * DIL_WIDTH
    s = s.at[d0:d0 + DIFF_WIDTH].set(DIFF_QK_DIM ** -0.5)
    return s.reshape(1, IN_WIDTH)


def kernel(x, norm1_g, w_in, ssm_lam_re, ssm_lam_im, ssm_log_step, ssm_b_re, ssm_b_im, ssm_c_re, ssm_c_im, ssm_d, ssm_w_glu, ssm_norm_g, dil_norm_g, diff_lam_q1, diff_lam_k1, diff_lam_q2, diff_lam_k2, diff_subln_g, w_out, norm2_g, peer_w_query, peer_sub_keys, peer_u, peer_v, final_norm_g):
    batch, seq, _ = x.shape
    n = batch * seq
    depth = w_in.shape[0]
    x2 = x.reshape(n, D_MODEL)
    col_scale = _in_proj_col_scale()
    row = lambda v: v.reshape(1, -1)
    for layer in range(depth):
        lambda_init = 0.8 - 0.6 * math.exp(-0.3 * layer)
        u_tm, dil_qkv, diff_qkv = _in_proj(x2, row(norm1_g[layer]), w_in[layer].astype(BF16),
                                           col_scale, batch, seq)
        bmat, cmat, lre, lim = _ssm_matrices(
            ssm_lam_re[layer], ssm_lam_im[layer], ssm_log_step[layer], ssm_b_re[layer],
            ssm_b_im[layer], ssm_c_re[layer], ssm_c_im[layer])
        y_ssm = _ssm(u_tm.reshape(seq * batch, SSM_WIDTH), bmat, cmat, lre, lim,
                     row(ssm_d[layer]), ssm_w_glu[layer].astype(BF16), row(ssm_norm_g[layer]),
                     batch, seq)
        y_dil = _dilated_attention(dil_qkv, row(dil_norm_g[layer]), batch, seq)
        lam = (jnp.exp(jnp.sum(diff_lam_q1[layer] * diff_lam_k1[layer]))
               - jnp.exp(jnp.sum(diff_lam_q2[layer] * diff_lam_k2[layer])) + lambda_init)
        y_diff = _diff_attention(diff_qkv, lam.reshape(1), diff_subln_g[layer].reshape(-1, 1),
                                 1.0 - lambda_init, batch, seq)
        x2 = _out_proj(x2, y_ssm.reshape(seq, batch * SSM_WIDTH), y_dil, y_diff,
                       w_out[layer].astype(BF16), batch, seq)
        wq_heads = peer_w_query[layer].reshape(D_MODEL, PEER_HEADS, 2 * PEER_KEYS)
        wq_heads = wq_heads.transpose(1, 0, 2).astype(BF16)
        keys = peer_sub_keys[layer].reshape(2 * PEER_HEADS, PEER_KEYS, PEER_KEYS).astype(BF16)
        u_q, vt_q, scales = _peer_tables(peer_u[layer], peer_v[layer], norm2_g[layer])
        x2 = _peer(x2, row(norm2_g[layer]), wq_heads, keys, u_q, vt_q, scales,
                   row(final_norm_g), layer == depth - 1)
    return x2.reshape(batch, seq, D_MODEL)
```

```python
import functools
import math

import jax
import jax.numpy as jnp
from jax import lax
from jax.experimental import pallas as pl
from jax.experimental.pallas import tpu as pltpu

F32 = jnp.float32
BF16 = jnp.bfloat16
F8 = jnp.float8_e4m3fn
FP8_LIMIT = 240.0

D_MODEL = 1024
HEAD_DIM = 64
SSM_WIDTH = 384
SSM_GROUP_CH = 16
SSM_GROUPS = 24
SSM_STATE = 64
SSM_HALF = SSM_GROUPS * SSM_STATE
DIL_WIDTH = 384
DIL_HEADS = 6
DIL_DILATIONS = (1, 4, 16)
DIFF_WIDTH = 256
DIFF_HEADS = 4
DIFF_QK_DIM = 32
IN_WIDTH = 2304
PEER_HEADS = 8
PEER_KEYS = 128
PEER_TOPK = 16
PEER_EXPERTS = PEER_KEYS * PEER_KEYS
NORM_EPS = 1e-6

LANES = 128
SUBLANES = 8
VMEM_LIMIT_BYTES = 56 * 1024 * 1024

PROJ_TOKENS = 1024
SSM_CHUNK = 128
ATT_BLOCK = 128
DIFF_BLOCK = 256
PEER_TOKENS = 1024
PEER_GATE_CHUNK = LANES
PEER_EXPERT_BLOCK = 1024
PEER_TOKEN_TILE = 256
PEER_SCORE_TOKENS = 256

NEG_INF = float("-inf")


def _rms(x, gain):
    return x * lax.rsqrt(jnp.mean(x * x, axis=-1, keepdims=True) + NORM_EPS) * gain


def _dot_nt(a, b):
    return lax.dot_general(a, b, (((1,), (1,)), ((), ())), preferred_element_type=F32)


def _params(semantics):
    return pltpu.CompilerParams(dimension_semantics=semantics, vmem_limit_bytes=VMEM_LIMIT_BYTES)


def _in_proj_kernel(x_ref, g_ref, w_ref, scale_ref, u_ref, dil_ref, diff_ref):
    h = _rms(x_ref[...], g_ref[...])
    p = jnp.dot(h.astype(BF16), w_ref[...], preferred_element_type=F32) * scale_ref[...]
    u_ref[...] = p[:, :SSM_WIDTH].astype(BF16)
    for t in range(dil_ref.shape[0]):
        dil_ref[t] = p[:, SSM_WIDTH + t * LANES:SSM_WIDTH + (t + 1) * LANES]
    diff_ref[...] = p[:, SSM_WIDTH + 3 * DIL_WIDTH:].astype(BF16)


def _in_proj(x2, gain, w_in, col_scale, batch, seq):
    n = batch * seq
    blocks_per_seq = seq // PROJ_TOKENS
    return pl.pallas_call(
        _in_proj_kernel,
        grid=(n // PROJ_TOKENS,),
        in_specs=[
            pl.BlockSpec((PROJ_TOKENS, D_MODEL), lambda i: (i, 0)),
            pl.BlockSpec((1, D_MODEL), lambda i: (0, 0)),
            pl.BlockSpec((D_MODEL, IN_WIDTH), lambda i: (0, 0)),
            pl.BlockSpec((1, IN_WIDTH), lambda i: (0, 0)),
        ],
        out_specs=[
            pl.BlockSpec((PROJ_TOKENS, SSM_WIDTH),
                         lambda i: (i % blocks_per_seq, i // blocks_per_seq)),
            pl.BlockSpec((3 * DIL_WIDTH // LANES, PROJ_TOKENS, LANES), lambda i: (0, i, 0)),
            pl.BlockSpec((PROJ_TOKENS, 3 * DIFF_WIDTH), lambda i: (i, 0)),
        ],
        out_shape=[
            jax.ShapeDtypeStruct((seq, batch * SSM_WIDTH), BF16),
            jax.ShapeDtypeStruct((3 * DIL_WIDTH // LANES, n, LANES), F32),
            jax.ShapeDtypeStruct((n, 3 * DIFF_WIDTH), BF16),
        ],
        compiler_params=_params(("arbitrary",)),
        name="in_proj",
    )(x2, gain, w_in, col_scale)


def _ssm_kernel(u_ref, bmat_ref, cmat_ref, lre_ref, lim_ref, d_ref, wglu_ref, g_ref,
                y_ref, s_ref, state_ref):
    rows = u_ref.shape[0]
    sub = 256

    @pl.when(pl.program_id(0) == 0)
    def _():
        state_ref[...] = jnp.zeros_like(state_ref)

    span = (LANES // SSM_GROUP_CH) * SSM_STATE
    blocks = [(slice(k * LANES, (k + 1) * LANES), slice(off + k * span, off + (k + 1) * span))
              for k in range(SSM_WIDTH // LANES) for off in (0, SSM_HALF)]
    for c in range(rows // sub):
        sl = slice(c * sub, (c + 1) * sub)
        for ch, st in blocks:
            s_ref[sl, st] = jnp.dot(u_ref[sl, ch], bmat_ref[ch, st], preferred_element_type=F32)

    lre = lre_ref[...]
    lim = lim_ref[...]

    def step(t, carry):
        sre, sim = carry
        r0 = pl.multiple_of(t * SUBLANES, SUBLANES)
        bre = s_ref[pl.ds(r0, SUBLANES), :SSM_HALF]
        bim = s_ref[pl.ds(r0, SUBLANES), SSM_HALF:]
        nre = lre * sre - lim * sim + bre
        nim = lre * sim + lim * sre + bim
        s_ref[pl.ds(r0, SUBLANES), :SSM_HALF] = nre
        s_ref[pl.ds(r0, SUBLANES), SSM_HALF:] = nim
        return nre, nim

    sre, sim = lax.fori_loop(0, rows // SUBLANES, step,
                             (state_ref[:, :SSM_HALF], state_ref[:, SSM_HALF:]), unroll=2)
    state_ref[:, :SSM_HALF] = sre
    state_ref[:, SSM_HALF:] = sim

    for c in range(rows // sub):
        sl = slice(c * sub, (c + 1) * sub)
        tiles = []
        for k in range(SSM_WIDTH // LANES):
            (ch, st_re), (_, st_im) = blocks[2 * k], blocks[2 * k + 1]
            tiles.append(
                jnp.dot(s_ref[sl, st_re].astype(BF16), cmat_ref[st_re, ch],
                        preferred_element_type=F32)
                + jnp.dot(s_ref[sl, st_im].astype(BF16), cmat_ref[st_im, ch],
                          preferred_element_type=F32))
        y = jnp.concatenate(tiles, axis=1)
        y = y + d_ref[...] * u_ref[sl, :].astype(F32)
        g = jax.nn.gelu(y)
        z = jnp.dot(g.astype(BF16), wglu_ref[...], preferred_element_type=F32)
        y = g * jax.nn.sigmoid(z)
        y_ref[sl, :] = _rms(y, g_ref[...]).astype(BF16)


def _ssm(u_tm, bmat, cmat, lre, lim, d_skip, w_glu, gain, batch, seq):
    assert batch == SUBLANES, "the S5 scan keeps one batch per sublane"
    rows = SSM_CHUNK * batch
    const = lambda i: (0, 0)
    return pl.pallas_call(
        _ssm_kernel,
        grid=(seq // SSM_CHUNK,),
        in_specs=[
            pl.BlockSpec((rows, SSM_WIDTH), lambda i: (i, 0)),
            pl.BlockSpec((SSM_WIDTH, 2 * SSM_HALF), const),
            pl.BlockSpec((2 * SSM_HALF, SSM_WIDTH), const),
            pl.BlockSpec((SUBLANES, SSM_HALF), const),
            pl.BlockSpec((SUBLANES, SSM_HALF), const),
            pl.BlockSpec((1, SSM_WIDTH), const),
            pl.BlockSpec((SSM_WIDTH, SSM_WIDTH), const),
            pl.BlockSpec((1, SSM_WIDTH), const),
        ],
        out_specs=pl.BlockSpec((rows, SSM_WIDTH), lambda i: (i, 0)),
        out_shape=jax.ShapeDtypeStruct((seq * batch, SSM_WIDTH), BF16),
        scratch_shapes=[
            pltpu.VMEM((rows, 2 * SSM_HALF), F32),
            pltpu.VMEM((SUBLANES, 2 * SSM_HALF), F32),
        ],
        compiler_params=_params(("arbitrary",)),
        name="ssm",
    )(u_tm, bmat, cmat, lre, lim, d_skip, w_glu, gain)


def _dilated_kernel(x_ref, g_ref, o_ref, m_ref, l_ref, acc_ref):
    seq = x_ref.shape[1]
    q_tiles = DIL_WIDTH // LANES
    heads_per_tile = LANES // HEAD_DIM
    m_ref[...] = jnp.full(m_ref.shape, NEG_INF, F32)
    l_ref[...] = jnp.zeros_like(l_ref)
    acc_ref[...] = jnp.zeros_like(acc_ref)
    row = lax.broadcasted_iota(jnp.int32, (ATT_BLOCK, ATT_BLOCK), 0)
    col = lax.broadcasted_iota(jnp.int32, (ATT_BLOCK, ATT_BLOCK), 1)
    lane = lax.broadcasted_iota(jnp.int32, (ATT_BLOCK, LANES), 1)
    cur_ok = col <= row
    spread = [(row == heads_per_tile * t + col // HEAD_DIM).astype(BF16) for t in range(q_tiles)]
    in_head = [col // HEAD_DIM == k for k in range(heads_per_tile)]
    assert heads_per_tile == 2

    def unit(dil, start, prev_start, prev_live):
        rows = pl.ds(start, ATT_BLOCK, stride=dil)
        load = lambda t, r: x_ref[t, r, :].astype(BF16)
        q = [load(t, rows) for t in range(q_tiles)]
        kc = [load(q_tiles + t, rows) for t in range(q_tiles)]
        vc = [load(2 * q_tiles + t, rows) for t in range(q_tiles)]
        has_prev = prev_start is not None
        if has_prev:
            prows = pl.ds(prev_start, ATT_BLOCK, stride=dil)
            kp = [load(q_tiles + t, prows) for t in range(q_tiles)]
            vp = [load(2 * q_tiles + t, prows) for t in range(q_tiles)]
            prev_ok = jnp.logical_and(col >= row, prev_live)
        tile_of = [h // heads_per_tile for h in range(DIL_HEADS)]
        qm = [jnp.where(in_head[h % heads_per_tile], q[tile_of[h]], jnp.zeros((), BF16))
              for h in range(DIL_HEADS)]
        sc_all = [_dot_nt(qm[h], kc[tile_of[h]]) for h in range(DIL_HEADS)]
        if has_prev:
            sp_all = [_dot_nt(qm[h], kp[tile_of[h]]) for h in range(DIL_HEADS)]
        pc_all, pp_all = [], []
        m_new_p = jnp.zeros((ATT_BLOCK, LANES), F32)
        l_new_p = jnp.zeros((ATT_BLOCK, LANES), F32)
        for h in range(DIL_HEADS):
            sc = jnp.where(cur_ok, sc_all[h], NEG_INF)
            m = jnp.max(sc, axis=-1, keepdims=True)
            if has_prev:
                sp = jnp.where(prev_ok, sp_all[h], NEG_INF)
                m = jnp.maximum(m, jnp.max(sp, axis=-1, keepdims=True))
            pc = jnp.exp(sc - m)
            den = jnp.sum(pc, axis=-1, keepdims=True)
            if has_prev:
                pp = jnp.exp(sp - m)
                den = den + jnp.sum(pp, axis=-1, keepdims=True)
                pp_all.append(pp.astype(BF16))
            pc_all.append(pc.astype(BF16))
            m_new_p = jnp.where(lane == h, m, m_new_p)
            l_new_p = jnp.where(lane == h, den, l_new_p)
        m_old = m_ref[rows, :]
        m_new = jnp.maximum(m_old, m_new_p)
        w_old = jnp.exp(m_old - m_new)
        w_pat = jnp.exp(m_new_p - m_new)
        m_ref[rows, :] = m_new
        l_ref[rows, :] = w_old * l_ref[rows, :] + w_pat * l_new_p
        for t in range(q_tiles):
            parts = []
            for h in range(heads_per_tile * t, heads_per_tile * (t + 1)):
                part = jnp.dot(pc_all[h], vc[t], preferred_element_type=F32)
                if has_prev:
                    part = part + jnp.dot(pp_all[h], vp[t], preferred_element_type=F32)
                parts.append(part)
            new = jnp.where(in_head[0], parts[0], parts[1])
            s_old = jnp.dot(w_old.astype(BF16), spread[t], preferred_element_type=F32)
            s_pat = jnp.dot(w_pat.astype(BF16), spread[t], preferred_element_type=F32)
            acc_ref[t, rows, :] = s_old * acc_ref[t, rows, :] + s_pat * new

    for dil in reversed(DIL_DILATIONS):
        sub_len = seq // dil
        nblk = sub_len // ATT_BLOCK
        units = dil * nblk

        def pair(u, _, dil=dil, nblk=nblk):
            for k in range(2):
                idx = 2 * u + k
                res, blk = idx % dil, idx // dil
                start = res + dil * ATT_BLOCK * blk
                if nblk == 1:
                    unit(dil, start, None, None)
                else:
                    prev = res + dil * ATT_BLOCK * jnp.maximum(blk - 1, 0)
                    unit(dil, start, prev, blk > 0)
            return 0

        lax.fori_loop(0, units // 2, pair, 0)

    def finish(blk, _):
        rows = pl.ds(pl.multiple_of(blk * ATT_BLOCK, ATT_BLOCK), ATT_BLOCK)
        inv = jnp.where(lane < DIL_HEADS, 1.0 / l_ref[rows, :], 0.0)
        inv_hi = inv.astype(BF16)
        inv_lo = (inv - inv_hi.astype(F32)).astype(BF16)
        ys = []
        for t in range(q_tiles):
            scale = (jnp.dot(inv_hi, spread[t], preferred_element_type=F32)
                     + jnp.dot(inv_lo, spread[t], preferred_element_type=F32))
            ys.append(acc_ref[t, rows, :] * scale)
        sq = sum(jnp.sum(y * y, axis=-1, keepdims=True) for y in ys)
        norm = lax.rsqrt(sq * (1.0 / DIL_WIDTH) + NORM_EPS)
        for t in range(q_tiles):
            cols = slice(t * LANES, (t + 1) * LANES)
            o_ref[rows, cols] = (ys[t] * norm * g_ref[:, cols]).astype(BF16)
        return 0

    lax.fori_loop(0, seq // ATT_BLOCK, finish, 0)


def _dilated_attention(qkv_tiles, gain, batch, seq):
    n = batch * seq
    n_tiles = 3 * DIL_WIDTH // LANES
    return pl.pallas_call(
        _dilated_kernel,
        grid=(batch,),
        in_specs=[
            pl.BlockSpec((n_tiles, seq, LANES), lambda b: (0, b, 0)),
            pl.BlockSpec((1, DIL_WIDTH), lambda b: (0, 0)),
        ],
        out_specs=pl.BlockSpec((seq, DIL_WIDTH), lambda b: (b, 0)),
        out_shape=jax.ShapeDtypeStruct((n, DIL_WIDTH), BF16),
        scratch_shapes=[
            pltpu.VMEM((seq, LANES), F32),
            pltpu.VMEM((seq, LANES), F32),
            pltpu.VMEM((DIL_WIDTH // LANES, seq, LANES), F32),
        ],
        compiler_params=_params(("arbitrary",)),
        name="dilated_attention",
    )(qkv_tiles, gain)


def _diff_kernel(out_scale, lam_ref, q_ref, k_ref, v_ref, g_ref, o_ref, qt_ref, vt_ref, ot_ref,
                 qm_ref, m_ref, l_ref, acc_ref):
    seq = q_ref.shape[0]
    nblk = seq // DIFF_BLOCK
    lam = lam_ref[0]
    for i in range(nblk):
        rows = slice(i * DIFF_BLOCK, (i + 1) * DIFF_BLOCK)
        qt_ref[i] = q_ref[rows, :].astype(F32).T.astype(BF16)
        vt_ref[i] = v_ref[rows, :].astype(F32).T.astype(BF16)
    krow = lax.broadcasted_iota(jnp.int32, (DIFF_BLOCK, DIFF_BLOCK), 0)
    qcol = lax.broadcasted_iota(jnp.int32, (DIFF_BLOCK, DIFF_BLOCK), 1)
    causal = krow <= qcol
    chan = lax.broadcasted_iota(jnp.int32, (DIFF_WIDTH, DIFF_BLOCK), 0)

    n_maps = 2 * DIFF_HEADS

    def attend(kj, mask):
        k0 = pl.multiple_of(kj * DIFF_BLOCK, DIFF_BLOCK)
        k = k_ref[pl.ds(k0, DIFF_BLOCK), :]
        scores = [jnp.dot(k, qm_ref[c], preferred_element_type=F32) for c in range(n_maps)]
        probs, alphas = [], []
        for c in range(n_maps):
            s = scores[c]
            if mask:
                s = jnp.where(causal, s, NEG_INF)
            m_old = m_ref[c]
            m_new = jnp.maximum(m_old, jnp.max(s, axis=0, keepdims=True))
            alpha = jnp.exp(m_old - m_new)
            p = jnp.exp(s - m_new)
            m_ref[c] = m_new
            l_ref[c] = alpha * l_ref[c] + jnp.sum(p, axis=0, keepdims=True)
            probs.append(p.astype(BF16))
            alphas.append(alpha)
        for c in range(n_maps):
            h = c // 2
            vt = vt_ref[kj, h * HEAD_DIM:(h + 1) * HEAD_DIM, :]
            acc_ref[c] = alphas[c] * acc_ref[c] + jnp.dot(vt, probs[c],
                                                          preferred_element_type=F32)

    def q_block(qi, _):
        qt = qt_ref[qi]
        for c in range(n_maps):
            lo = c * DIFF_QK_DIM
            qm_ref[c] = jnp.where(jnp.logical_and(chan >= lo, chan < lo + DIFF_QK_DIM), qt,
                                  jnp.zeros((), BF16))
        m_ref[...] = jnp.full(m_ref.shape, NEG_INF, F32)
        l_ref[...] = jnp.zeros_like(l_ref)
        acc_ref[...] = jnp.zeros_like(acc_ref)

        def k_block(kj, _):
            attend(kj, False)
            return 0

        lax.fori_loop(0, qi, k_block, 0)
        attend(qi, True)
        for h in range(DIFF_HEADS):
            o = (acc_ref[2 * h] / l_ref[2 * h]
                 - lam * (acc_ref[2 * h + 1] / l_ref[2 * h + 1]))
            o = o * lax.rsqrt(jnp.mean(o * o, axis=0, keepdims=True) + NORM_EPS)
            ot_ref[qi, h * HEAD_DIM:(h + 1) * HEAD_DIM, :] = o * (g_ref[...] * out_scale)
        return 0

    lax.fori_loop(0, nblk, q_block, 0)

    for i in range(nblk):
        o_ref[i * DIFF_BLOCK:(i + 1) * DIFF_BLOCK, :] = ot_ref[i].T.astype(BF16)


def _diff_attention(qkv, lam, subln_g_col, out_scale, batch, seq):
    n = batch * seq
    blk = (seq, DIFF_WIDTH)
    nblk = seq // DIFF_BLOCK
    return pl.pallas_call(
        functools.partial(_diff_kernel, out_scale),
        grid=(batch,),
        in_specs=[
            pl.BlockSpec(memory_space=pltpu.SMEM),
            pl.BlockSpec(blk, lambda b: (b, 0)),
            pl.BlockSpec(blk, lambda b: (b, 1)),
            pl.BlockSpec(blk, lambda b: (b, 2)),
            pl.BlockSpec((HEAD_DIM, 1), lambda b: (0, 0)),
        ],
        out_specs=pl.BlockSpec(blk, lambda b: (b, 0)),
        out_shape=jax.ShapeDtypeStruct((n, DIFF_WIDTH), BF16),
        scratch_shapes=[
            pltpu.VMEM((nblk, DIFF_WIDTH, DIFF_BLOCK), BF16),
            pltpu.VMEM((nblk, DIFF_WIDTH, DIFF_BLOCK), BF16),
            pltpu.VMEM((nblk, DIFF_WIDTH, DIFF_BLOCK), F32),
            pltpu.VMEM((2 * DIFF_HEADS, DIFF_WIDTH, DIFF_BLOCK), BF16),
            pltpu.VMEM((2 * DIFF_HEADS, 1, DIFF_BLOCK), F32),
            pltpu.VMEM((2 * DIFF_HEADS, 1, DIFF_BLOCK), F32),
            pltpu.VMEM((2 * DIFF_HEADS, HEAD_DIM, DIFF_BLOCK), F32),
        ],
        compiler_params=_params(("arbitrary",)),
        name="diff_attention",
    )(lam, qkv, qkv, qkv, subln_g_col)


def _out_proj_kernel(x_ref, ssm_ref, dil_ref, diff_ref, w_ref, y_ref):
    acc = jnp.dot(ssm_ref[...], w_ref[:SSM_WIDTH, :], preferred_element_type=F32)
    acc = acc + jnp.dot(dil_ref[...], w_ref[SSM_WIDTH:SSM_WIDTH + DIL_WIDTH, :],
                        preferred_element_type=F32)
    acc = acc + jnp.dot(diff_ref[...], w_ref[SSM_WIDTH + DIL_WIDTH:, :],
                        preferred_element_type=F32)
    y_ref[...] = x_ref[...] + acc


def _out_proj(x2, y_ssm_tm, y_dil, y_diff, w_out, batch, seq):
    n = batch * seq
    blocks_per_seq = seq // PROJ_TOKENS
    tok = lambda w: pl.BlockSpec((PROJ_TOKENS, w), lambda i: (i, 0))
    return pl.pallas_call(
        _out_proj_kernel,
        grid=(n // PROJ_TOKENS,),
        in_specs=[
            tok(D_MODEL),
            pl.BlockSpec((PROJ_TOKENS, SSM_WIDTH),
                         lambda i: (i % blocks_per_seq, i // blocks_per_seq)),
            tok(DIL_WIDTH),
            tok(DIFF_WIDTH),
            pl.BlockSpec((D_MODEL, D_MODEL), lambda i: (0, 0)),
        ],
        out_specs=tok(D_MODEL),
        out_shape=jax.ShapeDtypeStruct((n, D_MODEL), F32),
        compiler_params=_params(("arbitrary",)),
        name="out_proj",
    )(x2, y_ssm_tm, y_dil, y_diff, w_out)


def _sorting_network(n):
    pairs = []
    p = 1
    while p < n:
        k = p
        while k >= 1:
            for j in range(k % p, n - k, 2 * k):
                for i in range(min(k, n - j - k)):
                    if (i + j) // (2 * p) == (i + j + k) // (2 * p):
                        pairs.append((i + j, i + j + k))
            k //= 2
        p *= 2
    return pairs


_SORT16 = _sorting_network(PEER_TOPK)
_BITONIC16 = [(i, i + s) for s in (8, 4, 2, 1) for i in range(PEER_TOPK) if (i // s) % 2 == 0]


def _top16_desc(vals):
    v = list(vals)
    for a, b in _SORT16:
        hi, lo = jnp.maximum(v[a], v[b]), jnp.minimum(v[a], v[b])
        v[a], v[b] = hi, lo
    for shift in (4, 2, 1):
        other = [pltpu.roll(x, shift, 0) for x in v]
        v = [jnp.maximum(v[k], other[PEER_TOPK - 1 - k]) for k in range(PEER_TOPK)]
        for a, b in _BITONIC16:
            hi, lo = jnp.maximum(v[a], v[b]), jnp.minimum(v[a], v[b])
            v[a], v[b] = hi, lo
    return v


def _peer_gates(sa, sb):
    t = sa.shape[1]
    a_top = _top16_desc([sa[SUBLANES * v:SUBLANES * (v + 1), :] for v in range(PEER_KEYS // SUBLANES)])
    b_top = _top16_desc([sb[SUBLANES * v:SUBLANES * (v + 1), :] for v in range(PEER_KEYS // SUBLANES)])
    sub = lax.broadcasted_iota(jnp.int32, (SUBLANES, t), 0)

    def pack(rows):
        out = rows[0]
        for s in range(1, SUBLANES):
            out = jnp.where(sub == s, rows[s], out)
        return out

    b_lo, b_hi, a_hi = pack(b_top[:8]), pack(b_top[8:]), pack(a_top[8:])
    cands = [a_top[0] + b_lo, a_top[0] + b_hi]
    cands += [a_top[k] + b_lo for k in range(1, 8)]
    cands += [a_hi + b_top[0]]
    valid = [None, None] + [sub < (PEER_TOPK // (k + 1)) for k in range(1, 8)] + [None]
    cands = [c if ok is None else jnp.where(ok, c, NEG_INF) for c, ok in zip(cands, valid)]
    pad = jnp.full((SUBLANES, t), NEG_INF, F32)
    tau = _top16_desc(cands + [pad] * (PEER_TOPK - len(cands)))[PEER_TOPK - 1]
    top = a_top[0] + b_top[0]
    z = jnp.zeros((SUBLANES, t), F32)
    for c in cands:
        z = z + jnp.where(c >= tau, jnp.exp(c - top), 0.0)
    for shift in (4, 2, 1):
        z = z + pltpu.roll(z, shift, 0)
    inv_z = 1.0 / z
    last = PEER_TOPK - 1

    def search(test):
        total = None
        bits = []
        for level, weight in enumerate((8, 4, 2, 1)):
            leaves = [b_top[m] for m in range(weight - 1, last, 2 * weight)]
            for bit in reversed(bits):
                leaves = [jnp.where(bit, leaves[2 * n + 1], leaves[2 * n])
                          for n in range(len(leaves) // 2)]
            bit = test(leaves[0])
            bits.append(bit)
            term = jnp.where(bit, float(weight), 0.0)
            total = term if total is None else total + term
        return total

    wa, count, wb, rank = [], [], [], []
    for v in range(PEER_KEYS // SUBLANES):
        xa = sa[SUBLANES * v:SUBLANES * (v + 1), :]
        xb = sb[SUBLANES * v:SUBLANES * (v + 1), :]
        wa.append(jnp.where(xa >= a_top[last], jnp.exp(xa - a_top[0]), 0.0) * inv_z)
        wb.append(jnp.where(xb >= b_top[last], jnp.exp(xb - b_top[0]), 0.0))
        count.append(search(lambda b: xa + b >= tau)
                     + jnp.where(xa + b_top[last] >= tau, 1.0, 0.0))
        rank.append(search(lambda b: b >= xb) + jnp.where(b_top[last] >= xb, 1.0, 0.0))
    cat = lambda parts: jnp.concatenate(parts, axis=0)
    return cat(wa), cat(count), cat(wb), cat(rank)


def _gelu_tanh(a):
    c = math.sqrt(2.0 / math.pi)
    inner = a * (a * a * (c * 0.044715) + c)
    return (a * 0.5) * (jnp.tanh(inner) + 1.0)


def _peer_kernel(final, scale_ref, x_ref, g_ref, wq_ref, keys_ref, u_ref, vt_ref, fg_ref, y_ref,
                 h_ref, h8_ref, sc_ref, wa_ref, cnt_ref, wb_ref, rank_ref, a0_ref, a1_ref, p_ref,
                 acc_ref):
    h_scale, act_unscale, p_scale, out_unscale = (scale_ref[k] for k in range(4))
    s = pl.program_id(1)
    n_blocks = PEER_EXPERTS // PEER_EXPERT_BLOCK
    tb = x_ref.shape[0]
    n_chunks = tb // PEER_GATE_CHUNK
    rows_per_step = PEER_EXPERT_BLOCK // PEER_KEYS

    @pl.when(s == 0)
    def _():
        h = _rms(x_ref[...], g_ref[...])
        h_ref[...] = h.astype(BF16)
        h8_ref[...] = (h * h_scale).astype(F8)
        acc_ref[...] = jnp.zeros_like(acc_ref)

        chunks_per_stage = PEER_SCORE_TOKENS // PEER_GATE_CHUNK

        def stage(si, _):
            r0 = pl.multiple_of(si * PEER_SCORE_TOKENS, PEER_SCORE_TOKENS)
            hs = h_ref[pl.ds(r0, PEER_SCORE_TOKENS), :]
            qs = [jnp.dot(hs, wq_ref[hd], preferred_element_type=F32).astype(BF16)
                  for hd in range(PEER_HEADS)]
            for hd in range(PEER_HEADS):
                for half in range(2):
                    sc = _dot_nt(keys_ref[2 * hd + half],
                                 qs[hd][:, half * PEER_KEYS:(half + 1) * PEER_KEYS])
                    for c in range(chunks_per_stage):
                        sc_ref[c, 2 * hd + half] = sc[:, c * PEER_GATE_CHUNK:
                                                      (c + 1) * PEER_GATE_CHUNK]

            def unit(k, _):
                c = k // (PEER_HEADS // 2)
                ci = si * chunks_per_stage + c
                for hd_local in range(2):
                    hd = (k % (PEER_HEADS // 2)) * 2 + hd_local
                    wa, count, wb, rank = _peer_gates(sc_ref[c, 2 * hd], sc_ref[c, 2 * hd + 1])
                    wa_ref[ci, hd] = wa * p_scale
                    cnt_ref[ci, hd] = count
                    wb_ref[ci, hd] = wb.astype(BF16)
                    rank_ref[ci, hd] = rank.astype(BF16)
                return 0

            lax.fori_loop(0, chunks_per_stage * PEER_HEADS // 2, unit, 0)
            return 0

        lax.fori_loop(0, tb // PEER_SCORE_TOKENS, stage, 0)

    packed_rows = 2 * SUBLANES
    tiles = PEER_KEYS // packed_rows

    def step(par, activate=True, gate_prev=True):
        a_new, a_old = (a0_ref, a1_ref) if par == 0 else (a1_ref, a0_ref)
        gate_block = s - 1
        for t0 in range(0, tb, PEER_TOKEN_TILE):
            tok = slice(t0, t0 + PEER_TOKEN_TILE)
            if activate:
                a_new[:, tok] = (_dot_nt(u_ref[...], h8_ref[tok, :]) * act_unscale).astype(BF16)
            if not gate_prev:
                continue
            for ci in range(t0 // PEER_GATE_CHUNK, (t0 + PEER_TOKEN_TILE) // PEER_GATE_CHUNK):
                cols = slice(ci * PEER_GATE_CHUNK, (ci + 1) * PEER_GATE_CHUNK)
                for il in range(rows_per_step):
                    i = gate_block * rows_per_step + il
                    rows = slice(il * PEER_KEYS, (il + 1) * PEER_KEYS)
                    gate = jnp.zeros((tiles, packed_rows, PEER_GATE_CHUNK), BF16)
                    for hd in range(PEER_HEADS):
                        row = lambda ref: jnp.broadcast_to(
                            ref[ci, hd, pl.ds(i, 1), :],
                            (packed_rows, PEER_GATE_CHUNK)).astype(BF16)[None]
                        rank = rank_ref[ci, hd].reshape(tiles, packed_rows, PEER_GATE_CHUNK)
                        wb = wb_ref[ci, hd].reshape(tiles, packed_rows, PEER_GATE_CHUNK)
                        gate = gate + jnp.where(rank <= row(cnt_ref), wb * row(wa_ref),
                                                jnp.zeros((), BF16))
                    gate = gate.reshape(PEER_KEYS, PEER_GATE_CHUNK)
                    p_ref[rows, cols] = (gate * _gelu_tanh(a_old[rows, cols])).astype(F8)
            acc_ref[:, tok] += jnp.dot(vt_ref[...], p_ref[:, tok], preferred_element_type=F32)

    assert n_blocks % 2 == 0
    last = n_blocks
    inner = jnp.logical_and(s > 0, s < last)

    @pl.when(s == 0)
    def _():
        step(0, gate_prev=False)

    @pl.when(jnp.logical_and(inner, s % 2 == 0))
    def _():
        step(0)

    @pl.when(jnp.logical_and(inner, s % 2 == 1))
    def _():
        step(1)

    @pl.when(s == last)
    def _():
        step(0, activate=False)
        y = x_ref[...] + acc_ref[...].T * out_unscale
        if final:
            y = _rms(y, fg_ref[...])
        y_ref[...] = y


def _peer(x2, gain, wq_heads, keys, u_q, vt_q, scales, final_gain, final):
    n = x2.shape[0]
    n_chunks = PEER_TOKENS // PEER_GATE_CHUNK
    n_blocks = PEER_EXPERTS // PEER_EXPERT_BLOCK
    gate_shape = (n_chunks, PEER_HEADS, PEER_KEYS, PEER_GATE_CHUNK)
    act_block = lambda s: jnp.minimum(s, n_blocks - 1)
    out_block = lambda s: jnp.maximum(s - 1, 0)
    once = pl.Buffered(1)
    return pl.pallas_call(
        functools.partial(_peer_kernel, final),
        grid=(n // PEER_TOKENS, n_blocks + 1),
        in_specs=[
            pl.BlockSpec(memory_space=pltpu.SMEM),
            pl.BlockSpec((PEER_TOKENS, D_MODEL), lambda t, s: (t, 0), pipeline_mode=once),
            pl.BlockSpec((1, D_MODEL), lambda t, s: (0, 0)),
            pl.BlockSpec((PEER_HEADS, D_MODEL, 2 * PEER_KEYS), lambda t, s: (0, 0, 0),
                         pipeline_mode=once),
            pl.BlockSpec((2 * PEER_HEADS, PEER_KEYS, PEER_KEYS), lambda t, s: (0, 0, 0)),
            pl.BlockSpec((PEER_EXPERT_BLOCK, D_MODEL), lambda t, s: (act_block(s), 0)),
            pl.BlockSpec((None, D_MODEL, PEER_EXPERT_BLOCK), lambda t, s: (out_block(s), 0, 0)),
            pl.BlockSpec((1, D_MODEL), lambda t, s: (0, 0)),
        ],
        out_specs=pl.BlockSpec((PEER_TOKENS, D_MODEL), lambda t, s: (t, 0)),
        out_shape=jax.ShapeDtypeStruct((n, D_MODEL), F32),
        scratch_shapes=[
            pltpu.VMEM((PEER_TOKENS, D_MODEL), BF16),
            pltpu.VMEM((PEER_TOKENS, D_MODEL), F8),
            pltpu.VMEM((PEER_SCORE_TOKENS // PEER_GATE_CHUNK, 2 * PEER_HEADS, PEER_KEYS,
                        PEER_GATE_CHUNK), F32),
            pltpu.VMEM(gate_shape, F32),
            pltpu.VMEM(gate_shape, F32),
            pltpu.VMEM(gate_shape, BF16),
            pltpu.VMEM(gate_shape, BF16),
            pltpu.VMEM((PEER_EXPERT_BLOCK, PEER_TOKENS), BF16),
            pltpu.VMEM((PEER_EXPERT_BLOCK, PEER_TOKENS), BF16),
            pltpu.VMEM((PEER_EXPERT_BLOCK, PEER_TOKENS), F8),
            pltpu.VMEM((D_MODEL, PEER_TOKENS), F32),
        ],
        compiler_params=_params(("arbitrary", "arbitrary")),
        name="peer",
    )(scales, x2, gain, wq_heads, keys, u_q, vt_q, final_gain)


def _ssm_matrices(lam_re, lam_im, log_step, b_re, b_im, c_re, c_im):
    lam = lax.complex(lam_re, lam_im)
    step = jnp.exp(log_step)[:, None]
    lam_bar = jnp.exp(lam * step)
    b_bar = ((lam_bar - 1.0) / lam)[:, :, None] * lax.complex(b_re, b_im)
    eye = jnp.eye(SSM_GROUPS, dtype=F32)

    def embed_in(m):
        return jnp.einsum("gpc,gh->gchp", m, eye).reshape(SSM_WIDTH, SSM_HALF)

    def embed_out(m):
        return jnp.einsum("gcp,gh->gphc", m, eye).reshape(SSM_HALF, SSM_WIDTH)

    bmat = jnp.concatenate([embed_in(jnp.real(b_bar)), embed_in(jnp.imag(b_bar))], axis=1)
    cmat = jnp.concatenate([embed_out(c_re), embed_out(-c_im)], axis=0)
    lre = jnp.broadcast_to(jnp.real(lam_bar).reshape(1, SSM_HALF), (SUBLANES, SSM_HALF))
    lim = jnp.broadcast_to(jnp.imag(lam_bar).reshape(1, SSM_HALF), (SUBLANES, SSM_HALF))
    return bmat.astype(BF16), cmat.astype(BF16), lre, lim


def _pow2_scale(bound):
    return jnp.exp2(jnp.floor(jnp.log2(FP8_LIMIT / jnp.maximum(bound, 1e-30))))


def _peer_tables(u_exp, v_exp, gain):
    h_bound = math.sqrt(D_MODEL) * jnp.max(jnp.abs(gain))
    u_norm = jnp.sqrt(jnp.max(jnp.sum(u_exp * u_exp, axis=1)))
    u_scale = _pow2_scale(u_norm)
    v_scale = _pow2_scale(jnp.max(jnp.abs(v_exp)))
    h_scale = _pow2_scale(h_bound)
    p_scale = _pow2_scale(u_norm * h_bound)
    u_q = (u_exp * u_scale).astype(F8)
    blocks = v_exp.reshape(PEER_EXPERTS // PEER_EXPERT_BLOCK, PEER_EXPERT_BLOCK, D_MODEL)
    vt_q = (blocks.transpose(0, 2, 1) * v_scale).astype(F8)
    scales = jnp.stack([h_scale, 1.0 / (h_scale * u_scale), p_scale, 1.0 / (p_scale * v_scale)])
    return u_q, vt_q, scales.astype(F32)


def _in_proj_col_scale():
    s = jnp.ones((IN_WIDTH,), F32)
    s = s.at[SSM_WIDTH:SSM_WIDTH + DIL_WIDTH].set(HEAD_DIM ** -0.5)
    d0 = SSM_WIDTH + 3 * DIL_WIDTH
    s = s.at[d0:d0 + DIFF_WIDTH].set(DIFF_QK_DIM ** -0.5)
    return s.reshape(1, IN_WIDTH)


def kernel(x, norm1_g, w_in, ssm_lam_re, ssm_lam_im, ssm_log_step, ssm_b_re, ssm_b_im, ssm_c_re, ssm_c_im, ssm_d, ssm_w_glu, ssm_norm_g, dil_norm_g, diff_lam_q1, diff_lam_k1, diff_lam_q2, diff_lam_k2, diff_subln_g, w_out, norm2_g, peer_w_query, peer_sub_keys, peer_u, peer_v, final_norm_g):
    batch, seq, _ = x.shape
    n = batch * seq
    depth = w_in.shape[0]
    x2 = x.reshape(n, D_MODEL)
    col_scale = _in_proj_col_scale()
    row = lambda v: v.reshape(1, -1)
    for layer in range(depth):
        lambda_init = 0.8 - 0.6 * math.exp(-0.3 * layer)
        u_tm, dil_qkv, diff_qkv = _in_proj(x2, row(norm1_g[layer]), w_in[layer].astype(BF16),
                                           col_scale, batch, seq)
        bmat, cmat, lre, lim = _ssm_matrices(
            ssm_lam_re[layer], ssm_lam_im[layer], ssm_log_step[layer], ssm_b_re[layer],
            ssm_b_im[layer], ssm_c_re[layer], ssm_c_im[layer])
        y_ssm = _ssm(u_tm.reshape(seq * batch, SSM_WIDTH), bmat, cmat, lre, lim,
                     row(ssm_d[layer]), ssm_w_glu[layer].astype(BF16), row(ssm_norm_g[layer]),
                     batch, seq)
        y_dil = _dilated_attention(dil_qkv, row(dil_norm_g[layer]), batch, seq)
        lam = (jnp.exp(jnp.sum(diff_lam_q1[layer] * diff_lam_k1[layer]))
               - jnp.exp(jnp.sum(diff_lam_q2[layer] * diff_lam_k2[layer])) + lambda_init)
        y_diff = _diff_attention(diff_qkv, lam.reshape(1), diff_subln_g[layer].reshape(-1, 1),
                                 1.0 - lambda_init, batch, seq)
        x2 = _out_proj(x2, y_ssm.reshape(seq, batch * SSM_WIDTH), y_dil, y_diff,
                       w_out[layer].astype(BF16), batch, seq)
        wq_heads = peer_w_query[layer].reshape(D_MODEL, PEER_HEADS, 2 * PEER_KEYS)
        wq_heads = wq_heads.transpose(1, 0, 2).astype(BF16)
        keys = peer_sub_keys[layer].reshape(2 * PEER_HEADS, PEER_KEYS, PEER_KEYS).astype(BF16)
        u_q, vt_q, scales = _peer_tables(peer_u[layer], peer_v[layer], norm2_g[layer])
        x2 = _peer(x2, row(norm2_g[layer]), wq_heads, keys, u_q, vt_q, scales,
                   row(final_norm_g), layer == depth - 1)
    return x2.reshape(batch, seq, D_MODEL)
```

```python
import functools
import math

import jax
import jax.numpy as jnp
from jax import lax
from jax.experimental import pallas as pl
from jax.experimental.pallas import tpu as pltpu

F32 = jnp.float32
BF16 = jnp.bfloat16
F8 = jnp.float8_e4m3fn
FP8_LIMIT = 240.0

D_MODEL = 1024
HEAD_DIM = 64
SSM_WIDTH = 384
SSM_GROUP_CH = 16
SSM_GROUPS = 24
SSM_STATE = 64
SSM_HALF = SSM_GROUPS * SSM_STATE
DIL_WIDTH = 384
DIL_HEADS = 6
DIL_DILATIONS = (1, 4, 16)
DIFF_WIDTH = 256
DIFF_HEADS = 4
DIFF_QK_DIM = 32
IN_WIDTH = 2304
PEER_HEADS = 8
PEER_KEYS = 128
PEER_TOPK = 16
PEER_EXPERTS = PEER_KEYS * PEER_KEYS
NORM_EPS = 1e-6

LANES = 128
SUBLANES = 8
VMEM_LIMIT_BYTES = 56 * 1024 * 1024

PROJ_TOKENS = 1024
SSM_CHUNK = 128
ATT_BLOCK = 128
DIFF_BLOCK = 256
PEER_TOKENS = 1024
PEER_GATE_CHUNK = LANES
PEER_EXPERT_BLOCK = 1024
PEER_TOKEN_TILE = 256
PEER_SCORE_TOKENS = 256

NEG_INF = float("-inf")


def _rms(x, gain):
    return x * lax.rsqrt(jnp.mean(x * x, axis=-1, keepdims=True) + NORM_EPS) * gain


def _dot_nt(a, b):
    return lax.dot_general(a, b, (((1,), (1,)), ((), ())), preferred_element_type=F32)


def _params(semantics):
    return pltpu.CompilerParams(dimension_semantics=semantics, vmem_limit_bytes=VMEM_LIMIT_BYTES)


def _in_proj_kernel(x_ref, g_ref, w_ref, scale_ref, u_ref, dil_ref, diff_ref):
    h = _rms(x_ref[...], g_ref[...])
    p = jnp.dot(h.astype(BF16), w_ref[...], preferred_element_type=F32) * scale_ref[...]
    u_ref[...] = p[:, :SSM_WIDTH].astype(BF16)
    for t in range(dil_ref.shape[0]):
        dil_ref[t] = p[:, SSM_WIDTH + t * LANES:SSM_WIDTH + (t + 1) * LANES]
    diff_ref[...] = p[:, SSM_WIDTH + 3 * DIL_WIDTH:].astype(BF16)


def _in_proj(x2, gain, w_in, col_scale, batch, seq):
    n = batch * seq
    blocks_per_seq = seq // PROJ_TOKENS
    return pl.pallas_call(
        _in_proj_kernel,
        grid=(n // PROJ_TOKENS,),
        in_specs=[
            pl.BlockSpec((PROJ_TOKENS, D_MODEL), lambda i: (i, 0)),
            pl.BlockSpec((1, D_MODEL), lambda i: (0, 0)),
            pl.BlockSpec((D_MODEL, IN_WIDTH), lambda i: (0, 0)),
            pl.BlockSpec((1, IN_WIDTH), lambda i: (0, 0)),
        ],
        out_specs=[
            pl.BlockSpec((PROJ_TOKENS, SSM_WIDTH),
                         lambda i: (i % blocks_per_seq, i // blocks_per_seq)),
            pl.BlockSpec((3 * DIL_WIDTH // LANES, PROJ_TOKENS, LANES), lambda i: (0, i, 0)),
            pl.BlockSpec((PROJ_TOKENS, 3 * DIFF_WIDTH), lambda i: (i, 0)),
        ],
        out_shape=[
            jax.ShapeDtypeStruct((seq, batch * SSM_WIDTH), BF16),
            jax.ShapeDtypeStruct((3 * DIL_WIDTH // LANES, n, LANES), F32),
            jax.ShapeDtypeStruct((n, 3 * DIFF_WIDTH), BF16),
        ],
        compiler_params=_params(("arbitrary",)),
        name="in_proj",
    )(x2, gain, w_in, col_scale)


def _ssm_kernel(u_ref, bmat_ref, cmat_ref, lre_ref, lim_ref, d_ref, wglu_ref, g_ref,
                y_ref, s_ref, state_ref):
    rows = u_ref.shape[0]
    sub = 256

    @pl.when(pl.program_id(0) == 0)
    def _():
        state_ref[...] = jnp.zeros_like(state_ref)

    span = (LANES // SSM_GROUP_CH) * SSM_STATE
    blocks = [(slice(k * LANES, (k + 1) * LANES), slice(off + k * span, off + (k + 1) * span))
              for k in range(SSM_WIDTH // LANES) for off in (0, SSM_HALF)]
    for c in range(rows // sub):
        sl = slice(c * sub, (c + 1) * sub)
        for ch, st in blocks:
            s_ref[sl, st] = jnp.dot(u_ref[sl, ch], bmat_ref[ch, st], preferred_element_type=F32)

    lre = lre_ref[...]
    lim = lim_ref[...]

    def step(t, carry):
        sre, sim = carry
        r0 = pl.multiple_of(t * SUBLANES, SUBLANES)
        bre = s_ref[pl.ds(r0, SUBLANES), :SSM_HALF]
        bim = s_ref[pl.ds(r0, SUBLANES), SSM_HALF:]
        nre = lre * sre - lim * sim + bre
        nim = lre * sim + lim * sre + bim
        s_ref[pl.ds(r0, SUBLANES), :SSM_HALF] = nre
        s_ref[pl.ds(r0, SUBLANES), SSM_HALF:] = nim
        return nre, nim

    sre, sim = lax.fori_loop(0, rows // SUBLANES, step,
                             (state_ref[:, :SSM_HALF], state_ref[:, SSM_HALF:]), unroll=2)
    state_ref[:, :SSM_HALF] = sre
    state_ref[:, SSM_HALF:] = sim

    for c in range(rows // sub):
        sl = slice(c * sub, (c + 1) * sub)
        tiles = []
        for k in range(SSM_WIDTH // LANES):
            (ch, st_re), (_, st_im) = blocks[2 * k], blocks[2 * k + 1]
            tiles.append(
                jnp.dot(s_ref[sl, st_re].astype(BF16), cmat_ref[st_re, ch],
                        preferred_element_type=F32)
                + jnp.dot(s_ref[sl, st_im].astype(BF16), cmat_ref[st_im, ch],
                          preferred_element_type=F32))
        y = jnp.concatenate(tiles, axis=1)
        y = y + d_ref[...] * u_ref[sl, :].astype(F32)
        g = jax.nn.gelu(y)
        z = jnp.dot(g.astype(BF16), wglu_ref[...], preferred_element_type=F32)
        y = g * jax.nn.sigmoid(z)
        y_ref[sl, :] = _rms(y, g_ref[...]).astype(BF16)


def _ssm(u_tm, bmat, cmat, lre, lim, d_skip, w_glu, gain, batch, seq):
    assert batch == SUBLANES, "the S5 scan keeps one batch per sublane"
    rows = SSM_CHUNK * batch
    const = lambda i: (0, 0)
    return pl.pallas_call(
        _ssm_kernel,
        grid=(seq // SSM_CHUNK,),
        in_specs=[
            pl.BlockSpec((rows, SSM_WIDTH), lambda i: (i, 0)),
            pl.BlockSpec((SSM_WIDTH, 2 * SSM_HALF), const),
            pl.BlockSpec((2 * SSM_HALF, SSM_WIDTH), const),
            pl.BlockSpec((SUBLANES, SSM_HALF), const),
            pl.BlockSpec((SUBLANES, SSM_HALF), const),
            pl.BlockSpec((1, SSM_WIDTH), const),
            pl.BlockSpec((SSM_WIDTH, SSM_WIDTH), const),
            pl.BlockSpec((1, SSM_WIDTH), const),
        ],
        out_specs=pl.BlockSpec((rows, SSM_WIDTH), lambda i: (i, 0)),
        out_shape=jax.ShapeDtypeStruct((seq * batch, SSM_WIDTH), BF16),
        scratch_shapes=[
            pltpu.VMEM((rows, 2 * SSM_HALF), F32),
            pltpu.VMEM((SUBLANES, 2 * SSM_HALF), F32),
        ],
        compiler_params=_params(("arbitrary",)),
        name="ssm",
    )(u_tm, bmat, cmat, lre, lim, d_skip, w_glu, gain)


def _dilated_kernel(x_ref, g_ref, o_ref, m_ref, l_ref, acc_ref):
    seq = x_ref.shape[1]
    q_tiles = DIL_WIDTH // LANES
    heads_per_tile = LANES // HEAD_DIM
    m_ref[...] = jnp.full(m_ref.shape, NEG_INF, F32)
    l_ref[...] = jnp.zeros_like(l_ref)
    acc_ref[...] = jnp.zeros_like(acc_ref)
    row = lax.broadcasted_iota(jnp.int32, (ATT_BLOCK, ATT_BLOCK), 0)
    col = lax.broadcasted_iota(jnp.int32, (ATT_BLOCK, ATT_BLOCK), 1)
    lane = lax.broadcasted_iota(jnp.int32, (ATT_BLOCK, LANES), 1)
    cur_ok = col <= row
    spread = [(row == heads_per_tile * t + col // HEAD_DIM).astype(BF16) for t in range(q_tiles)]
    in_head = [col // HEAD_DIM == k for k in range(heads_per_tile)]
    assert heads_per_tile == 2

    def unit(dil, start, prev_start, prev_live):
        rows = pl.ds(start, ATT_BLOCK, stride=dil)
        load = lambda t, r: x_ref[t, r, :].astype(BF16)
        q = [load(t, rows) for t in range(q_tiles)]
        kc = [load(q_tiles + t, rows) for t in range(q_tiles)]
        vc = [load(2 * q_tiles + t, rows) for t in range(q_tiles)]
        has_prev = prev_start is not None
        if has_prev:
            prows = pl.ds(prev_start, ATT_BLOCK, stride=dil)
            kp = [load(q_tiles + t, prows) for t in range(q_tiles)]
            vp = [load(2 * q_tiles + t, prows) for t in range(q_tiles)]
            prev_ok = jnp.logical_and(col >= row, prev_live)
        tile_of = [h // heads_per_tile for h in range(DIL_HEADS)]
        qm = [jnp.where(in_head[h % heads_per_tile], q[tile_of[h]], jnp.zeros((), BF16))
              for h in range(DIL_HEADS)]
        sc_all = [_dot_nt(qm[h], kc[tile_of[h]]) for h in range(DIL_HEADS)]
        if has_prev:
            sp_all = [_dot_nt(qm[h], kp[tile_of[h]]) for h in range(DIL_HEADS)]
        pc_all, pp_all = [], []
        m_new_p = jnp.zeros((ATT_BLOCK, LANES), F32)
        l_new_p = jnp.zeros((ATT_BLOCK, LANES), F32)
        for h in range(DIL_HEADS):
            sc = jnp.where(cur_ok, sc_all[h], NEG_INF)
            m = jnp.max(sc, axis=-1, keepdims=True)
            if has_prev:
                sp = jnp.where(prev_ok, sp_all[h], NEG_INF)
                m = jnp.maximum(m, jnp.max(sp, axis=-1, keepdims=True))
            pc = jnp.exp(sc - m)
            den = jnp.sum(pc, axis=-1, keepdims=True)
            if has_prev:
                pp = jnp.exp(sp - m)
                den = den + jnp.sum(pp, axis=-1, keepdims=True)
                pp_all.append(pp.astype(BF16))
            pc_all.append(pc.astype(BF16))
            m_new_p = jnp.where(lane == h, m, m_new_p)
            l_new_p = jnp.where(lane == h, den, l_new_p)
        m_old = m_ref[rows, :]
        m_new = jnp.maximum(m_old, m_new_p)
        w_old = jnp.exp(m_old - m_new)
        w_pat = jnp.exp(m_new_p - m_new)
        m_ref[rows, :] = m_new
        l_ref[rows, :] = w_old * l_ref[rows, :] + w_pat * l_new_p
        for t in range(q_tiles):
            parts = []
            for h in range(heads_per_tile * t, heads_per_tile * (t + 1)):
                part = jnp.dot(pc_all[h], vc[t], preferred_element_type=F32)
                if has_prev:
                    part = part + jnp.dot(pp_all[h], vp[t], preferred_element_type=F32)
                parts.append(part)
            new = jnp.where(in_head[0], parts[0], parts[1])
            s_old = jnp.dot(w_old.astype(BF16), spread[t], preferred_element_type=F32)
            s_pat = jnp.dot(w_pat.astype(BF16), spread[t], preferred_element_type=F32)
            acc_ref[t, rows, :] = s_old * acc_ref[t, rows, :] + s_pat * new

    for dil in reversed(DIL_DILATIONS):
        sub_len = seq // dil
        nblk = sub_len // ATT_BLOCK
        units = dil * nblk

        def pair(u, _, dil=dil, nblk=nblk):
            for k in range(2):
                idx = 2 * u + k
                res, blk = idx % dil, idx // dil
                start = res + dil * ATT_BLOCK * blk
                if nblk == 1:
                    unit(dil, start, None, None)
                else:
                    prev = res + dil * ATT_BLOCK * jnp.maximum(blk - 1, 0)
                    unit(dil, start, prev, blk > 0)
            return 0

        lax.fori_loop(0, units // 2, pair, 0)

    def finish(blk, _):
        rows = pl.ds(pl.multiple_of(blk * ATT_BLOCK, ATT_BLOCK), ATT_BLOCK)
        inv = jnp.where(lane < DIL_HEADS, 1.0 / l_ref[rows, :], 0.0)
        inv_hi = inv.astype(BF16)
        inv_lo = (inv - inv_hi.astype(F32)).astype(BF16)
        ys = []
        for t in range(q_tiles):
            scale = (jnp.dot(inv_hi, spread[t], preferred_element_type=F32)
                     + jnp.dot(inv_lo, spread[t], preferred_element_type=F32))
            ys.append(acc_ref[t, rows, :] * scale)
        sq = sum(jnp.sum(y * y, axis=-1, keepdims=True) for y in ys)
        norm = lax.rsqrt(sq * (1.0 / DIL_WIDTH) + NORM_EPS)
        for t in range(q_tiles):
            cols = slice(t * LANES, (t + 1) * LANES)
            o_ref[rows, cols] = (ys[t] * norm * g_ref[:, cols]).astype(BF16)
        return 0

    lax.fori_loop(0, seq // ATT_BLOCK, finish, 0)


def _dilated_attention(qkv_tiles, gain, batch, seq):
    n = batch * seq
    n_tiles = 3 * DIL_WIDTH // LANES
    return pl.pallas_call(
        _dilated_kernel,
        grid=(batch,),
        in_specs=[
            pl.BlockSpec((n_tiles, seq, LANES), lambda b: (0, b, 0)),
            pl.BlockSpec((1, DIL_WIDTH), lambda b: (0, 0)),
        ],
        out_specs=pl.BlockSpec((seq, DIL_WIDTH), lambda b: (b, 0)),
        out_shape=jax.ShapeDtypeStruct((n, DIL_WIDTH), BF16),
        scratch_shapes=[
            pltpu.VMEM((seq, LANES), F32),
            pltpu.VMEM((seq, LANES), F32),
            pltpu.VMEM((DIL_WIDTH // LANES, seq, LANES), F32),
        ],
        compiler_params=_params(("arbitrary",)),
        name="dilated_attention",
    )(qkv_tiles, gain)


def _diff_kernel(out_scale, lam_ref, q_ref, k_ref, v_ref, g_ref, o_ref, qt_ref, vt_ref, ot_ref,
                 qm_ref, m_ref, l_ref, acc_ref):
    seq = q_ref.shape[0]
    nblk = seq // DIFF_BLOCK
    lam = lam_ref[0]
    for i in range(nblk):
        rows = slice(i * DIFF_BLOCK, (i + 1) * DIFF_BLOCK)
        qt_ref[i] = q_ref[rows, :].astype(F32).T.astype(BF16)
        vt_ref[i] = v_ref[rows, :].astype(F32).T.astype(BF16)
    krow = lax.broadcasted_iota(jnp.int32, (DIFF_BLOCK, DIFF_BLOCK), 0)
    qcol = lax.broadcasted_iota(jnp.int32, (DIFF_BLOCK, DIFF_BLOCK), 1)
    causal = krow <= qcol
    chan = lax.broadcasted_iota(jnp.int32, (DIFF_WIDTH, DIFF_BLOCK), 0)

    n_maps = 2 * DIFF_HEADS

    def attend(kj, mask):
        k0 = pl.multiple_of(kj * DIFF_BLOCK, DIFF_BLOCK)
        k = k_ref[pl.ds(k0, DIFF_BLOCK), :]
        scores = [jnp.dot(k, qm_ref[c], preferred_element_type=F32) for c in range(n_maps)]
        probs, alphas = [], []
        for c in range(n_maps):
            s = scores[c]
            if mask:
                s = jnp.where(causal, s, NEG_INF)
            m_old = m_ref[c]
            m_new = jnp.maximum(m_old, jnp.max(s, axis=0, keepdims=True))
            alpha = jnp.exp(m_old - m_new)
            p = jnp.exp(s - m_new)
            m_ref[c] = m_new
            l_ref[c] = alpha * l_ref[c] + jnp.sum(p, axis=0, keepdims=True)
            probs.append(p.astype(BF16))
            alphas.append(alpha)
        for c in range(n_maps):
            h = c // 2
            vt = vt_ref[kj, h * HEAD_DIM:(h + 1) * HEAD_DIM, :]
            acc_ref[c] = alphas[c] * acc_ref[c] + jnp.dot(vt, probs[c],
                                                          preferred_element_type=F32)

    def q_block(qi, _):
        qt = qt_ref[qi]
        for c in range(n_maps):
            lo = c * DIFF_QK_DIM
            qm_ref[c] = jnp.where(jnp.logical_and(chan >= lo, chan < lo + DIFF_QK_DIM), qt,
                                  jnp.zeros((), BF16))
        m_ref[...] = jnp.full(m_ref.shape, NEG_INF, F32)
        l_ref[...] = jnp.zeros_like(l_ref)
        acc_ref[...] = jnp.zeros_like(acc_ref)

        def k_block(kj, _):
            attend(kj, False)
            return 0

        lax.fori_loop(0, qi, k_block, 0)
        attend(qi, True)
        for h in range(DIFF_HEADS):
            o = (acc_ref[2 * h] / l_ref[2 * h]
                 - lam * (acc_ref[2 * h + 1] / l_ref[2 * h + 1]))
            o = o * lax.rsqrt(jnp.mean(o * o, axis=0, keepdims=True) + NORM_EPS)
            ot_ref[qi, h * HEAD_DIM:(h + 1) * HEAD_DIM, :] = o * (g_ref[...] * out_scale)
        return 0

    lax.fori_loop(0, nblk, q_block, 0)

    for i in range(nblk):
        o_ref[i * DIFF_BLOCK:(i + 1) * DIFF_BLOCK, :] = ot_ref[i].T.astype(BF16)


def _diff_attention(qkv, lam, subln_g_col, out_scale, batch, seq):
    n = batch * seq
    blk = (seq, DIFF_WIDTH)
    nblk = seq // DIFF_BLOCK
    return pl.pallas_call(
        functools.partial(_diff_kernel, out_scale),
        grid=(batch,),
        in_specs=[
            pl.BlockSpec(memory_space=pltpu.SMEM),
            pl.BlockSpec(blk, lambda b: (b, 0)),
            pl.BlockSpec(blk, lambda b: (b, 1)),
            pl.BlockSpec(blk, lambda b: (b, 2)),
            pl.BlockSpec((HEAD_DIM, 1), lambda b: (0, 0)),
        ],
        out_specs=pl.BlockSpec(blk, lambda b: (b, 0)),
        out_shape=jax.ShapeDtypeStruct((n, DIFF_WIDTH), BF16),
        scratch_shapes=[
            pltpu.VMEM((nblk, DIFF_WIDTH, DIFF_BLOCK), BF16),
            pltpu.VMEM((nblk, DIFF_WIDTH, DIFF_BLOCK), BF16),
            pltpu.VMEM((nblk, DIFF_WIDTH, DIFF_BLOCK), F32),
            pltpu.VMEM((2 * DIFF_HEADS, DIFF_WIDTH, DIFF_BLOCK), BF16),
            pltpu.VMEM((2 * DIFF_HEADS, 1, DIFF_BLOCK), F32),
            pltpu.VMEM((2 * DIFF_HEADS, 1, DIFF_BLOCK), F32),
            pltpu.VMEM((2 * DIFF_HEADS, HEAD_DIM, DIFF_BLOCK), F32),
        ],
        compiler_params=_params(("arbitrary",)),
        name="diff_attention",
    )(lam, qkv, qkv, qkv, subln_g_col)


def _out_proj_kernel(x_ref, ssm_ref, dil_ref, diff_ref, w_ref, y_ref):
    acc = jnp.dot(ssm_ref[...], w_ref[:SSM_WIDTH, :], preferred_element_type=F32)
    acc = acc + jnp.dot(dil_ref[...], w_ref[SSM_WIDTH:SSM_WIDTH + DIL_WIDTH, :],
                        preferred_element_type=F32)
    acc = acc + jnp.dot(diff_ref[...], w_ref[SSM_WIDTH + DIL_WIDTH:, :],
                        preferred_element_type=F32)
    y_ref[...] = x_ref[...] + acc


def _out_proj(x2, y_ssm_tm, y_dil, y_diff, w_out, batch, seq):
    n = batch * seq
    blocks_per_seq = seq // PROJ_TOKENS
    tok = lambda w: pl.BlockSpec((PROJ_TOKENS, w), lambda i: (i, 0))
    return pl.pallas_call(
        _out_proj_kernel,
        grid=(n // PROJ_TOKENS,),
        in_specs=[
            tok(D_MODEL),
            pl.BlockSpec((PROJ_TOKENS, SSM_WIDTH),
                         lambda i: (i % blocks_per_seq, i // blocks_per_seq)),
            tok(DIL_WIDTH),
            tok(DIFF_WIDTH),
            pl.BlockSpec((D_MODEL, D_MODEL), lambda i: (0, 0)),
        ],
        out_specs=tok(D_MODEL),
        out_shape=jax.ShapeDtypeStruct((n, D_MODEL), F32),
        compiler_params=_params(("arbitrary",)),
        name="out_proj",
    )(x2, y_ssm_tm, y_dil, y_diff, w_out)


def _sorting_network(n):
    pairs = []
    p = 1
    while p < n:
        k = p
        while k >= 1:
            for j in range(k % p, n - k, 2 * k):
                for i in range(min(k, n - j - k)):
                    if (i + j) // (2 * p) == (i + j + k) // (2 * p):
                        pairs.append((i + j, i + j + k))
            k //= 2
        p *= 2
    return pairs


_SORT16 = _sorting_network(PEER_TOPK)
_BITONIC16 = [(i, i + s) for s in (8, 4, 2, 1) for i in range(PEER_TOPK) if (i // s) % 2 == 0]


def _top16_desc(vals):
    v = list(vals)
    for a, b in _SORT16:
        hi, lo = jnp.maximum(v[a], v[b]), jnp.minimum(v[a], v[b])
        v[a], v[b] = hi, lo
    for shift in (4, 2, 1):
        other = [pltpu.roll(x, shift, 0) for x in v]
        v = [jnp.maximum(v[k], other[PEER_TOPK - 1 - k]) for k in range(PEER_TOPK)]
        for a, b in _BITONIC16:
            hi, lo = jnp.maximum(v[a], v[b]), jnp.minimum(v[a], v[b])
            v[a], v[b] = hi, lo
    return v


def _peer_gates(sa, sb):
    t = sa.shape[1]
    a_top = _top16_desc([sa[SUBLANES * v:SUBLANES * (v + 1), :] for v in range(PEER_KEYS // SUBLANES)])
    b_top = _top16_desc([sb[SUBLANES * v:SUBLANES * (v + 1), :] for v in range(PEER_KEYS // SUBLANES)])
    sub = lax.broadcasted_iota(jnp.int32, (SUBLANES, t), 0)

    def pack(rows):
        out = rows[0]
        for s in range(1, SUBLANES):
            out = jnp.where(sub == s, rows[s], out)
        return out

    b_lo, b_hi, a_hi = pack(b_top[:8]), pack(b_top[8:]), pack(a_top[8:])
    cands = [a_top[0] + b_lo, a_top[0] + b_hi]
    cands += [a_top[k] + b_lo for k in range(1, 8)]
    cands += [a_hi + b_top[0]]
    valid = [None, None] + [sub < (PEER_TOPK // (k + 1)) for k in range(1, 8)] + [None]
    cands = [c if ok is None else jnp.where(ok, c, NEG_INF) for c, ok in zip(cands, valid)]
    pad = jnp.full((SUBLANES, t), NEG_INF, F32)
    tau = _top16_desc(cands + [pad] * (PEER_TOPK - len(cands)))[PEER_TOPK - 1]
    top = a_top[0] + b_top[0]
    z = jnp.zeros((SUBLANES, t), F32)
    for c in cands:
        z = z + jnp.where(c >= tau, jnp.exp(c - top), 0.0)
    for shift in (4, 2, 1):
        z = z + pltpu.roll(z, shift, 0)
    inv_z = 1.0 / z
    last = PEER_TOPK - 1

    def search(test):
        total = None
        bits = []
        for level, weight in enumerate((8, 4, 2, 1)):
            leaves = [b_top[m] for m in range(weight - 1, last, 2 * weight)]
            for bit in reversed(bits):
                leaves = [jnp.where(bit, leaves[2 * n + 1], leaves[2 * n])
                          for n in range(len(leaves) // 2)]
            bit = test(leaves[0])
            bits.append(bit)
            term = jnp.where(bit, float(weight), 0.0)
            total = term if total is None else total + term
        return total

    wa, count, wb, rank = [], [], [], []
    for v in range(PEER_KEYS // SUBLANES):
        xa = sa[SUBLANES * v:SUBLANES * (v + 1), :]
        xb = sb[SUBLANES * v:SUBLANES * (v + 1), :]
        wa.append(jnp.where(xa >= a_top[last], jnp.exp(xa - a_top[0]), 0.0) * inv_z)
        wb.append(jnp.where(xb >= b_top[last], jnp.exp(xb - b_top[0]), 0.0))
        count.append(search(lambda b: xa + b >= tau)
                     + jnp.where(xa + b_top[last] >= tau, 1.0, 0.0))
        rank.append(search(lambda b: b >= xb) + jnp.where(b_top[last] >= xb, 1.0, 0.0))
    cat = lambda parts: jnp.concatenate(parts, axis=0)
    return cat(wa), cat(count), cat(wb), cat(rank)


def _gelu_tanh(a):
    c = math.sqrt(2.0 / math.pi)
    inner = a * (a * a * (c * 0.044715) + c)
    return (a * 0.5) * (jnp.tanh(inner) + 1.0)


def _peer_kernel(final, scale_ref, x_ref, g_ref, wq_ref, keys_ref, u_ref, vt_ref, fg_ref, y_ref,
                 h_ref, h8_ref, sc_ref, wa_ref, cnt_ref, wb_ref, rank_ref, a0_ref, a1_ref, p_ref,
                 acc_ref):
    h_scale, act_unscale, p_scale, out_unscale = (scale_ref[k] for k in range(4))
    s = pl.program_id(1)
    n_blocks = PEER_EXPERTS // PEER_EXPERT_BLOCK
    tb = x_ref.shape[0]
    n_chunks = tb // PEER_GATE_CHUNK
    rows_per_step = PEER_EXPERT_BLOCK // PEER_KEYS

    @pl.when(s == 0)
    def _():
        h = _rms(x_ref[...], g_ref[...])
        h_ref[...] = h.astype(BF16)
        h8_ref[...] = (h * h_scale).astype(F8)
        acc_ref[...] = jnp.zeros_like(acc_ref)

        chunks_per_stage = PEER_SCORE_TOKENS // PEER_GATE_CHUNK

        def stage(si, _):
            r0 = pl.multiple_of(si * PEER_SCORE_TOKENS, PEER_SCORE_TOKENS)
            hs = h_ref[pl.ds(r0, PEER_SCORE_TOKENS), :]
            qs = [jnp.dot(hs, wq_ref[hd], preferred_element_type=F32).astype(BF16)
                  for hd in range(PEER_HEADS)]
            for hd in range(PEER_HEADS):
                for half in range(2):
                    sc = _dot_nt(keys_ref[2 * hd + half],
                                 qs[hd][:, half * PEER_KEYS:(half + 1) * PEER_KEYS])
                    for c in range(chunks_per_stage):
                        sc_ref[c, 2 * hd + half] = sc[:, c * PEER_GATE_CHUNK:
                                                      (c + 1) * PEER_GATE_CHUNK]

            def unit(k, _):
                c = k // (PEER_HEADS // 2)
                ci = si * chunks_per_stage + c
                for hd_local in range(2):
                    hd = (k % (PEER_HEADS // 2)) * 2 + hd_local
                    wa, count, wb, rank = _peer_gates(sc_ref[c, 2 * hd], sc_ref[c, 2 * hd + 1])
                    wa_ref[ci, hd] = wa * p_scale
                    cnt_ref[ci, hd] = count
                    wb_ref[ci, hd] = wb.astype(BF16)
                    rank_ref[ci, hd] = rank.astype(BF16)
                return 0

            lax.fori_loop(0, chunks_per_stage * PEER_HEADS // 2, unit, 0)
            return 0

        lax.fori_loop(0, tb // PEER_SCORE_TOKENS, stage, 0)

    packed_rows = 2 * SUBLANES
    tiles = PEER_KEYS // packed_rows

    def step(par, activate=True, gate_prev=True):
        a_new, a_old = (a0_ref, a1_ref) if par == 0 else (a1_ref, a0_ref)
        gate_block = s - 1
        for t0 in range(0, tb, PEER_TOKEN_TILE):
            tok = slice(t0, t0 + PEER_TOKEN_TILE)
            if activate:
                a_new[:, tok] = (_dot_nt(u_ref[...], h8_ref[tok, :]) * act_unscale).astype(BF16)
            if not gate_prev:
                continue
            for ci in range(t0 // PEER_GATE_CHUNK, (t0 + PEER_TOKEN_TILE) // PEER_GATE_CHUNK):
                cols = slice(ci * PEER_GATE_CHUNK, (ci + 1) * PEER_GATE_CHUNK)
                for il in range(rows_per_step):
                    i = gate_block * rows_per_step + il
                    rows = slice(il * PEER_KEYS, (il + 1) * PEER_KEYS)
                    gate = jnp.zeros((tiles, packed_rows, PEER_GATE_CHUNK), BF16)
                    for hd in range(PEER_HEADS):
                        row = lambda ref: jnp.broadcast_to(
                            ref[ci, hd, pl.ds(i, 1), :],
                            (packed_rows, PEER_GATE_CHUNK)).astype(BF16)[None]
                        rank = rank_ref[ci, hd].reshape(tiles, packed_rows, PEER_GATE_CHUNK)
                        wb = wb_ref[ci, hd].reshape(tiles, packed_rows, PEER_GATE_CHUNK)
                        gate = gate + jnp.where(rank <= row(cnt_ref), wb * row(wa_ref),
                                                jnp.zeros((), BF16))
                    gate = gate.reshape(PEER_KEYS, PEER_GATE_CHUNK)
                    p_ref[rows, cols] = (gate * _gelu_tanh(a_old[rows, cols])).astype(F8)
            acc_ref[:, tok] += jnp.dot(vt_ref[...], p_ref[:, tok], preferred_element_type=F32)

    assert n_blocks % 2 == 0
    last = n_blocks
    inner = jnp.logical_and(s > 0, s < last)

    @pl.when(s == 0)
    def _():
        step(0, gate_prev=False)

    @pl.when(jnp.logical_and(inner, s % 2 == 0))
    def _():
        step(0)

    @pl.when(jnp.logical_and(inner, s % 2 == 1))
    def _():
        step(1)

    @pl.when(s == last)
    def _():
        step(0, activate=False)
        y = x_ref[...] + acc_ref[...].T * out_unscale
        if final:
            y = _rms(y, fg_ref[...])
        y_ref[...] = y


def _peer(x2, gain, wq_heads, keys, u_q, vt_q, scales, final_gain, final):
    n = x2.shape[0]
    n_chunks = PEER_TOKENS // PEER_GATE_CHUNK
    n_blocks = PEER_EXPERTS // PEER_EXPERT_BLOCK
    gate_shape = (n_chunks, PEER_HEADS, PEER_KEYS, PEER_GATE_CHUNK)
    act_block = lambda s: jnp.minimum(s, n_blocks - 1)
    out_block = lambda s: jnp.maximum(s - 1, 0)
    once = pl.Buffered(1)
    return pl.pallas_call(
        functools.partial(_peer_kernel, final),
        grid=(n // PEER_TOKENS, n_blocks + 1),
        in_specs=[
            pl.BlockSpec(memory_space=pltpu.SMEM),
            pl.BlockSpec((PEER_TOKENS, D_MODEL), lambda t, s: (t, 0), pipeline_mode=once),
            pl.BlockSpec((1, D_MODEL), lambda t, s: (0, 0)),
            pl.BlockSpec((PEER_HEADS, D_MODEL, 2 * PEER_KEYS), lambda t, s: (0, 0, 0),
                         pipeline_mode=once),
            pl.BlockSpec((2 * PEER_HEADS, PEER_KEYS, PEER_KEYS), lambda t, s: (0, 0, 0)),
            pl.BlockSpec((PEER_EXPERT_BLOCK, D_MODEL), lambda t, s: (act_block(s), 0)),
            pl.BlockSpec((None, D_MODEL, PEER_EXPERT_BLOCK), lambda t, s: (out_block(s), 0, 0)),
            pl.BlockSpec((1, D_MODEL), lambda t, s: (0, 0)),
        ],
        out_specs=pl.BlockSpec((PEER_TOKENS, D_MODEL), lambda t, s: (t, 0)),
        out_shape=jax.ShapeDtypeStruct((n, D_MODEL), F32),
        scratch_shapes=[
            pltpu.VMEM((PEER_TOKENS, D_MODEL), BF16),
            pltpu.VMEM((PEER_TOKENS, D_MODEL), F8),
            pltpu.VMEM((PEER_SCORE_TOKENS // PEER_GATE_CHUNK, 2 * PEER_HEADS, PEER_KEYS,
                        PEER_GATE_CHUNK), F32),
            pltpu.VMEM(gate_shape, F32),
            pltpu.VMEM(gate_shape, F32),
            pltpu.VMEM(gate_shape, BF16),
            pltpu.VMEM(gate_shape, BF16),
            pltpu.VMEM((PEER_EXPERT_BLOCK, PEER_TOKENS), BF16),
            pltpu.VMEM((PEER_EXPERT_BLOCK, PEER_TOKENS), BF16),
            pltpu.VMEM((PEER_EXPERT_BLOCK, PEER_TOKENS), F8),
            pltpu.VMEM((D_MODEL, PEER_TOKENS), F32),
        ],
        compiler_params=_params(("arbitrary", "arbitrary")),
        name="peer",
    )(scales, x2, gain, wq_heads, keys, u_q, vt_q, final_gain)


def _ssm_matrices(lam_re, lam_im, log_step, b_re, b_im, c_re, c_im):
    lam = lax.complex(lam_re, lam_im)
    step = jnp.exp(log_step)[:, None]
    lam_bar = jnp.exp(lam * step)
    b_bar = ((lam_bar - 1.0) / lam)[:, :, None] * lax.complex(b_re, b_im)
    eye = jnp.eye(SSM_GROUPS, dtype=F32)

    def embed_in(m):
        return jnp.einsum("gpc,gh->gchp", m, eye).reshape(SSM_WIDTH, SSM_HALF)

    def embed_out(m):
        return jnp.einsum("gcp,gh->gphc", m, eye).reshape(SSM_HALF, SSM_WIDTH)

    bmat = jnp.concatenate([embed_in(jnp.real(b_bar)), embed_in(jnp.imag(b_bar))], axis=1)
    cmat = jnp.concatenate([embed_out(c_re), embed_out(-c_im)], axis=0)
    lre = jnp.broadcast_to(jnp.real(lam_bar).reshape(1, SSM_HALF), (SUBLANES, SSM_HALF))
    lim = jnp.broadcast_to(jnp.imag(lam_bar).reshape(1, SSM_HALF), (SUBLANES, SSM_HALF))
    return bmat.astype(BF16), cmat.astype(BF16), lre, lim


def _pow2_scale(bound):
    return jnp.exp2(jnp.floor(jnp.log2(FP8_LIMIT / jnp.maximum(bound, 1e-30))))


def _peer_tables(u_exp, v_exp, gain):
    h_bound = math.sqrt(D_MODEL) * jnp.max(jnp.abs(gain))
    u_norm = jnp.sqrt(jnp.max(jnp.sum(u_exp * u_exp, axis=1)))
    u_scale = _pow2_scale(u_norm)
    v_scale = _pow2_scale(jnp.max(jnp.abs(v_exp)))
    h_scale = _pow2_scale(h_bound)
    p_scale = _pow2_scale(PEER_HEADS * u_norm * h_bound)
    u_q = (u_exp * u_scale).astype(F8)
    blocks = v_exp.reshape(PEER_EXPERTS // PEER_EXPERT_BLOCK, PEER_EXPERT_BLOCK, D_MODEL)
    vt_q = (blocks.transpose(0, 2, 1) * v_scale).astype(F8)
    scales = jnp.stack([h_scale, 1.0 / (h_scale * u_scale), p_scale, 1.0 / (p_scale * v_scale)])
    return u_q, vt_q, scales.astype(F32)


def _in_proj_col_scale():
    s = jnp.ones((IN_WIDTH,), F32)
    s = s.at[SSM_WIDTH:SSM_WIDTH + DIL_WIDTH].set(HEAD_DIM ** -0.5)
    d0 = SSM_WIDTH + 3 * DIL_WIDTH
    s = s.at[d0:d0 + DIFF_WIDTH].set(DIFF_QK_DIM ** -0.5)
    return s.reshape(1, IN_WIDTH)


def kernel(x, norm1_g, w_in, ssm_lam_re, ssm_lam_im, ssm_log_step, ssm_b_re, ssm_b_im, ssm_c_re, ssm_c_im, ssm_d, ssm_w_glu, ssm_norm_g, dil_norm_g, diff_lam_q1, diff_lam_k1, diff_lam_q2, diff_lam_k2, diff_subln_g, w_out, norm2_g, peer_w_query, peer_sub_keys, peer_u, peer_v, final_norm_g):
    batch, seq, _ = x.shape
    n = batch * seq
    depth = w_in.shape[0]
    x2 = x.reshape(n, D_MODEL)
    col_scale = _in_proj_col_scale()
    row = lambda v: v.reshape(1, -1)
    for layer in range(depth):
        lambda_init = 0.8 - 0.6 * math.exp(-0.3 * layer)
        u_tm, dil_qkv, diff_qkv = _in_proj(x2, row(norm1_g[layer]), w_in[layer].astype(BF16),
                                           col_scale, batch, seq)
        bmat, cmat, lre, lim = _ssm_matrices(
            ssm_lam_re[layer], ssm_lam_im[layer], ssm_log_step[layer], ssm_b_re[layer],
            ssm_b_im[layer], ssm_c_re[layer], ssm_c_im[layer])
        y_ssm = _ssm(u_tm.reshape(seq * batch, SSM_WIDTH), bmat, cmat, lre, lim,
                     row(ssm_d[layer]), ssm_w_glu[layer].astype(BF16), row(ssm_norm_g[layer]),
                     batch, seq)
        y_dil = _dilated_attention(dil_qkv, row(dil_norm_g[layer]), batch, seq)
        lam = (jnp.exp(jnp.sum(diff_lam_q1[layer] * diff_lam_k1[layer]))
               - jnp.exp(jnp.sum(diff_lam_q2[layer] * diff_lam_k2[layer])) + lambda_init)
        y_diff = _diff_attention(diff_qkv, lam.reshape(1), diff_subln_g[layer].reshape(-1, 1),
                                 1.0 - lambda_init, batch, seq)
        x2 = _out_proj(x2, y_ssm.reshape(seq, batch * SSM_WIDTH), y_dil, y_diff,
                       w_out[layer].astype(BF16), batch, seq)
        wq_heads = peer_w_query[layer].reshape(D_MODEL, PEER_HEADS, 2 * PEER_KEYS)
        wq_heads = wq_heads.transpose(1, 0, 2).astype(BF16)
        keys = peer_sub_keys[layer].reshape(2 * PEER_HEADS, PEER_KEYS, PEER_KEYS).astype(BF16)
        u_q, vt_q, scales = _peer_tables(peer_u[layer], peer_v[layer], norm2_g[layer])
        x2 = _peer(x2, row(norm2_g[layer]), wq_heads, keys, u_q, vt_q, scales,
                   row(final_norm_g), layer == depth - 1)
    return x2.reshape(batch, seq, D_MODEL)
```

```python
import functools
import math

import jax
import jax.numpy as jnp
from jax import lax
from jax.experimental import pallas as pl
from jax.experimental.pallas import tpu as pltpu

F32 = jnp.float32
BF16 = jnp.bfloat16
F8 = jnp.float8_e4m3fn
FP8_LIMIT = 240.0

D_MODEL = 1024
HEAD_DIM = 64
SSM_WIDTH = 384
SSM_GROUP_CH = 16
SSM_GROUPS = 24
SSM_STATE = 64
SSM_HALF = SSM_GROUPS * SSM_STATE
DIL_WIDTH = 384
DIL_HEADS = 6
DIL_DILATIONS = (1, 4, 16)
DIFF_WIDTH = 256
DIFF_HEADS = 4
DIFF_QK_DIM = 32
IN_WIDTH = 2304
PEER_HEADS = 8
PEER_KEYS = 128
PEER_TOPK = 16
PEER_EXPERTS = PEER_KEYS * PEER_KEYS
NORM_EPS = 1e-6

LANES = 128
SUBLANES = 8
VMEM_LIMIT_BYTES = 56 * 1024 * 1024

PROJ_TOKENS = 1024
SSM_CHUNK = 128
ATT_BLOCK = 128
DIFF_BLOCK = 512
PEER_TOKENS = 1024
PEER_GATE_CHUNK = LANES
PEER_EXPERT_BLOCK = 1024
PEER_TOKEN_TILE = 256
PEER_SCORE_TOKENS = 256

NEG_INF = float("-inf")


def _rms(x, gain):
    return x * lax.rsqrt(jnp.mean(x * x, axis=-1, keepdims=True) + NORM_EPS) * gain


def _dot_nt(a, b):
    return lax.dot_general(a, b, (((1,), (1,)), ((), ())), preferred_element_type=F32)


def _params(semantics):
    return pltpu.CompilerParams(dimension_semantics=semantics, vmem_limit_bytes=VMEM_LIMIT_BYTES)


def _in_proj_kernel(x_ref, g_ref, w_ref, scale_ref, u_ref, dil_ref, diff_ref):
    h = _rms(x_ref[...], g_ref[...])
    p = jnp.dot(h.astype(BF16), w_ref[...], preferred_element_type=F32) * scale_ref[...]
    u_ref[...] = p[:, :SSM_WIDTH].astype(BF16)
    for t in range(dil_ref.shape[0]):
        dil_ref[t] = p[:, SSM_WIDTH + t * LANES:SSM_WIDTH + (t + 1) * LANES]
    diff_ref[...] = p[:, SSM_WIDTH + 3 * DIL_WIDTH:].astype(BF16)


def _in_proj(x2, gain, w_in, col_scale, batch, seq):
    n = batch * seq
    blocks_per_seq = seq // PROJ_TOKENS
    return pl.pallas_call(
        _in_proj_kernel,
        grid=(n // PROJ_TOKENS,),
        in_specs=[
            pl.BlockSpec((PROJ_TOKENS, D_MODEL), lambda i: (i, 0)),
            pl.BlockSpec((1, D_MODEL), lambda i: (0, 0)),
            pl.BlockSpec((D_MODEL, IN_WIDTH), lambda i: (0, 0)),
            pl.BlockSpec((1, IN_WIDTH), lambda i: (0, 0)),
        ],
        out_specs=[
            pl.BlockSpec((PROJ_TOKENS, SSM_WIDTH),
                         lambda i: (i % blocks_per_seq, i // blocks_per_seq)),
            pl.BlockSpec((3 * DIL_WIDTH // LANES, PROJ_TOKENS, LANES), lambda i: (0, i, 0)),
            pl.BlockSpec((PROJ_TOKENS, 3 * DIFF_WIDTH), lambda i: (i, 0)),
        ],
        out_shape=[
            jax.ShapeDtypeStruct((seq, batch * SSM_WIDTH), BF16),
            jax.ShapeDtypeStruct((3 * DIL_WIDTH // LANES, n, LANES), F32),
            jax.ShapeDtypeStruct((n, 3 * DIFF_WIDTH), BF16),
        ],
        compiler_params=_params(("arbitrary",)),
        name="in_proj",
    )(x2, gain, w_in, col_scale)


def _ssm_kernel(u_ref, bmat_ref, cmat_ref, lre_ref, lim_ref, d_ref, wglu_ref, g_ref,
                y_ref, s_ref, state_ref):
    rows = u_ref.shape[0]
    sub = 256

    @pl.when(pl.program_id(0) == 0)
    def _():
        state_ref[...] = jnp.zeros_like(state_ref)

    span = (LANES // SSM_GROUP_CH) * SSM_STATE
    blocks = [(slice(k * LANES, (k + 1) * LANES), slice(off + k * span, off + (k + 1) * span))
              for k in range(SSM_WIDTH // LANES) for off in (0, SSM_HALF)]
    for c in range(rows // sub):
        sl = slice(c * sub, (c + 1) * sub)
        for ch, st in blocks:
            s_ref[sl, st] = jnp.dot(u_ref[sl, ch], bmat_ref[ch, st], preferred_element_type=F32)

    lre = lre_ref[...]
    lim = lim_ref[...]

    def step(t, carry):
        sre, sim = carry
        r0 = pl.multiple_of(t * SUBLANES, SUBLANES)
        bre = s_ref[pl.ds(r0, SUBLANES), :SSM_HALF]
        bim = s_ref[pl.ds(r0, SUBLANES), SSM_HALF:]
        nre = lre * sre - lim * sim + bre
        nim = lre * sim + lim * sre + bim
        s_ref[pl.ds(r0, SUBLANES), :SSM_HALF] = nre
        s_ref[pl.ds(r0, SUBLANES), SSM_HALF:] = nim
        return nre, nim

    sre, sim = lax.fori_loop(0, rows // SUBLANES, step,
                             (state_ref[:, :SSM_HALF], state_ref[:, SSM_HALF:]), unroll=2)
    state_ref[:, :SSM_HALF] = sre
    state_ref[:, SSM_HALF:] = sim

    for c in range(rows // sub):
        sl = slice(c * sub, (c + 1) * sub)
        tiles = []
        for k in range(SSM_WIDTH // LANES):
            (ch, st_re), (_, st_im) = blocks[2 * k], blocks[2 * k + 1]
            tiles.append(
                jnp.dot(s_ref[sl, st_re].astype(BF16), cmat_ref[st_re, ch],
                        preferred_element_type=F32)
                + jnp.dot(s_ref[sl, st_im].astype(BF16), cmat_ref[st_im, ch],
                          preferred_element_type=F32))
        y = jnp.concatenate(tiles, axis=1)
        y = y + d_ref[...] * u_ref[sl, :].astype(F32)
        g = jax.nn.gelu(y)
        z = jnp.dot(g.astype(BF16), wglu_ref[...], preferred_element_type=F32)
        y = g * jax.nn.sigmoid(z)
        y_ref[sl, :] = _rms(y, g_ref[...]).astype(BF16)


def _ssm(u_tm, bmat, cmat, lre, lim, d_skip, w_glu, gain, batch, seq):
    assert batch == SUBLANES, "the S5 scan keeps one batch per sublane"
    rows = SSM_CHUNK * batch
    const = lambda i: (0, 0)
    return pl.pallas_call(
        _ssm_kernel,
        grid=(seq // SSM_CHUNK,),
        in_specs=[
            pl.BlockSpec((rows, SSM_WIDTH), lambda i: (i, 0)),
            pl.BlockSpec((SSM_WIDTH, 2 * SSM_HALF), const),
            pl.BlockSpec((2 * SSM_HALF, SSM_WIDTH), const),
            pl.BlockSpec((SUBLANES, SSM_HALF), const),
            pl.BlockSpec((SUBLANES, SSM_HALF), const),
            pl.BlockSpec((1, SSM_WIDTH), const),
            pl.BlockSpec((SSM_WIDTH, SSM_WIDTH), const),
            pl.BlockSpec((1, SSM_WIDTH), const),
        ],
        out_specs=pl.BlockSpec((rows, SSM_WIDTH), lambda i: (i, 0)),
        out_shape=jax.ShapeDtypeStruct((seq * batch, SSM_WIDTH), BF16),
        scratch_shapes=[
            pltpu.VMEM((rows, 2 * SSM_HALF), F32),
            pltpu.VMEM((SUBLANES, 2 * SSM_HALF), F32),
        ],
        compiler_params=_params(("arbitrary",)),
        name="ssm",
    )(u_tm, bmat, cmat, lre, lim, d_skip, w_glu, gain)


def _dilated_kernel(x_ref, g_ref, o_ref, m_ref, l_ref, acc_ref):
    seq = x_ref.shape[1]
    q_tiles = DIL_WIDTH // LANES
    heads_per_tile = LANES // HEAD_DIM
    m_ref[...] = jnp.full(m_ref.shape, NEG_INF, F32)
    l_ref[...] = jnp.zeros_like(l_ref)
    acc_ref[...] = jnp.zeros_like(acc_ref)
    row = lax.broadcasted_iota(jnp.int32, (ATT_BLOCK, ATT_BLOCK), 0)
    col = lax.broadcasted_iota(jnp.int32, (ATT_BLOCK, ATT_BLOCK), 1)
    lane = lax.broadcasted_iota(jnp.int32, (ATT_BLOCK, LANES), 1)
    cur_ok = col <= row
    spread = [(row == heads_per_tile * t + col // HEAD_DIM).astype(BF16) for t in range(q_tiles)]
    in_head = [col // HEAD_DIM == k for k in range(heads_per_tile)]
    assert heads_per_tile == 2

    def unit(dil, start, prev_start, prev_live):
        rows = pl.ds(start, ATT_BLOCK, stride=dil)
        load = lambda t, r: x_ref[t, r, :].astype(BF16)
        q = [load(t, rows) for t in range(q_tiles)]
        kc = [load(q_tiles + t, rows) for t in range(q_tiles)]
        vc = [load(2 * q_tiles + t, rows) for t in range(q_tiles)]
        has_prev = prev_start is not None
        if has_prev:
            prows = pl.ds(prev_start, ATT_BLOCK, stride=dil)
            kp = [load(q_tiles + t, prows) for t in range(q_tiles)]
            vp = [load(2 * q_tiles + t, prows) for t in range(q_tiles)]
            prev_ok = jnp.logical_and(col >= row, prev_live)
        tile_of = [h // heads_per_tile for h in range(DIL_HEADS)]
        qm = [jnp.where(in_head[h % heads_per_tile], q[tile_of[h]], jnp.zeros((), BF16))
              for h in range(DIL_HEADS)]
        sc_all = [_dot_nt(qm[h], kc[tile_of[h]]) for h in range(DIL_HEADS)]
        if has_prev:
            sp_all = [_dot_nt(qm[h], kp[tile_of[h]]) for h in range(DIL_HEADS)]
        pc_all, pp_all = [], []
        m_new_p = jnp.zeros((ATT_BLOCK, LANES), F32)
        l_new_p = jnp.zeros((ATT_BLOCK, LANES), F32)
        for h in range(DIL_HEADS):
            sc = jnp.where(cur_ok, sc_all[h], NEG_INF)
            m = jnp.max(sc, axis=-1, keepdims=True)
            if has_prev:
                sp = jnp.where(prev_ok, sp_all[h], NEG_INF)
                m = jnp.maximum(m, jnp.max(sp, axis=-1, keepdims=True))
            pc = jnp.exp(sc - m)
            den = jnp.sum(pc, axis=-1, keepdims=True)
            if has_prev:
                pp = jnp.exp(sp - m)
                den = den + jnp.sum(pp, axis=-1, keepdims=True)
                pp_all.append(pp.astype(BF16))
            pc_all.append(pc.astype(BF16))
            m_new_p = jnp.where(lane == h, m, m_new_p)
            l_new_p = jnp.where(lane == h, den, l_new_p)
        m_old = m_ref[rows, :]
        m_new = jnp.maximum(m_old, m_new_p)
        w_old = jnp.exp(m_old - m_new)
        w_pat = jnp.exp(m_new_p - m_new)
        m_ref[rows, :] = m_new
        l_ref[rows, :] = w_old * l_ref[rows, :] + w_pat * l_new_p
        for t in range(q_tiles):
            parts = []
            for h in range(heads_per_tile * t, heads_per_tile * (t + 1)):
                part = jnp.dot(pc_all[h], vc[t], preferred_element_type=F32)
                if has_prev:
                    part = part + jnp.dot(pp_all[h], vp[t], preferred_element_type=F32)
                parts.append(part)
            new = jnp.where(in_head[0], parts[0], parts[1])
            s_old = jnp.dot(w_old.astype(BF16), spread[t], preferred_element_type=F32)
            s_pat = jnp.dot(w_pat.astype(BF16), spread[t], preferred_element_type=F32)
            acc_ref[t, rows, :] = s_old * acc_ref[t, rows, :] + s_pat * new

    for dil in reversed(DIL_DILATIONS):
        sub_len = seq // dil
        nblk = sub_len // ATT_BLOCK
        units = dil * nblk

        def pair(u, _, dil=dil, nblk=nblk):
            for k in range(2):
                idx = 2 * u + k
                res, blk = idx % dil, idx // dil
                start = res + dil * ATT_BLOCK * blk
                if nblk == 1:
                    unit(dil, start, None, None)
                else:
                    prev = res + dil * ATT_BLOCK * jnp.maximum(blk - 1, 0)
                    unit(dil, start, prev, blk > 0)
            return 0

        lax.fori_loop(0, units // 2, pair, 0)

    def finish(blk, _):
        rows = pl.ds(pl.multiple_of(blk * ATT_BLOCK, ATT_BLOCK), ATT_BLOCK)
        inv = jnp.where(lane < DIL_HEADS, 1.0 / l_ref[rows, :], 0.0)
        inv_hi = inv.astype(BF16)
        inv_lo = (inv - inv_hi.astype(F32)).astype(BF16)
        ys = []
        for t in range(q_tiles):
            scale = (jnp.dot(inv_hi, spread[t], preferred_element_type=F32)
                     + jnp.dot(inv_lo, spread[t], preferred_element_type=F32))
            ys.append(acc_ref[t, rows, :] * scale)
        sq = sum(jnp.sum(y * y, axis=-1, keepdims=True) for y in ys)
        norm = lax.rsqrt(sq * (1.0 / DIL_WIDTH) + NORM_EPS)
        for t in range(q_tiles):
            cols = slice(t * LANES, (t + 1) * LANES)
            o_ref[rows, cols] = (ys[t] * norm * g_ref[:, cols]).astype(BF16)
        return 0

    lax.fori_loop(0, seq // ATT_BLOCK, finish, 0)


def _dilated_attention(qkv_tiles, gain, batch, seq):
    n = batch * seq
    n_tiles = 3 * DIL_WIDTH // LANES
    return pl.pallas_call(
        _dilated_kernel,
        grid=(batch,),
        in_specs=[
            pl.BlockSpec((n_tiles, seq, LANES), lambda b: (0, b, 0)),
            pl.BlockSpec((1, DIL_WIDTH), lambda b: (0, 0)),
        ],
        out_specs=pl.BlockSpec((seq, DIL_WIDTH), lambda b: (b, 0)),
        out_shape=jax.ShapeDtypeStruct((n, DIL_WIDTH), BF16),
        scratch_shapes=[
            pltpu.VMEM((seq, LANES), F32),
            pltpu.VMEM((seq, LANES), F32),
            pltpu.VMEM((DIL_WIDTH // LANES, seq, LANES), F32),
        ],
        compiler_params=_params(("arbitrary",)),
        name="dilated_attention",
    )(qkv_tiles, gain)


def _diff_kernel(out_scale, lam_ref, q_ref, k_ref, v_ref, g_ref, o_ref, qt_ref, vt_ref, ot_ref,
                 qm_ref, m_ref, l_ref, acc_ref):
    seq = q_ref.shape[0]
    nblk = seq // DIFF_BLOCK
    lam = lam_ref[0]
    for i in range(nblk):
        rows = slice(i * DIFF_BLOCK, (i + 1) * DIFF_BLOCK)
        qt_ref[i] = q_ref[rows, :].astype(F32).T.astype(BF16)
        vt_ref[i] = v_ref[rows, :].astype(F32).T.astype(BF16)
    krow = lax.broadcasted_iota(jnp.int32, (DIFF_BLOCK, DIFF_BLOCK), 0)
    qcol = lax.broadcasted_iota(jnp.int32, (DIFF_BLOCK, DIFF_BLOCK), 1)
    causal = krow <= qcol
    chan = lax.broadcasted_iota(jnp.int32, (DIFF_WIDTH, DIFF_BLOCK), 0)

    n_maps = 2 * DIFF_HEADS

    def attend(kj, mask):
        k0 = pl.multiple_of(kj * DIFF_BLOCK, DIFF_BLOCK)
        k = k_ref[pl.ds(k0, DIFF_BLOCK), :]
        scores = [jnp.dot(k, qm_ref[c], preferred_element_type=F32) for c in range(n_maps)]
        probs, alphas = [], []
        for c in range(n_maps):
            s = scores[c]
            if mask:
                s = jnp.where(causal, s, NEG_INF)
            m_old = m_ref[c]
            m_new = jnp.maximum(m_old, jnp.max(s, axis=0, keepdims=True))
            alpha = jnp.exp(m_old - m_new)
            p = jnp.exp(s - m_new)
            m_ref[c] = m_new
            l_ref[c] = alpha * l_ref[c] + jnp.sum(p, axis=0, keepdims=True)
            probs.append(p.astype(BF16))
            alphas.append(alpha)
        for c in range(n_maps):
            h = c // 2
            vt = vt_ref[kj, h * HEAD_DIM:(h + 1) * HEAD_DIM, :]
            acc_ref[c] = alphas[c] * acc_ref[c] + jnp.dot(vt, probs[c],
                                                          preferred_element_type=F32)

    def q_block(qi, _):
        qt = qt_ref[qi]
        for c in range(n_maps):
            lo = c * DIFF_QK_DIM
            qm_ref[c] = jnp.where(jnp.logical_and(chan >= lo, chan < lo + DIFF_QK_DIM), qt,
                                  jnp.zeros((), BF16))
        m_ref[...] = jnp.full(m_ref.shape, NEG_INF, F32)
        l_ref[...] = jnp.zeros_like(l_ref)
        acc_ref[...] = jnp.zeros_like(acc_ref)

        def k_block(kj, _):
            attend(kj, False)
            return 0

        lax.fori_loop(0, qi, k_block, 0)
        attend(qi, True)
        for h in range(DIFF_HEADS):
            o = (acc_ref[2 * h] / l_ref[2 * h]
                 - lam * (acc_ref[2 * h + 1] / l_ref[2 * h + 1]))
            o = o * lax.rsqrt(jnp.mean(o * o, axis=0, keepdims=True) + NORM_EPS)
            ot_ref[qi, h * HEAD_DIM:(h + 1) * HEAD_DIM, :] = o * (g_ref[...] * out_scale)
        return 0

    lax.fori_loop(0, nblk, q_block, 0)

    for i in range(nblk):
        o_ref[i * DIFF_BLOCK:(i + 1) * DIFF_BLOCK, :] = ot_ref[i].T.astype(BF16)


def _diff_attention(qkv, lam, subln_g_col, out_scale, batch, seq):
    n = batch * seq
    blk = (seq, DIFF_WIDTH)
    nblk = seq // DIFF_BLOCK
    return pl.pallas_call(
        functools.partial(_diff_kernel, out_scale),
        grid=(batch,),
        in_specs=[
            pl.BlockSpec(memory_space=pltpu.SMEM),
            pl.BlockSpec(blk, lambda b: (b, 0)),
            pl.BlockSpec(blk, lambda b: (b, 1)),
            pl.BlockSpec(blk, lambda b: (b, 2)),
            pl.BlockSpec((HEAD_DIM, 1), lambda b: (0, 0)),
        ],
        out_specs=pl.BlockSpec(blk, lambda b: (b, 0)),
        out_shape=jax.ShapeDtypeStruct((n, DIFF_WIDTH), BF16),
        scratch_shapes=[
            pltpu.VMEM((nblk, DIFF_WIDTH, DIFF_BLOCK), BF16),
            pltpu.VMEM((nblk, DIFF_WIDTH, DIFF_BLOCK), BF16),
            pltpu.VMEM((nblk, DIFF_WIDTH, DIFF_BLOCK), F32),
            pltpu.VMEM((2 * DIFF_HEADS, DIFF_WIDTH, DIFF_BLOCK), BF16),
            pltpu.VMEM((2 * DIFF_HEADS, 1, DIFF_BLOCK), F32),
            pltpu.VMEM((2 * DIFF_HEADS, 1, DIFF_BLOCK), F32),
            pltpu.VMEM((2 * DIFF_HEADS, HEAD_DIM, DIFF_BLOCK), F32),
        ],
        compiler_params=_params(("arbitrary",)),
        name="diff_attention",
    )(lam, qkv, qkv, qkv, subln_g_col)


def _out_proj_kernel(x_ref, ssm_ref, dil_ref, diff_ref, w_ref, y_ref):
    acc = jnp.dot(ssm_ref[...], w_ref[:SSM_WIDTH, :], preferred_element_type=F32)
    acc = acc + jnp.dot(dil_ref[...], w_ref[SSM_WIDTH:SSM_WIDTH + DIL_WIDTH, :],
                        preferred_element_type=F32)
    acc = acc + jnp.dot(diff_ref[...], w_ref[SSM_WIDTH + DIL_WIDTH:, :],
                        preferred_element_type=F32)
    y_ref[...] = x_ref[...] + acc


def _out_proj(x2, y_ssm_tm, y_dil, y_diff, w_out, batch, seq):
    n = batch * seq
    blocks_per_seq = seq // PROJ_TOKENS
    tok = lambda w: pl.BlockSpec((PROJ_TOKENS, w), lambda i: (i, 0))
    return pl.pallas_call(
        _out_proj_kernel,
        grid=(n // PROJ_TOKENS,),
        in_specs=[
            tok(D_MODEL),
            pl.BlockSpec((PROJ_TOKENS, SSM_WIDTH),
                         lambda i: (i % blocks_per_seq, i // blocks_per_seq)),
            tok(DIL_WIDTH),
            tok(DIFF_WIDTH),
            pl.BlockSpec((D_MODEL, D_MODEL), lambda i: (0, 0)),
        ],
        out_specs=tok(D_MODEL),
        out_shape=jax.ShapeDtypeStruct((n, D_MODEL), F32),
        compiler_params=_params(("arbitrary",)),
        name="out_proj",
    )(x2, y_ssm_tm, y_dil, y_diff, w_out)


def _sorting_network(n):
    pairs = []
    p = 1
    while p < n:
        k = p
        while k >= 1:
            for j in range(k % p, n - k, 2 * k):
                for i in range(min(k, n - j - k)):
                    if (i + j) // (2 * p) == (i + j + k) // (2 * p):
                        pairs.append((i + j, i + j + k))
            k //= 2
        p *= 2
    return pairs


_SORT16 = _sorting_network(PEER_TOPK)
_BITONIC16 = [(i, i + s) for s in (8, 4, 2, 1) for i in range(PEER_TOPK) if (i // s) % 2 == 0]


def _top16_desc(vals):
    v = list(vals)
    for a, b in _SORT16:
        hi, lo = jnp.maximum(v[a], v[b]), jnp.minimum(v[a], v[b])
        v[a], v[b] = hi, lo
    for shift in (4, 2, 1):
        other = [pltpu.roll(x, shift, 0) for x in v]
        v = [jnp.maximum(v[k], other[PEER_TOPK - 1 - k]) for k in range(PEER_TOPK)]
        for a, b in _BITONIC16:
            hi, lo = jnp.maximum(v[a], v[b]), jnp.minimum(v[a], v[b])
            v[a], v[b] = hi, lo
    return v


def _peer_gates(sa, sb):
    t = sa.shape[1]
    a_top = _top16_desc([sa[SUBLANES * v:SUBLANES * (v + 1), :] for v in range(PEER_KEYS // SUBLANES)])
    b_top = _top16_desc([sb[SUBLANES * v:SUBLANES * (v + 1), :] for v in range(PEER_KEYS // SUBLANES)])
    sub = lax.broadcasted_iota(jnp.int32, (SUBLANES, t), 0)

    def pack(rows):
        out = rows[0]
        for s in range(1, SUBLANES):
            out = jnp.where(sub == s, rows[s], out)
        return out

    b_lo, b_hi, a_hi = pack(b_top[:8]), pack(b_top[8:]), pack(a_top[8:])
    cands = [a_top[0] + b_lo, a_top[0] + b_hi]
    cands += [a_top[k] + b_lo for k in range(1, 8)]
    cands += [a_hi + b_top[0]]
    valid = [None, None] + [sub < (PEER_TOPK // (k + 1)) for k in range(1, 8)] + [None]
    cands = [c if ok is None else jnp.where(ok, c, NEG_INF) for c, ok in zip(cands, valid)]
    pad = jnp.full((SUBLANES, t), NEG_INF, F32)
    tau = _top16_desc(cands + [pad] * (PEER_TOPK - len(cands)))[PEER_TOPK - 1]
    top = a_top[0] + b_top[0]
    z = jnp.zeros((SUBLANES, t), F32)
    for c in cands:
        z = z + jnp.where(c >= tau, jnp.exp(c - top), 0.0)
    for shift in (4, 2, 1):
        z = z + pltpu.roll(z, shift, 0)
    inv_z = 1.0 / z
    last = PEER_TOPK - 1

    def search(test):
        total = None
        bits = []
        for level, weight in enumerate((8, 4, 2, 1)):
            leaves = [b_top[m] for m in range(weight - 1, last, 2 * weight)]
            for bit in reversed(bits):
                leaves = [jnp.where(bit, leaves[2 * n + 1], leaves[2 * n])
                          for n in range(len(leaves) // 2)]
            bit = test(leaves[0])
            bits.append(bit)
            term = jnp.where(bit, float(weight), 0.0)
            total = term if total is None else total + term
        return total

    wa, count, wb, rank = [], [], [], []
    for v in range(PEER_KEYS // SUBLANES):
        xa = sa[SUBLANES * v:SUBLANES * (v + 1), :]
        xb = sb[SUBLANES * v:SUBLANES * (v + 1), :]
        wa.append(jnp.where(xa >= a_top[last], jnp.exp(xa - a_top[0]), 0.0) * inv_z)
        wb.append(jnp.where(xb >= b_top[last], jnp.exp(xb - b_top[0]), 0.0))
        count.append(search(lambda b: xa + b >= tau)
                     + jnp.where(xa + b_top[last] >= tau, 1.0, 0.0))
        rank.append(search(lambda b: b >= xb) + jnp.where(b_top[last] >= xb, 1.0, 0.0))
    cat = lambda parts: jnp.concatenate(parts, axis=0)
    return cat(wa), cat(count), cat(wb), cat(rank)


def _gelu_tanh(a):
    c = math.sqrt(2.0 / math.pi)
    inner = a * (a * a * (c * 0.044715) + c)
    return (a * 0.5) * (jnp.tanh(inner) + 1.0)


def _peer_kernel(final, scale_ref, x_ref, g_ref, wq_ref, keys_ref, u_ref, vt_ref, fg_ref, y_ref,
                 h_ref, h8_ref, sc_ref, wa_ref, cnt_ref, wb_ref, rank_ref, a0_ref, a1_ref, p_ref,
                 acc_ref):
    h_scale, act_unscale, p_scale, out_unscale = (scale_ref[k] for k in range(4))
    s = pl.program_id(1)
    n_blocks = PEER_EXPERTS // PEER_EXPERT_BLOCK
    tb = x_ref.shape[0]
    n_chunks = tb // PEER_GATE_CHUNK
    rows_per_step = PEER_EXPERT_BLOCK // PEER_KEYS

    @pl.when(s == 0)
    def _():
        h = _rms(x_ref[...], g_ref[...])
        h_ref[...] = h.astype(BF16)
        h8_ref[...] = (h * h_scale).astype(F8)
        acc_ref[...] = jnp.zeros_like(acc_ref)

        chunks_per_stage = PEER_SCORE_TOKENS // PEER_GATE_CHUNK

        def stage(si, _):
            r0 = pl.multiple_of(si * PEER_SCORE_TOKENS, PEER_SCORE_TOKENS)
            hs = h_ref[pl.ds(r0, PEER_SCORE_TOKENS), :]
            qs = [jnp.dot(hs, wq_ref[hd], preferred_element_type=F32).astype(BF16)
                  for hd in range(PEER_HEADS)]
            for hd in range(PEER_HEADS):
                for half in range(2):
                    sc = _dot_nt(keys_ref[2 * hd + half],
                                 qs[hd][:, half * PEER_KEYS:(half + 1) * PEER_KEYS])
                    for c in range(chunks_per_stage):
                        sc_ref[c, 2 * hd + half] = sc[:, c * PEER_GATE_CHUNK:
                                                      (c + 1) * PEER_GATE_CHUNK]

            def unit(k, _):
                c = k // (PEER_HEADS // 2)
                ci = si * chunks_per_stage + c
                for hd_local in range(2):
                    hd = (k % (PEER_HEADS // 2)) * 2 + hd_local
                    wa, count, wb, rank = _peer_gates(sc_ref[c, 2 * hd], sc_ref[c, 2 * hd + 1])
                    wa_ref[ci, hd] = wa * p_scale
                    cnt_ref[ci, hd] = count
                    wb_ref[ci, hd] = wb.astype(BF16)
                    rank_ref[ci, hd] = rank.astype(BF16)
                return 0

            lax.fori_loop(0, chunks_per_stage * PEER_HEADS // 2, unit, 0)
            return 0

        lax.fori_loop(0, tb // PEER_SCORE_TOKENS, stage, 0)

    packed_rows = 2 * SUBLANES
    tiles = PEER_KEYS // packed_rows

    def step(par, activate=True, gate_prev=True):
        a_new, a_old = (a0_ref, a1_ref) if par == 0 else (a1_ref, a0_ref)
        gate_block = s - 1
        for t0 in range(0, tb, PEER_TOKEN_TILE):
            tok = slice(t0, t0 + PEER_TOKEN_TILE)
            if activate:
                a_new[:, tok] = (_dot_nt(u_ref[...], h8_ref[tok, :]) * act_unscale).astype(BF16)
            if not gate_prev:
                continue
            for ci in range(t0 // PEER_GATE_CHUNK, (t0 + PEER_TOKEN_TILE) // PEER_GATE_CHUNK):
                cols = slice(ci * PEER_GATE_CHUNK, (ci + 1) * PEER_GATE_CHUNK)
                for il in range(rows_per_step):
                    i = gate_block * rows_per_step + il
                    rows = slice(il * PEER_KEYS, (il + 1) * PEER_KEYS)
                    gate = jnp.zeros((tiles, packed_rows, PEER_GATE_CHUNK), BF16)
                    for hd in range(PEER_HEADS):
                        row = lambda ref: jnp.broadcast_to(
                            ref[ci, hd, pl.ds(i, 1), :],
                            (packed_rows, PEER_GATE_CHUNK)).astype(BF16)[None]
                        rank = rank_ref[ci, hd].reshape(tiles, packed_rows, PEER_GATE_CHUNK)
                        wb = wb_ref[ci, hd].reshape(tiles, packed_rows, PEER_GATE_CHUNK)
                        gate = gate + jnp.where(rank <= row(cnt_ref), wb * row(wa_ref),
                                                jnp.zeros((), BF16))
                    gate = gate.reshape(PEER_KEYS, PEER_GATE_CHUNK)
                    p_ref[rows, cols] = (gate * _gelu_tanh(a_old[rows, cols])).astype(F8)
            acc_ref[:, tok] += jnp.dot(vt_ref[...], p_ref[:, tok], preferred_element_type=F32)

    assert n_blocks % 2 == 0
    last = n_blocks
    inner = jnp.logical_and(s > 0, s < last)

    @pl.when(s == 0)
    def _():
        step(0, gate_prev=False)

    @pl.when(jnp.logical_and(inner, s % 2 == 0))
    def _():
        step(0)

    @pl.when(jnp.logical_and(inner, s % 2 == 1))
    def _():
        step(1)

    @pl.when(s == last)
    def _():
        step(0, activate=False)
        y = x_ref[...] + acc_ref[...].T * out_unscale
        if final:
            y = _rms(y, fg_ref[...])
        y_ref[...] = y


def _peer(x2, gain, wq_heads, keys, u_q, vt_q, scales, final_gain, final):
    n = x2.shape[0]
    n_chunks = PEER_TOKENS // PEER_GATE_CHUNK
    n_blocks = PEER_EXPERTS // PEER_EXPERT_BLOCK
    gate_shape = (n_chunks, PEER_HEADS, PEER_KEYS, PEER_GATE_CHUNK)
    act_block = lambda s: jnp.minimum(s, n_blocks - 1)
    out_block = lambda s: jnp.maximum(s - 1, 0)
    once = pl.Buffered(1)
    return pl.pallas_call(
        functools.partial(_peer_kernel, final),
        grid=(n // PEER_TOKENS, n_blocks + 1),
        in_specs=[
            pl.BlockSpec(memory_space=pltpu.SMEM),
            pl.BlockSpec((PEER_TOKENS, D_MODEL), lambda t, s: (t, 0), pipeline_mode=once),
            pl.BlockSpec((1, D_MODEL), lambda t, s: (0, 0)),
            pl.BlockSpec((PEER_HEADS, D_MODEL, 2 * PEER_KEYS), lambda t, s: (0, 0, 0),
                         pipeline_mode=once),
            pl.BlockSpec((2 * PEER_HEADS, PEER_KEYS, PEER_KEYS), lambda t, s: (0, 0, 0)),
            pl.BlockSpec((PEER_EXPERT_BLOCK, D_MODEL), lambda t, s: (act_block(s), 0)),
            pl.BlockSpec((None, D_MODEL, PEER_EXPERT_BLOCK), lambda t, s: (out_block(s), 0, 0)),
            pl.BlockSpec((1, D_MODEL), lambda t, s: (0, 0)),
        ],
        out_specs=pl.BlockSpec((PEER_TOKENS, D_MODEL), lambda t, s: (t, 0)),
        out_shape=jax.ShapeDtypeStruct((n, D_MODEL), F32),
        scratch_shapes=[
            pltpu.VMEM((PEER_TOKENS, D_MODEL), BF16),
            pltpu.VMEM((PEER_TOKENS, D_MODEL), F8),
            pltpu.VMEM((PEER_SCORE_TOKENS // PEER_GATE_CHUNK, 2 * PEER_HEADS, PEER_KEYS,
                        PEER_GATE_CHUNK), F32),
            pltpu.VMEM(gate_shape, F32),
            pltpu.VMEM(gate_shape, F32),
            pltpu.VMEM(gate_shape, BF16),
            pltpu.VMEM(gate_shape, BF16),
            pltpu.VMEM((PEER_EXPERT_BLOCK, PEER_TOKENS), BF16),
            pltpu.VMEM((PEER_EXPERT_BLOCK, PEER_TOKENS), BF16),
            pltpu.VMEM((PEER_EXPERT_BLOCK, PEER_TOKENS), F8),
            pltpu.VMEM((D_MODEL, PEER_TOKENS), F32),
        ],
        compiler_params=_params(("arbitrary", "arbitrary")),
        name="peer",
    )(scales, x2, gain, wq_heads, keys, u_q, vt_q, final_gain)


def _ssm_matrices(lam_re, lam_im, log_step, b_re, b_im, c_re, c_im):
    lam = lax.complex(lam_re, lam_im)
    step = jnp.exp(log_step)[:, None]
    lam_bar = jnp.exp(lam * step)
    b_bar = ((lam_bar - 1.0) / lam)[:, :, None] * lax.complex(b_re, b_im)
    eye = jnp.eye(SSM_GROUPS, dtype=F32)

    def embed_in(m):
        return jnp.einsum("gpc,gh->gchp", m, eye).reshape(SSM_WIDTH, SSM_HALF)

    def embed_out(m):
        return jnp.einsum("gcp,gh->gphc", m, eye).reshape(SSM_HALF, SSM_WIDTH)

    bmat = jnp.concatenate([embed_in(jnp.real(b_bar)), embed_in(jnp.imag(b_bar))], axis=1)
    cmat = jnp.concatenate([embed_out(c_re), embed_out(-c_im)], axis=0)
    lre = jnp.broadcast_to(jnp.real(lam_bar).reshape(1, SSM_HALF), (SUBLANES, SSM_HALF))
    lim = jnp.broadcast_to(jnp.imag(lam_bar).reshape(1, SSM_HALF), (SUBLANES, SSM_HALF))
    return bmat.astype(BF16), cmat.astype(BF16), lre, lim


def _pow2_scale(bound):
    return jnp.exp2(jnp.floor(jnp.log2(FP8_LIMIT / jnp.maximum(bound, 1e-30))))


def _peer_tables(u_exp, v_exp, gain):
    h_bound = math.sqrt(D_MODEL) * jnp.max(jnp.abs(gain))
    u_norm = jnp.sqrt(jnp.max(jnp.sum(u_exp * u_exp, axis=1)))
    u_scale = _pow2_scale(u_norm)
    v_scale = _pow2_scale(jnp.max(jnp.abs(v_exp)))
    h_scale = _pow2_scale(h_bound)
    p_scale = _pow2_scale(PEER_HEADS * u_norm * h_bound)
    u_q = (u_exp * u_scale).astype(F8)
    blocks = v_exp.reshape(PEER_EXPERTS // PEER_EXPERT_BLOCK, PEER_EXPERT_BLOCK, D_MODEL)
    vt_q = (blocks.transpose(0, 2, 1) * v_scale).astype(F8)
    scales = jnp.stack([h_scale, 1.0 / (h_scale * u_scale), p_scale, 1.0 / (p_scale * v_scale)])
    return u_q, vt_q, scales.astype(F32)


def _in_proj_col_scale():
    s = jnp.ones((IN_WIDTH,), F32)
    s = s.at[SSM_WIDTH:SSM_WIDTH + DIL_WIDTH].set(HEAD_DIM ** -0.5)
    d0 = SSM_WIDTH + 3 * DIL_WIDTH
    s = s.at[d0:d0 + DIFF_WIDTH].set(DIFF_QK_DIM ** -0.5)
    return s.reshape(1, IN_WIDTH)


def kernel(x, norm1_g, w_in, ssm_lam_re, ssm_lam_im, ssm_log_step, ssm_b_re, ssm_b_im, ssm_c_re, ssm_c_im, ssm_d, ssm_w_glu, ssm_norm_g, dil_norm_g, diff_lam_q1, diff_lam_k1, diff_lam_q2, diff_lam_k2, diff_subln_g, w_out, norm2_g, peer_w_query, peer_sub_keys, peer_u, peer_v, final_norm_g):
    batch, seq, _ = x.shape
    n = batch * seq
    depth = w_in.shape[0]
    x2 = x.reshape(n, D_MODEL)
    col_scale = _in_proj_col_scale()
    row = lambda v: v.reshape(1, -1)
    for layer in range(depth):
        lambda_init = 0.8 - 0.6 * math.exp(-0.3 * layer)
        u_tm, dil_qkv, diff_qkv = _in_proj(x2, row(norm1_g[layer]), w_in[layer].astype(BF16),
                                           col_scale, batch, seq)
        bmat, cmat, lre, lim = _ssm_matrices(
            ssm_lam_re[layer], ssm_lam_im[layer], ssm_log_step[layer], ssm_b_re[layer],
            ssm_b_im[layer], ssm_c_re[layer], ssm_c_im[layer])
        y_ssm = _ssm(u_tm.reshape(seq * batch, SSM_WIDTH), bmat, cmat, lre, lim,
                     row(ssm_d[layer]), ssm_w_glu[layer].astype(BF16), row(ssm_norm_g[layer]),
                     batch, seq)
        y_dil = _dilated_attention(dil_qkv, row(dil_norm_g[layer]), batch, seq)
        lam = (jnp.exp(jnp.sum(diff_lam_q1[layer] * diff_lam_k1[layer]))
               - jnp.exp(jnp.sum(diff_lam_q2[layer] * diff_lam_k2[layer])) + lambda_init)
        y_diff = _diff_attention(diff_qkv, lam.reshape(1), diff_subln_g[layer].reshape(-1, 1),
                                 1.0 - lambda_init, batch, seq)
        x2 = _out_proj(x2, y_ssm.reshape(seq, batch * SSM_WIDTH), y_dil, y_diff,
                       w_out[layer].astype(BF16), batch, seq)
        wq_heads = peer_w_query[layer].reshape(D_MODEL, PEER_HEADS, 2 * PEER_KEYS)
        wq_heads = wq_heads.transpose(1, 0, 2).astype(BF16)
        keys = peer_sub_keys[layer].reshape(2 * PEER_HEADS, PEER_KEYS, PEER_KEYS).astype(BF16)
        u_q, vt_q, scales = _peer_tables(peer_u[layer], peer_v[layer], norm2_g[layer])
        x2 = _peer(x2, row(norm2_g[layer]), wq_heads, keys, u_q, vt_q, scales,
                   row(final_norm_g), layer == depth - 1)
    return x2.reshape(batch, seq, D_MODEL)
```

```python
import functools
import math

import jax
import jax.numpy as jnp
from jax import lax
from jax.experimental import pallas as pl
from jax.experimental.pallas import tpu as pltpu

F32 = jnp.float32
BF16 = jnp.bfloat16
F8 = jnp.float8_e4m3fn
FP8_LIMIT = 240.0

D_MODEL = 1024
HEAD_DIM = 64
SSM_WIDTH = 384
SSM_GROUP_CH = 16
SSM_GROUPS = 24
SSM_STATE = 64
SSM_HALF = SSM_GROUPS * SSM_STATE
DIL_WIDTH = 384
DIL_HEADS = 6
DIL_DILATIONS = (1, 4, 16)
DIFF_WIDTH = 256
DIFF_HEADS = 4
DIFF_QK_DIM = 32
IN_WIDTH = 2304
PEER_HEADS = 8
PEER_KEYS = 128
PEER_TOPK = 16
PEER_EXPERTS = PEER_KEYS * PEER_KEYS
NORM_EPS = 1e-6

LANES = 128
SUBLANES = 8
VMEM_LIMIT_BYTES = 56 * 1024 * 1024

PROJ_TOKENS = 1024
SSM_CHUNK = 128
ATT_BLOCK = 128
DIFF_BLOCK = 512
PEER_TOKENS = 1024
PEER_GATE_CHUNK = LANES
PEER_EXPERT_BLOCK = 1024
PEER_TOKEN_TILE = 256
PEER_SCORE_TOKENS = 256

NEG_INF = float("-inf")


def _rms(x, gain):
    return x * lax.rsqrt(jnp.mean(x * x, axis=-1, keepdims=True) + NORM_EPS) * gain


def _dot_nt(a, b):
    return lax.dot_general(a, b, (((1,), (1,)), ((), ())), preferred_element_type=F32)


def _params(semantics):
    return pltpu.CompilerParams(dimension_semantics=semantics, vmem_limit_bytes=VMEM_LIMIT_BYTES)


def _in_proj_kernel(x_ref, g_ref, w_ref, scale_ref, u_ref, dil_ref, diff_ref):
    h = _rms(x_ref[...], g_ref[...]).astype(BF16)
    d0, d1 = SSM_WIDTH, SSM_WIDTH + 3 * DIL_WIDTH

    def proj(lo, hi):
        return (jnp.dot(h, w_ref[:, lo:hi], preferred_element_type=F32)
                * scale_ref[:, lo:hi])

    u_ref[...] = proj(0, d0).astype(BF16)
    p = proj(d0, d1)
    for t in range(dil_ref.shape[0]):
        dil_ref[t] = p[:, t * LANES:(t + 1) * LANES]
    diff_ref[...] = proj(d1, IN_WIDTH).astype(BF16)


def _in_proj(x2, gain, w_in, col_scale, batch, seq):
    n = batch * seq
    blocks_per_seq = seq // PROJ_TOKENS
    return pl.pallas_call(
        _in_proj_kernel,
        grid=(n // PROJ_TOKENS,),
        in_specs=[
            pl.BlockSpec((PROJ_TOKENS, D_MODEL), lambda i: (i, 0)),
            pl.BlockSpec((1, D_MODEL), lambda i: (0, 0)),
            pl.BlockSpec((D_MODEL, IN_WIDTH), lambda i: (0, 0)),
            pl.BlockSpec((1, IN_WIDTH), lambda i: (0, 0)),
        ],
        out_specs=[
            pl.BlockSpec((PROJ_TOKENS, SSM_WIDTH),
                         lambda i: (i % blocks_per_seq, i // blocks_per_seq)),
            pl.BlockSpec((3 * DIL_WIDTH // LANES, PROJ_TOKENS, LANES), lambda i: (0, i, 0)),
            pl.BlockSpec((PROJ_TOKENS, 3 * DIFF_WIDTH), lambda i: (i, 0)),
        ],
        out_shape=[
            jax.ShapeDtypeStruct((seq, batch * SSM_WIDTH), BF16),
            jax.ShapeDtypeStruct((3 * DIL_WIDTH // LANES, n, LANES), F32),
            jax.ShapeDtypeStruct((n, 3 * DIFF_WIDTH), BF16),
        ],
        compiler_params=_params(("arbitrary",)),
        name="in_proj",
    )(x2, gain, w_in, col_scale)


def _ssm_kernel(u_ref, bmat_ref, cmat_ref, lre_ref, lim_ref, d_ref, wglu_ref, g_ref,
                y_ref, s_ref, state_ref):
    rows = u_ref.shape[0]
    sub = 256

    @pl.when(pl.program_id(0) == 0)
    def _():
        state_ref[...] = jnp.zeros_like(state_ref)

    span = (LANES // SSM_GROUP_CH) * SSM_STATE
    blocks = [(slice(k * LANES, (k + 1) * LANES), slice(off + k * span, off + (k + 1) * span))
              for k in range(SSM_WIDTH // LANES) for off in (0, SSM_HALF)]
    for c in range(rows // sub):
        sl = slice(c * sub, (c + 1) * sub)
        for ch, st in blocks:
            s_ref[sl, st] = jnp.dot(u_ref[sl, ch], bmat_ref[ch, st], preferred_element_type=F32)

    lre = lre_ref[...]
    lim = lim_ref[...]

    def step(t, carry):
        sre, sim = carry
        r0 = pl.multiple_of(t * SUBLANES, SUBLANES)
        bre = s_ref[pl.ds(r0, SUBLANES), :SSM_HALF]
        bim = s_ref[pl.ds(r0, SUBLANES), SSM_HALF:]
        nre = lre * sre - lim * sim + bre
        nim = lre * sim + lim * sre + bim
        s_ref[pl.ds(r0, SUBLANES), :SSM_HALF] = nre
        s_ref[pl.ds(r0, SUBLANES), SSM_HALF:] = nim
        return nre, nim

    sre, sim = lax.fori_loop(0, rows // SUBLANES, step,
                             (state_ref[:, :SSM_HALF], state_ref[:, SSM_HALF:]), unroll=2)
    state_ref[:, :SSM_HALF] = sre
    state_ref[:, SSM_HALF:] = sim

    for c in range(rows // sub):
        sl = slice(c * sub, (c + 1) * sub)
        tiles = []
        for k in range(SSM_WIDTH // LANES):
            (ch, st_re), (_, st_im) = blocks[2 * k], blocks[2 * k + 1]
            tiles.append(
                jnp.dot(s_ref[sl, st_re].astype(BF16), cmat_ref[st_re, ch],
                        preferred_element_type=F32)
                + jnp.dot(s_ref[sl, st_im].astype(BF16), cmat_ref[st_im, ch],
                          preferred_element_type=F32))
        y = jnp.concatenate(tiles, axis=1)
        y = y + d_ref[...] * u_ref[sl, :].astype(F32)
        g = jax.nn.gelu(y)
        z = jnp.dot(g.astype(BF16), wglu_ref[...], preferred_element_type=F32)
        y = g * jax.nn.sigmoid(z)
        y_ref[sl, :] = _rms(y, g_ref[...]).astype(BF16)


def _ssm(u_tm, bmat, cmat, lre, lim, d_skip, w_glu, gain, batch, seq):
    assert batch == SUBLANES, "the S5 scan keeps one batch per sublane"
    rows = SSM_CHUNK * batch
    const = lambda i: (0, 0)
    return pl.pallas_call(
        _ssm_kernel,
        grid=(seq // SSM_CHUNK,),
        in_specs=[
            pl.BlockSpec((rows, SSM_WIDTH), lambda i: (i, 0)),
            pl.BlockSpec((SSM_WIDTH, 2 * SSM_HALF), const),
            pl.BlockSpec((2 * SSM_HALF, SSM_WIDTH), const),
            pl.BlockSpec((SUBLANES, SSM_HALF), const),
            pl.BlockSpec((SUBLANES, SSM_HALF), const),
            pl.BlockSpec((1, SSM_WIDTH), const),
            pl.BlockSpec((SSM_WIDTH, SSM_WIDTH), const),
            pl.BlockSpec((1, SSM_WIDTH), const),
        ],
        out_specs=pl.BlockSpec((rows, SSM_WIDTH), lambda i: (i, 0)),
        out_shape=jax.ShapeDtypeStruct((seq * batch, SSM_WIDTH), BF16),
        scratch_shapes=[
            pltpu.VMEM((rows, 2 * SSM_HALF), F32),
            pltpu.VMEM((SUBLANES, 2 * SSM_HALF), F32),
        ],
        compiler_params=_params(("arbitrary",)),
        name="ssm",
    )(u_tm, bmat, cmat, lre, lim, d_skip, w_glu, gain)


def _dilated_kernel(x_ref, g_ref, o_ref, m_ref, l_ref, acc_ref):
    seq = x_ref.shape[1]
    q_tiles = DIL_WIDTH // LANES
    heads_per_tile = LANES // HEAD_DIM
    m_ref[...] = jnp.full(m_ref.shape, NEG_INF, F32)
    l_ref[...] = jnp.zeros_like(l_ref)
    acc_ref[...] = jnp.zeros_like(acc_ref)
    row = lax.broadcasted_iota(jnp.int32, (ATT_BLOCK, ATT_BLOCK), 0)
    col = lax.broadcasted_iota(jnp.int32, (ATT_BLOCK, ATT_BLOCK), 1)
    lane = lax.broadcasted_iota(jnp.int32, (ATT_BLOCK, LANES), 1)
    cur_ok = col <= row
    spread = [(row == heads_per_tile * t + col // HEAD_DIM).astype(BF16) for t in range(q_tiles)]
    in_head = [col // HEAD_DIM == k for k in range(heads_per_tile)]
    assert heads_per_tile == 2

    def unit(dil, start, prev_start, prev_live):
        rows = pl.ds(start, ATT_BLOCK, stride=dil)
        load = lambda t, r: x_ref[t, r, :].astype(BF16)
        q = [load(t, rows) for t in range(q_tiles)]
        kc = [load(q_tiles + t, rows) for t in range(q_tiles)]
        vc = [load(2 * q_tiles + t, rows) for t in range(q_tiles)]
        has_prev = prev_start is not None
        if has_prev:
            prows = pl.ds(prev_start, ATT_BLOCK, stride=dil)
            kp = [load(q_tiles + t, prows) for t in range(q_tiles)]
            vp = [load(2 * q_tiles + t, prows) for t in range(q_tiles)]
            prev_ok = jnp.logical_and(col >= row, prev_live)
        tile_of = [h // heads_per_tile for h in range(DIL_HEADS)]
        qm = [jnp.where(in_head[h % heads_per_tile], q[tile_of[h]], jnp.zeros((), BF16))
              for h in range(DIL_HEADS)]
        sc_all = [_dot_nt(qm[h], kc[tile_of[h]]) for h in range(DIL_HEADS)]
        if has_prev:
            sp_all = [_dot_nt(qm[h], kp[tile_of[h]]) for h in range(DIL_HEADS)]
        pc_all, pp_all = [], []
        m_new_p = jnp.zeros((ATT_BLOCK, LANES), F32)
        l_new_p = jnp.zeros((ATT_BLOCK, LANES), F32)
        for h in range(DIL_HEADS):
            sc = jnp.where(cur_ok, sc_all[h], NEG_INF)
            m = jnp.max(sc, axis=-1, keepdims=True)
            if has_prev:
                sp = jnp.where(prev_ok, sp_all[h], NEG_INF)
                m = jnp.maximum(m, jnp.max(sp, axis=-1, keepdims=True))
            pc = jnp.exp(sc - m)
            den = jnp.sum(pc, axis=-1, keepdims=True)
            if has_prev:
                pp = jnp.exp(sp - m)
                den = den + jnp.sum(pp, axis=-1, keepdims=True)
                pp_all.append(pp.astype(BF16))
            pc_all.append(pc.astype(BF16))
            m_new_p = jnp.where(lane == h, m, m_new_p)
            l_new_p = jnp.where(lane == h, den, l_new_p)
        m_old = m_ref[rows, :]
        m_new = jnp.maximum(m_old, m_new_p)
        w_old = jnp.exp(m_old - m_new)
        w_pat = jnp.exp(m_new_p - m_new)
        m_ref[rows, :] = m_new
        l_ref[rows, :] = w_old * l_ref[rows, :] + w_pat * l_new_p
        for t in range(q_tiles):
            parts = []
            for h in range(heads_per_tile * t, heads_per_tile * (t + 1)):
                part = jnp.dot(pc_all[h], vc[t], preferred_element_type=F32)
                if has_prev:
                    part = part + jnp.dot(pp_all[h], vp[t], preferred_element_type=F32)
                parts.append(part)
            new = jnp.where(in_head[0], parts[0], parts[1])
            s_old = jnp.dot(w_old.astype(BF16), spread[t], preferred_element_type=F32)
            s_pat = jnp.dot(w_pat.astype(BF16), spread[t], preferred_element_type=F32)
            acc_ref[t, rows, :] = s_old * acc_ref[t, rows, :] + s_pat * new

    for dil in reversed(DIL_DILATIONS):
        sub_len = seq // dil
        nblk = sub_len // ATT_BLOCK
        units = dil * nblk

        def pair(u, _, dil=dil, nblk=nblk):
            for k in range(2):
                idx = 2 * u + k
                res, blk = idx % dil, idx // dil
                start = res + dil * ATT_BLOCK * blk
                if nblk == 1:
                    unit(dil, start, None, None)
                else:
                    prev = res + dil * ATT_BLOCK * jnp.maximum(blk - 1, 0)
                    unit(dil, start, prev, blk > 0)
            return 0

        lax.fori_loop(0, units // 2, pair, 0)

    def finish(blk, _):
        rows = pl.ds(pl.multiple_of(blk * ATT_BLOCK, ATT_BLOCK), ATT_BLOCK)
        inv = jnp.where(lane < DIL_HEADS, 1.0 / l_ref[rows, :], 0.0)
        inv_hi = inv.astype(BF16)
        inv_lo = (inv - inv_hi.astype(F32)).astype(BF16)
        ys = []
        for t in range(q_tiles):
            scale = (jnp.dot(inv_hi, spread[t], preferred_element_type=F32)
                     + jnp.dot(inv_lo, spread[t], preferred_element_type=F32))
            ys.append(acc_ref[t, rows, :] * scale)
        sq = sum(jnp.sum(y * y, axis=-1, keepdims=True) for y in ys)
        norm = lax.rsqrt(sq * (1.0 / DIL_WIDTH) + NORM_EPS)
        for t in range(q_tiles):
            cols = slice(t * LANES, (t + 1) * LANES)
            o_ref[rows, cols] = (ys[t] * norm * g_ref[:, cols]).astype(BF16)
        return 0

    lax.fori_loop(0, seq // ATT_BLOCK, finish, 0)


def _dilated_attention(qkv_tiles, gain, batch, seq):
    n = batch * seq
    n_tiles = 3 * DIL_WIDTH // LANES
    return pl.pallas_call(
        _dilated_kernel,
        grid=(batch,),
        in_specs=[
            pl.BlockSpec((n_tiles, seq, LANES), lambda b: (0, b, 0)),
            pl.BlockSpec((1, DIL_WIDTH), lambda b: (0, 0)),
        ],
        out_specs=pl.BlockSpec((seq, DIL_WIDTH), lambda b: (b, 0)),
        out_shape=jax.ShapeDtypeStruct((n, DIL_WIDTH), BF16),
        scratch_shapes=[
            pltpu.VMEM((seq, LANES), F32),
            pltpu.VMEM((seq, LANES), F32),
            pltpu.VMEM((DIL_WIDTH // LANES, seq, LANES), F32),
        ],
        compiler_params=_params(("arbitrary",)),
        name="dilated_attention",
    )(qkv_tiles, gain)


def _diff_kernel(out_scale, lam_ref, q_ref, k_ref, v_ref, g_ref, o_ref, qt_ref, vt_ref, ot_ref,
                 qm_ref, m_ref, l_ref, acc_ref):
    seq = q_ref.shape[0]
    nblk = seq // DIFF_BLOCK
    lam = lam_ref[0]
    for i in range(nblk):
        rows = slice(i * DIFF_BLOCK, (i + 1) * DIFF_BLOCK)
        qt_ref[i] = q_ref[rows, :].astype(F32).T.astype(BF16)
        vt_ref[i] = v_ref[rows, :].astype(F32).T.astype(BF16)
    krow = lax.broadcasted_iota(jnp.int32, (DIFF_BLOCK, DIFF_BLOCK), 0)
    qcol = lax.broadcasted_iota(jnp.int32, (DIFF_BLOCK, DIFF_BLOCK), 1)
    causal = krow <= qcol
    chan = lax.broadcasted_iota(jnp.int32, (DIFF_WIDTH, DIFF_BLOCK), 0)

    n_maps = 2 * DIFF_HEADS

    def attend(kj, mask):
        k0 = pl.multiple_of(kj * DIFF_BLOCK, DIFF_BLOCK)
        k = k_ref[pl.ds(k0, DIFF_BLOCK), :]
        scores = [jnp.dot(k, qm_ref[c], preferred_element_type=F32) for c in range(n_maps)]
        probs, alphas = [], []
        for c in range(n_maps):
            s = scores[c]
            if mask:
                s = jnp.where(causal, s, NEG_INF)
            m_old = m_ref[c]
            m_new = jnp.maximum(m_old, jnp.max(s, axis=0, keepdims=True))
            alpha = jnp.exp(m_old - m_new)
            p = jnp.exp(s - m_new)
            m_ref[c] = m_new
            l_ref[c] = alpha * l_ref[c] + jnp.sum(p, axis=0, keepdims=True)
            probs.append(p.astype(BF16))
            alphas.append(alpha)
        for c in range(n_maps):
            h = c // 2
            vt = vt_ref[kj, h * HEAD_DIM:(h + 1) * HEAD_DIM, :]
            acc_ref[c] = alphas[c] * acc_ref[c] + jnp.dot(vt, probs[c],
                                                          preferred_element_type=F32)

    def q_block(qi, _):
        qt = qt_ref[qi]
        for c in range(n_maps):
            lo = c * DIFF_QK_DIM
            qm_ref[c] = jnp.where(jnp.logical_and(chan >= lo, chan < lo + DIFF_QK_DIM), qt,
                                  jnp.zeros((), BF16))
        m_ref[...] = jnp.full(m_ref.shape, NEG_INF, F32)
        l_ref[...] = jnp.zeros_like(l_ref)
        acc_ref[...] = jnp.zeros_like(acc_ref)

        def k_block(kj, _):
            attend(kj, False)
            return 0

        lax.fori_loop(0, qi, k_block, 0)
        attend(qi, True)
        for h in range(DIFF_HEADS):
            o = (acc_ref[2 * h] / l_ref[2 * h]
                 - lam * (acc_ref[2 * h + 1] / l_ref[2 * h + 1]))
            o = o * lax.rsqrt(jnp.mean(o * o, axis=0, keepdims=True) + NORM_EPS)
            ot_ref[qi, h * HEAD_DIM:(h + 1) * HEAD_DIM, :] = o * (g_ref[...] * out_scale)
        return 0

    lax.fori_loop(0, nblk, q_block, 0)

    for i in range(nblk):
        o_ref[i * DIFF_BLOCK:(i + 1) * DIFF_BLOCK, :] = ot_ref[i].T.astype(BF16)


def _diff_attention(qkv, lam, subln_g_col, out_scale, batch, seq):
    n = batch * seq
    blk = (seq, DIFF_WIDTH)
    nblk = seq // DIFF_BLOCK
    return pl.pallas_call(
        functools.partial(_diff_kernel, out_scale),
        grid=(batch,),
        in_specs=[
            pl.BlockSpec(memory_space=pltpu.SMEM),
            pl.BlockSpec(blk, lambda b: (b, 0)),
            pl.BlockSpec(blk, lambda b: (b, 1)),
            pl.BlockSpec(blk, lambda b: (b, 2)),
            pl.BlockSpec((HEAD_DIM, 1), lambda b: (0, 0)),
        ],
        out_specs=pl.BlockSpec(blk, lambda b: (b, 0)),
        out_shape=jax.ShapeDtypeStruct((n, DIFF_WIDTH), BF16),
        scratch_shapes=[
            pltpu.VMEM((nblk, DIFF_WIDTH, DIFF_BLOCK), BF16),
            pltpu.VMEM((nblk, DIFF_WIDTH, DIFF_BLOCK), BF16),
            pltpu.VMEM((nblk, DIFF_WIDTH, DIFF_BLOCK), F32),
            pltpu.VMEM((2 * DIFF_HEADS, DIFF_WIDTH, DIFF_BLOCK), BF16),
            pltpu.VMEM((2 * DIFF_HEADS, 1, DIFF_BLOCK), F32),
            pltpu.VMEM((2 * DIFF_HEADS, 1, DIFF_BLOCK), F32),
            pltpu.VMEM((2 * DIFF_HEADS, HEAD_DIM, DIFF_BLOCK), F32),
        ],
        compiler_params=_params(("arbitrary",)),
        name="diff_attention",
    )(lam, qkv, qkv, qkv, subln_g_col)


def _out_proj_kernel(x_ref, ssm_ref, dil_ref, diff_ref, w_ref, y_ref):
    acc = jnp.dot(ssm_ref[...], w_ref[:SSM_WIDTH, :], preferred_element_type=F32)
    acc = acc + jnp.dot(dil_ref[...], w_ref[SSM_WIDTH:SSM_WIDTH + DIL_WIDTH, :],
                        preferred_element_type=F32)
    acc = acc + jnp.dot(diff_ref[...], w_ref[SSM_WIDTH + DIL_WIDTH:, :],
                        preferred_element_type=F32)
    y_ref[...] = x_ref[...] + acc


def _out_proj(x2, y_ssm_tm, y_dil, y_diff, w_out, batch, seq):
    n = batch * seq
    blocks_per_seq = seq // PROJ_TOKENS
    tok = lambda w: pl.BlockSpec((PROJ_TOKENS, w), lambda i: (i, 0))
    return pl.pallas_call(
        _out_proj_kernel,
        grid=(n // PROJ_TOKENS,),
        in_specs=[
            tok(D_MODEL),
            pl.BlockSpec((PROJ_TOKENS, SSM_WIDTH),
                         lambda i: (i % blocks_per_seq, i // blocks_per_seq)),
            tok(DIL_WIDTH),
            tok(DIFF_WIDTH),
            pl.BlockSpec((D_MODEL, D_MODEL), lambda i: (0, 0)),
        ],
        out_specs=tok(D_MODEL),
        out_shape=jax.ShapeDtypeStruct((n, D_MODEL), F32),
        compiler_params=_params(("arbitrary",)),
        name="out_proj",
    )(x2, y_ssm_tm, y_dil, y_diff, w_out)


def _sorting_network(n):
    pairs = []
    p = 1
    while p < n:
        k = p
        while k >= 1:
            for j in range(k % p, n - k, 2 * k):
                for i in range(min(k, n - j - k)):
                    if (i + j) // (2 * p) == (i + j + k) // (2 * p):
                        pairs.append((i + j, i + j + k))
            k //= 2
        p *= 2
    return pairs


_SORT16 = _sorting_network(PEER_TOPK)
_BITONIC16 = [(i, i + s) for s in (8, 4, 2, 1) for i in range(PEER_TOPK) if (i // s) % 2 == 0]


def _top16_desc(vals):
    v = list(vals)
    for a, b in _SORT16:
        hi, lo = jnp.maximum(v[a], v[b]), jnp.minimum(v[a], v[b])
        v[a], v[b] = hi, lo
    for shift in (4, 2, 1):
        other = [pltpu.roll(x, shift, 0) for x in v]
        v = [jnp.maximum(v[k], other[PEER_TOPK - 1 - k]) for k in range(PEER_TOPK)]
        for a, b in _BITONIC16:
            hi, lo = jnp.maximum(v[a], v[b]), jnp.minimum(v[a], v[b])
            v[a], v[b] = hi, lo
    return v


def _peer_gates(sa, sb):
    t = sa.shape[1]
    a_top = _top16_desc([sa[SUBLANES * v:SUBLANES * (v + 1), :] for v in range(PEER_KEYS // SUBLANES)])
    b_top = _top16_desc([sb[SUBLANES * v:SUBLANES * (v + 1), :] for v in range(PEER_KEYS // SUBLANES)])
    sub = lax.broadcasted_iota(jnp.int32, (SUBLANES, t), 0)

    def pack(rows):
        out = rows[0]
        for s in range(1, SUBLANES):
            out = jnp.where(sub == s, rows[s], out)
        return out

    b_lo, b_hi, a_hi = pack(b_top[:8]), pack(b_top[8:]), pack(a_top[8:])
    cands = [a_top[0] + b_lo, a_top[0] + b_hi]
    cands += [a_top[k] + b_lo for k in range(1, 8)]
    cands += [a_hi + b_top[0]]
    valid = [None, None] + [sub < (PEER_TOPK // (k + 1)) for k in range(1, 8)] + [None]
    cands = [c if ok is None else jnp.where(ok, c, NEG_INF) for c, ok in zip(cands, valid)]
    pad = jnp.full((SUBLANES, t), NEG_INF, F32)
    tau = _top16_desc(cands + [pad] * (PEER_TOPK - len(cands)))[PEER_TOPK - 1]
    top = a_top[0] + b_top[0]
    z = jnp.zeros((SUBLANES, t), F32)
    for c in cands:
        z = z + jnp.where(c >= tau, jnp.exp(c - top), 0.0)
    for shift in (4, 2, 1):
        z = z + pltpu.roll(z, shift, 0)
    inv_z = 1.0 / z
    last = PEER_TOPK - 1

    def search(test):
        total = None
        bits = []
        for level, weight in enumerate((8, 4, 2, 1)):
            leaves = [b_top[m] for m in range(weight - 1, last, 2 * weight)]
            for bit in reversed(bits):
                leaves = [jnp.where(bit, leaves[2 * n + 1], leaves[2 * n])
                          for n in range(len(leaves) // 2)]
            bit = test(leaves[0])
            bits.append(bit)
            term = jnp.where(bit, float(weight), 0.0)
            total = term if total is None else total + term
        return total

    wa, count, wb, rank = [], [], [], []
    for v in range(PEER_KEYS // SUBLANES):
        xa = sa[SUBLANES * v:SUBLANES * (v + 1), :]
        xb = sb[SUBLANES * v:SUBLANES * (v + 1), :]
        wa.append(jnp.where(xa >= a_top[last], jnp.exp(xa - a_top[0]), 0.0) * inv_z)
        wb.append(jnp.where(xb >= b_top[last], jnp.exp(xb - b_top[0]), 0.0))
        count.append(search(lambda b: xa + b >= tau)
                     + jnp.where(xa + b_top[last] >= tau, 1.0, 0.0))
        rank.append(search(lambda b: b >= xb) + jnp.where(b_top[last] >= xb, 1.0, 0.0))
    cat = lambda parts: jnp.concatenate(parts, axis=0)
    return cat(wa), cat(count), cat(wb), cat(rank)


def _gelu_tanh(a):
    c = math.sqrt(2.0 / math.pi)
    inner = a * (a * a * (c * 0.044715) + c)
    return (a * 0.5) * (jnp.tanh(inner) + 1.0)


def _peer_kernel(final, scale_ref, x_ref, g_ref, wq_ref, keys_ref, u_ref, vt_ref, fg_ref, y_ref,
                 h_ref, h8_ref, sc_ref, wa_ref, cnt_ref, wb_ref, rank_ref, a0_ref, a1_ref, p_ref,
                 acc_ref):
    h_scale, act_unscale, p_scale, out_unscale = (scale_ref[k] for k in range(4))
    s = pl.program_id(1)
    n_blocks = PEER_EXPERTS // PEER_EXPERT_BLOCK
    tb = x_ref.shape[0]
    n_chunks = tb // PEER_GATE_CHUNK
    rows_per_step = PEER_EXPERT_BLOCK // PEER_KEYS

    @pl.when(s == 0)
    def _():
        h = _rms(x_ref[...], g_ref[...])
        h_ref[...] = h.astype(BF16)
        h8_ref[...] = (h * h_scale).astype(F8)
        acc_ref[...] = jnp.zeros_like(acc_ref)

        chunks_per_stage = PEER_SCORE_TOKENS // PEER_GATE_CHUNK

        def stage(si, _):
            r0 = pl.multiple_of(si * PEER_SCORE_TOKENS, PEER_SCORE_TOKENS)
            hs = h_ref[pl.ds(r0, PEER_SCORE_TOKENS), :]
            qs = [jnp.dot(hs, wq_ref[hd], preferred_element_type=F32).astype(BF16)
                  for hd in range(PEER_HEADS)]
            for hd in range(PEER_HEADS):
                for half in range(2):
                    sc = _dot_nt(keys_ref[2 * hd + half],
                                 qs[hd][:, half * PEER_KEYS:(half + 1) * PEER_KEYS])
                    for c in range(chunks_per_stage):
                        sc_ref[c, 2 * hd + half] = sc[:, c * PEER_GATE_CHUNK:
                                                      (c + 1) * PEER_GATE_CHUNK]

            def unit(k, _):
                c = k // (PEER_HEADS // 2)
                ci = si * chunks_per_stage + c
                for hd_local in range(2):
                    hd = (k % (PEER_HEADS // 2)) * 2 + hd_local
                    wa, count, wb, rank = _peer_gates(sc_ref[c, 2 * hd], sc_ref[c, 2 * hd + 1])
                    wa_ref[ci, hd] = wa * p_scale
                    cnt_ref[ci, hd] = count
                    wb_ref[ci, hd] = wb.astype(BF16)
                    rank_ref[ci, hd] = rank.astype(BF16)
                return 0

            lax.fori_loop(0, chunks_per_stage * PEER_HEADS // 2, unit, 0)
            return 0

        lax.fori_loop(0, tb // PEER_SCORE_TOKENS, stage, 0)

    packed_rows = 2 * SUBLANES
    tiles = PEER_KEYS // packed_rows

    def step(par, activate=True, gate_prev=True):
        a_new, a_old = (a0_ref, a1_ref) if par == 0 else (a1_ref, a0_ref)
        gate_block = s - 1
        for t0 in range(0, tb, PEER_TOKEN_TILE):
            tok = slice(t0, t0 + PEER_TOKEN_TILE)
            if activate:
                a_new[:, tok] = (_dot_nt(u_ref[...], h8_ref[tok, :]) * act_unscale).astype(BF16)
            if not gate_prev:
                continue
            for ci in range(t0 // PEER_GATE_CHUNK, (t0 + PEER_TOKEN_TILE) // PEER_GATE_CHUNK):
                cols = slice(ci * PEER_GATE_CHUNK, (ci + 1) * PEER_GATE_CHUNK)
                for il in range(rows_per_step):
                    i = gate_block * rows_per_step + il
                    rows = slice(il * PEER_KEYS, (il + 1) * PEER_KEYS)
                    gate = jnp.zeros((tiles, packed_rows, PEER_GATE_CHUNK), BF16)
                    for hd in range(PEER_HEADS):
                        row = lambda ref: jnp.broadcast_to(
                            ref[ci, hd, pl.ds(i, 1), :],
                            (packed_rows, PEER_GATE_CHUNK)).astype(BF16)[None]
                        rank = rank_ref[ci, hd].reshape(tiles, packed_rows, PEER_GATE_CHUNK)
                        wb = wb_ref[ci, hd].reshape(tiles, packed_rows, PEER_GATE_CHUNK)
                        gate = gate + jnp.where(rank <= row(cnt_ref), wb * row(wa_ref),
                                                jnp.zeros((), BF16))
                    gate = gate.reshape(PEER_KEYS, PEER_GATE_CHUNK)
                    p_ref[rows, cols] = (gate * _gelu_tanh(a_old[rows, cols])).astype(F8)
            acc_ref[:, tok] += jnp.dot(vt_ref[...], p_ref[:, tok], preferred_element_type=F32)

    assert n_blocks % 2 == 0
    last = n_blocks
    inner = jnp.logical_and(s > 0, s < last)

    @pl.when(s == 0)
    def _():
        step(0, gate_prev=False)

    @pl.when(jnp.logical_and(inner, s % 2 == 0))
    def _():
        step(0)

    @pl.when(jnp.logical_and(inner, s % 2 == 1))
    def _():
        step(1)

    @pl.when(s == last)
    def _():
        step(0, activate=False)
        y = x_ref[...] + acc_ref[...].T * out_unscale
        if final:
            y = _rms(y, fg_ref[...])
        y_ref[...] = y


def _peer(x2, gain, wq_heads, keys, u_q, vt_q, scales, final_gain, final):
    n = x2.shape[0]
    n_chunks = PEER_TOKENS // PEER_GATE_CHUNK
    n_blocks = PEER_EXPERTS // PEER_EXPERT_BLOCK
    gate_shape = (n_chunks, PEER_HEADS, PEER_KEYS, PEER_GATE_CHUNK)
    act_block = lambda s: jnp.minimum(s, n_blocks - 1)
    out_block = lambda s: jnp.maximum(s - 1, 0)
    once = pl.Buffered(1)
    return pl.pallas_call(
        functools.partial(_peer_kernel, final),
        grid=(n // PEER_TOKENS, n_blocks + 1),
        in_specs=[
            pl.BlockSpec(memory_space=pltpu.SMEM),
            pl.BlockSpec((PEER_TOKENS, D_MODEL), lambda t, s: (t, 0), pipeline_mode=once),
            pl.BlockSpec((1, D_MODEL), lambda t, s: (0, 0)),
            pl.BlockSpec((PEER_HEADS, D_MODEL, 2 * PEER_KEYS), lambda t, s: (0, 0, 0),
                         pipeline_mode=once),
            pl.BlockSpec((2 * PEER_HEADS, PEER_KEYS, PEER_KEYS), lambda t, s: (0, 0, 0)),
            pl.BlockSpec((PEER_EXPERT_BLOCK, D_MODEL), lambda t, s: (act_block(s), 0)),
            pl.BlockSpec((None, D_MODEL, PEER_EXPERT_BLOCK), lambda t, s: (out_block(s), 0, 0)),
            pl.BlockSpec((1, D_MODEL), lambda t, s: (0, 0)),
        ],
        out_specs=pl.BlockSpec((PEER_TOKENS, D_MODEL), lambda t, s: (t, 0)),
        out_shape=jax.ShapeDtypeStruct((n, D_MODEL), F32),
        scratch_shapes=[
            pltpu.VMEM((PEER_TOKENS, D_MODEL), BF16),
            pltpu.VMEM((PEER_TOKENS, D_MODEL), F8),
            pltpu.VMEM((PEER_SCORE_TOKENS // PEER_GATE_CHUNK, 2 * PEER_HEADS, PEER_KEYS,
                        PEER_GATE_CHUNK), F32),
            pltpu.VMEM(gate_shape, F32),
            pltpu.VMEM(gate_shape, F32),
            pltpu.VMEM(gate_shape, BF16),
            pltpu.VMEM(gate_shape, BF16),
            pltpu.VMEM((PEER_EXPERT_BLOCK, PEER_TOKENS), BF16),
            pltpu.VMEM((PEER_EXPERT_BLOCK, PEER_TOKENS), BF16),
            pltpu.VMEM((PEER_EXPERT_BLOCK, PEER_TOKENS), F8),
            pltpu.VMEM((D_MODEL, PEER_TOKENS), F32),
        ],
        compiler_params=_params(("arbitrary", "arbitrary")),
        name="peer",
    )(scales, x2, gain, wq_heads, keys, u_q, vt_q, final_gain)


def _ssm_matrices(lam_re, lam_im, log_step, b_re, b_im, c_re, c_im):
    lam = lax.complex(lam_re, lam_im)
    step = jnp.exp(log_step)[:, None]
    lam_bar = jnp.exp(lam * step)
    b_bar = ((lam_bar - 1.0) / lam)[:, :, None] * lax.complex(b_re, b_im)
    eye = jnp.eye(SSM_GROUPS, dtype=F32)

    def embed_in(m):
        return jnp.einsum("gpc,gh->gchp", m, eye).reshape(SSM_WIDTH, SSM_HALF)

    def embed_out(m):
        return jnp.einsum("gcp,gh->gphc", m, eye).reshape(SSM_HALF, SSM_WIDTH)

    bmat = jnp.concatenate([embed_in(jnp.real(b_bar)), embed_in(jnp.imag(b_bar))], axis=1)
    cmat = jnp.concatenate([embed_out(c_re), embed_out(-c_im)], axis=0)
    lre = jnp.broadcast_to(jnp.real(lam_bar).reshape(1, SSM_HALF), (SUBLANES, SSM_HALF))
    lim = jnp.broadcast_to(jnp.imag(lam_bar).reshape(1, SSM_HALF), (SUBLANES, SSM_HALF))
    return bmat.astype(BF16), cmat.astype(BF16), lre, lim


def _pow2_scale(bound):
    return jnp.exp2(jnp.floor(jnp.log2(FP8_LIMIT / jnp.maximum(bound, 1e-30))))


def _peer_tables(u_exp, v_exp, gain):
    h_bound = math.sqrt(D_MODEL) * jnp.max(jnp.abs(gain))
    u_norm = jnp.sqrt(jnp.max(jnp.sum(u_exp * u_exp, axis=1)))
    u_scale = _pow2_scale(u_norm)
    v_scale = _pow2_scale(jnp.max(jnp.abs(v_exp)))
    h_scale = _pow2_scale(h_bound)
    p_scale = _pow2_scale(PEER_HEADS * u_norm * h_bound)
    u_q = (u_exp * u_scale).astype(F8)
    blocks = v_exp.reshape(PEER_EXPERTS // PEER_EXPERT_BLOCK, PEER_EXPERT_BLOCK, D_MODEL)
    vt_q = (blocks.transpose(0, 2, 1) * v_scale).astype(F8)
    scales = jnp.stack([h_scale, 1.0 / (h_scale * u_scale), p_scale, 1.0 / (p_scale * v_scale)])
    return u_q, vt_q, scales.astype(F32)


def _in_proj_col_scale():
    s = jnp.ones((IN_WIDTH,), F32)
    s = s.at[SSM_WIDTH:SSM_WIDTH + DIL_WIDTH].set(HEAD_DIM ** -0.5)
    d0 = SSM_WIDTH + 3 * DIL_WIDTH
    s = s.at[d0:d0 + DIFF_WIDTH].set(DIFF_QK_DIM ** -0.5)
    return s.reshape(1, IN_WIDTH)


def kernel(x, norm1_g, w_in, ssm_lam_re, ssm_lam_im, ssm_log_step, ssm_b_re, ssm_b_im, ssm_c_re, ssm_c_im, ssm_d, ssm_w_glu, ssm_norm_g, dil_norm_g, diff_lam_q1, diff_lam_k1, diff_lam_q2, diff_lam_k2, diff_subln_g, w_out, norm2_g, peer_w_query, peer_sub_keys, peer_u, peer_v, final_norm_g):
    batch, seq, _ = x.shape
    n = batch * seq
    depth = w_in.shape[0]
    x2 = x.reshape(n, D_MODEL)
    col_scale = _in_proj_col_scale()
    row = lambda v: v.reshape(1, -1)
    for layer in range(depth):
        lambda_init = 0.8 - 0.6 * math.exp(-0.3 * layer)
        u_tm, dil_qkv, diff_qkv = _in_proj(x2, row(norm1_g[layer]), w_in[layer].astype(BF16),
                                           col_scale, batch, seq)
        bmat, cmat, lre, lim = _ssm_matrices(
            ssm_lam_re[layer], ssm_lam_im[layer], ssm_log_step[layer], ssm_b_re[layer],
            ssm_b_im[layer], ssm_c_re[layer], ssm_c_im[layer])
        y_ssm = _ssm(u_tm.reshape(seq * batch, SSM_WIDTH), bmat, cmat, lre, lim,
                     row(ssm_d[layer]), ssm_w_glu[layer].astype(BF16), row(ssm_norm_g[layer]),
                     batch, seq)
        y_dil = _dilated_attention(dil_qkv, row(dil_norm_g[layer]), batch, seq)
        lam = (jnp.exp(jnp.sum(diff_lam_q1[layer] * diff_lam_k1[layer]))
               - jnp.exp(jnp.sum(diff_lam_q2[layer] * diff_lam_k2[layer])) + lambda_init)
        y_diff = _diff_attention(diff_qkv, lam.reshape(1), diff_subln_g[layer].reshape(-1, 1),
                                 1.0 - lambda_init, batch, seq)
        x2 = _out_proj(x2, y_ssm.reshape(seq, batch * SSM_WIDTH), y_dil, y_diff,
                       w_out[layer].astype(BF16), batch, seq)
        wq_heads = peer_w_query[layer].reshape(D_MODEL, PEER_HEADS, 2 * PEER_KEYS)
        wq_heads = wq_heads.transpose(1, 0, 2).astype(BF16)
        keys = peer_sub_keys[layer].reshape(2 * PEER_HEADS, PEER_KEYS, PEER_KEYS).astype(BF16)
        u_q, vt_q, scales = _peer_tables(peer_u[layer], peer_v[layer], norm2_g[layer])
        x2 = _peer(x2, row(norm2_g[layer]), wq_heads, keys, u_q, vt_q, scales,
                   row(final_norm_g), layer == depth - 1)
    return x2.reshape(batch, seq, D_MODEL)
```
